```python
import math
import jax, jax.numpy as jnp
from jax import lax
import numpy as np

D_MODEL = 2048
BATCH = 1
SEQ = 8192
DEPTH = 2

GRID_W = 64
CTX_LEN = 256
EPS = 1e-6

A_HEADS = 8
A_HEAD_DIM = 64
A_V_DIM = 2 * A_HEAD_DIM
A_WIDTH = A_HEADS * A_V_DIM
QK_COLS = A_HEADS * 2 * A_HEAD_DIM
Q_BLOCK = 128
ROPE_THETA = 10000.0

B_GROUPS = 8
B_GROUP_W = 128
B_WIDTH = B_GROUPS * B_GROUP_W
CHUNK = 128

IN_COLS = 2 * QK_COLS + A_WIDTH + 2 * B_WIDTH
MIX_WIDTH = A_WIDTH + B_WIDTH

C_GROUPS = 4
C_GROUP_W = D_MODEL // C_GROUPS

N_EXPERTS = 16
CAPACITY_FACTOR = 2
F_EXPERT = D_MODEL // 2

N_EVEN = (DEPTH + 1) // 2
N_ODD = DEPTH // 2

kernel_name = "hybrid_diffattn_gmlp_fnet_ecmoe_dit"


def rms_norm(x, g):
    xf = x.astype(jnp.float32)
    y = xf * lax.rsqrt(jnp.mean(xf * xf, -1, keepdims=True) + EPS)
    return (y * g.astype(jnp.float32)).astype(x.dtype)


def layer_norm(x, g, b):
    xf = x.astype(jnp.float32)
    mu = jnp.mean(xf, -1, keepdims=True)
    var = jnp.mean(jnp.square(xf - mu), -1, keepdims=True)
    y = (xf - mu) * lax.rsqrt(var + EPS) * g.astype(jnp.float32) + b.astype(jnp.float32)
    return y.astype(x.dtype)


def modulate(h, shift, scale):
    return h * (1.0 + scale) + shift


def _rope_1d(x, pos):
    dim = x.shape[-1]
    inv = ROPE_THETA ** (-jnp.arange(0, dim, 2, dtype=jnp.float32) / dim)
    ang = pos.astype(jnp.float32)[:, None] * inv[None, :]
    cos = jnp.cos(ang)[None, :, None, None, :]
    sin = jnp.sin(ang)[None, :, None, None, :]
    x1, x2 = x[..., : dim // 2], x[..., dim // 2:]
    return jnp.concatenate([x1 * cos - x2 * sin, x2 * cos + x1 * sin], -1)


def axial_rope(x, row, col):
    half = A_HEAD_DIM // 2
    xf = x.astype(jnp.float32)
    out = jnp.concatenate([_rope_1d(xf[..., :half], row), _rope_1d(xf[..., half:], col)], -1)
    return out.astype(x.dtype)


def diff_attend(q, k, v, lam):
    s = jnp.einsum('bqhcd,bkhcd->bhcqk', q.astype(jnp.float32), k.astype(jnp.float32)) * (A_HEAD_DIM ** -0.5)
    p = jax.nn.softmax(s, axis=-1)
    a = p[:, :, 0] - lam * p[:, :, 1]
    return jnp.einsum('bhqk,bkhe->bqhe', a, v.astype(jnp.float32))


def split_projection(h, w_in):
    B_, n, _ = h.shape
    z = h @ w_in
    q = z[..., :QK_COLS].reshape(B_, n, A_HEADS, 2, A_HEAD_DIM)
    k = z[..., QK_COLS:2 * QK_COLS].reshape(B_, n, A_HEADS, 2, A_HEAD_DIM)
    v = z[..., 2 * QK_COLS:2 * QK_COLS + A_WIDTH].reshape(B_, n, A_HEADS, A_V_DIM)
    uv = jax.nn.gelu(z[..., 2 * QK_COLS + A_WIDTH:])
    return q, k, v, uv


def spatial_gate(uv, ln_g, ln_b, w_s, b_s):
    B_, n, _ = uv.shape
    u, v = uv[..., :B_WIDTH], uv[..., B_WIDTH:]
    v = v.reshape(B_, n // CHUNK, CHUNK, B_GROUPS, B_GROUP_W)
    v = layer_norm(v, ln_g.reshape(B_GROUPS, B_GROUP_W), ln_b.reshape(B_GROUPS, B_GROUP_W))
    mixed = jnp.einsum('gpq,bcqgd->bcpgd', w_s, v) + b_s.T[:, :, None]
    return u * mixed.reshape(B_, n, B_WIDTH)


def even_mixer(h_lat, h_ctx, row, col, w_in, w_out, lq1, lk1, lq2, lk2, g_subln,
               ln_g, ln_b, w_s, b_s, lam_init, need_ctx_out):
    f32 = jnp.float32
    lam = (jnp.exp(jnp.sum(lq1.astype(f32) * lk1.astype(f32)))
           - jnp.exp(jnp.sum(lq2.astype(f32) * lk2.astype(f32))) + lam_init)
    q_l, k_l, v_l, uv_l = split_projection(h_lat, w_in)
    q_c, k_c, v_c, uv_c = split_projection(h_ctx, w_in)
    q_l = axial_rope(q_l, row, col)
    k_l = axial_rope(k_l, row, col)
    k_all = jnp.concatenate([k_c, k_l], axis=1)
    v_all = jnp.concatenate([v_c, v_l], axis=1)
    B_, n = h_lat.shape[0], h_lat.shape[1]
    nb = n // Q_BLOCK
    qb = jnp.moveaxis(q_l.reshape(B_, nb, Q_BLOCK, A_HEADS, 2, A_HEAD_DIM), 1, 0)
    o = lax.map(lambda qi: diff_attend(qi, k_all, v_all, lam), qb)
    o = jnp.moveaxis(o, 0, 1).reshape(B_, n, A_HEADS, A_V_DIM)

    def finish(o_attn, uv):
        a = rms_norm(o_attn, g_subln) * (1.0 - lam_init)
        a = a.reshape(a.shape[0], a.shape[1], A_WIDTH).astype(uv.dtype)
        g = spatial_gate(uv, ln_g, ln_b, w_s, b_s)
        return jnp.concatenate([a, g], axis=-1) @ w_out

    y_lat = finish(o, uv_l)
    y_ctx = finish(diff_attend(q_c, k_c, v_c, lam), uv_c) if need_ctx_out else None
    return y_lat, y_ctx


def fourier_mix(h, w_o):
    B_, n, _ = h.shape
    hg = h.astype(jnp.float32).reshape(B_, n, C_GROUPS, C_GROUP_W)
    f = jnp.fft.fft2(hg, axes=(1, 3), norm="ortho").real
    return f.reshape(B_, n, D_MODEL).astype(h.dtype) @ w_o


def expert_choice_ffn(h, w_r, w_gate, w_up, w_down):
    B_, n, D = h.shape
    cap = CAPACITY_FACTOR * n // N_EXPERTS
    aff = jax.nn.softmax((h @ w_r).astype(jnp.float32), axis=-1)
    g, idx = lax.top_k(jnp.swapaxes(aff, 1, 2), cap)
    xs = jax.vmap(lambda hb, ib: hb[ib])(h, idx)
    a = jnp.einsum('becd,edf->becf', xs, w_gate)
    u = jnp.einsum('becd,edf->becf', xs, w_up)
    y = jnp.einsum('becf,efd->becd', jax.nn.silu(a) * u, w_down) * g[..., None].astype(h.dtype)
    return jax.vmap(lambda yb, ib: jnp.zeros((n, D), yb.dtype).at[ib.reshape(-1)].add(yb.reshape(-1, D)))(y, idx)


def setup_inputs(seed: int = 0) -> dict:
    key = jax.random.key(seed)
    ks = jax.random.split(key, 32)
    nrm = jax.random.normal
    D = D_MODEL
    return {
        "x": nrm(ks[0], (BATCH, SEQ, D), jnp.float32),
        "c": nrm(ks[1], (BATCH, D), jnp.float32),
        "ctx": nrm(ks[2], (BATCH, CTX_LEN, D), jnp.float32),
        "c_ctx": nrm(ks[3], (D,), jnp.float32),
        "w_mod": nrm(ks[4], (DEPTH, D, 6 * D), jnp.float32) * (0.5 * D ** -0.5),
        "b_mod": nrm(ks[5], (DEPTH, 6 * D), jnp.float32) * 0.02,
        "g_norm_mix": 1.0 + 0.02 * nrm(ks[6], (DEPTH, D), jnp.float32),
        "g_norm_ffn": 1.0 + 0.02 * nrm(ks[7], (DEPTH, D), jnp.float32),
        "w_in": nrm(ks[8], (N_EVEN, D, IN_COLS), jnp.float32) * D ** -0.5,
        "w_out": nrm(ks[9], (N_EVEN, MIX_WIDTH, D), jnp.float32) * MIX_WIDTH ** -0.5,
        "lam_q1": nrm(ks[10], (N_EVEN, A_HEAD_DIM), jnp.float32) * 0.1,
        "lam_k1": nrm(ks[11], (N_EVEN, A_HEAD_DIM), jnp.float32) * 0.1,
        "lam_q2": nrm(ks[12], (N_EVEN, A_HEAD_DIM), jnp.float32) * 0.1,
        "lam_k2": nrm(ks[13], (N_EVEN, A_HEAD_DIM), jnp.float32) * 0.1,
        "g_subln": 1.0 + 0.02 * nrm(ks[14], (N_EVEN, A_V_DIM), jnp.float32),
        "sgu_ln_g": 1.0 + 0.02 * nrm(ks[15], (N_EVEN, B_WIDTH), jnp.float32),
        "sgu_ln_b": 0.02 * nrm(ks[16], (N_EVEN, B_WIDTH), jnp.float32),
        "w_spatial": nrm(ks[17], (N_EVEN, B_GROUPS, CHUNK, CHUNK), jnp.float32) * CHUNK ** -0.5,
        "b_spatial": 1.0 + 0.02 * nrm(ks[18], (N_EVEN, B_GROUPS, CHUNK), jnp.float32),
        "w_fourier_out": nrm(ks[19], (N_ODD, D, D), jnp.float32) * D ** -0.5,
        "w_router": nrm(ks[20], (DEPTH, D, N_EXPERTS), jnp.float32) * D ** -0.5,
        "w_gate": nrm(ks[21], (DEPTH, N_EXPERTS, D, F_EXPERT), jnp.float32) * D ** -0.5,
        "w_up": nrm(ks[22], (DEPTH, N_EXPERTS, D, F_EXPERT), jnp.float32) * D ** -0.5,
        "w_down": nrm(ks[23], (DEPTH, N_EXPERTS, F_EXPERT, D), jnp.float32) * F_EXPERT ** -0.5,
        "g_final": 1.0 + 0.02 * nrm(ks[24], (D,), jnp.float32),
    }


def reference(x, c, ctx, c_ctx, w_mod, b_mod, g_norm_mix, g_norm_ffn, w_in, w_out,
              lam_q1, lam_k1, lam_q2, lam_k2, g_subln, sgu_ln_g, sgu_ln_b, w_spatial,
              b_spatial, w_fourier_out, w_router, w_gate, w_up, w_down, g_final):
    n = x.shape[1]
    rows = n // GRID_W
    row = jnp.repeat(jnp.arange(rows, dtype=jnp.int32), GRID_W)
    col = jnp.tile(jnp.arange(GRID_W, dtype=jnp.int32), rows)

    for i in range(DEPTH):
        last = i == DEPTH - 1
        j = i // 2
        m_lat = (jax.nn.silu(c) @ w_mod[i] + b_mod[i])[:, None, :]
        m_ctx = (jax.nn.silu(c_ctx) @ w_mod[i] + b_mod[i])[None, None, :]
        sm, cm, gm, sf, cf, gf = jnp.split(m_lat, 6, axis=-1)
        smc, cmc, gmc, sfc, cfc, gfc = jnp.split(m_ctx, 6, axis=-1)

        h_lat = modulate(rms_norm(x, g_norm_mix[i]), sm, cm)
        if i % 2 == 0:
            lam_init = 0.8 - 0.6 * math.exp(-0.3 * i)
            h_ctx = modulate(rms_norm(ctx, g_norm_mix[i]), smc, cmc)
            y_lat, y_ctx = even_mixer(h_lat, h_ctx, row, col, w_in[j], w_out[j],
                                      lam_q1[j], lam_k1[j], lam_q2[j], lam_k2[j], g_subln[j],
                                      sgu_ln_g[j], sgu_ln_b[j], w_spatial[j], b_spatial[j],
                                      lam_init, not last)
        else:
            y_lat = fourier_mix(h_lat, w_fourier_out[j])
            y_ctx = None if last else fourier_mix(modulate(rms_norm(ctx, g_norm_mix[i]), smc, cmc), w_fourier_out[j])

        x = x + gm * y_lat
        x = x + gf * expert_choice_ffn(modulate(rms_norm(x, g_norm_ffn[i]), sf, cf),
                                       w_router[i], w_gate[i], w_up[i], w_down[i])
        if not last:
            ctx = ctx + gmc * y_ctx
            ctx = ctx + gfc * expert_choice_ffn(modulate(rms_norm(ctx, g_norm_ffn[i]), sfc, cfc),
                                                w_router[i], w_gate[i], w_up[i], w_down[i])

    return rms_norm(x, g_final)
```

```python
import functools
import math

import numpy as np
import jax
import jax.numpy as jnp
from jax import lax
from jax.experimental import pallas as pl
from jax.experimental.pallas import tpu as pltpu

F32 = jnp.float32
BF16 = jnp.bfloat16
I32 = jnp.int32

D_MODEL = 2048
DEPTH = 2
GRID_W = 64
EPS = 1e-6

A_HEADS = 8
A_HEAD_DIM = 64
A_V_DIM = 2 * A_HEAD_DIM
A_WIDTH = A_HEADS * A_V_DIM
QK_COLS = A_HEADS * 2 * A_HEAD_DIM
ROPE_THETA = 10000.0

B_GROUPS = 8
B_GROUP_W = 128
B_WIDTH = B_GROUPS * B_GROUP_W
CHUNK = 128
IN_COLS = 2 * QK_COLS + A_WIDTH + 2 * B_WIDTH

C_GROUPS = 4
C_GROUP_W = D_MODEL // C_GROUPS

N_EXPERTS = 16
CAPACITY_FACTOR = 2
F_EXPERT = D_MODEL // 2

LANES = 128
MXU_DIM = 256
VMEM_LIMIT_BYTES = 56 * 1024 * 1024

FFT_N2 = 32
ROUTE_TILE = 256
ROUTE_WIN = 256


def _params(sem):
    return pltpu.CompilerParams(dimension_semantics=sem, vmem_limit_bytes=VMEM_LIMIT_BYTES)


def _nt_dot(a, b):
    return lax.dot_general(a, b, (((1,), (1,)), ((), ())), preferred_element_type=F32)


def _norm_mod(x, g, shift, scale):
    ms = jnp.mean(x * x, axis=-1, keepdims=True)
    y = x * lax.rsqrt(ms + EPS) * g
    return y * (1.0 + scale) + shift


def _mod_kernel(c_ref, w_ref, b_ref, o_ref):
    c = c_ref[...]
    s = c * jax.nn.sigmoid(c)
    o_ref[...] = jnp.dot(s.astype(BF16), w_ref[...].astype(BF16), preferred_element_type=F32) + b_ref[...]


def _modulation(c8, w_mod, b_mod):
    depth, d, n6 = w_mod.shape
    tn = 1024
    return pl.pallas_call(
        _mod_kernel,
        grid=(depth, n6 // tn),
        in_specs=[
            pl.BlockSpec((8, d), lambda l, j: (0, 0)),
            pl.BlockSpec((None, d, tn), lambda l, j: (l, 0, j)),
            pl.BlockSpec((None, 1, tn), lambda l, j: (l, 0, j)),
        ],
        out_specs=pl.BlockSpec((None, 8, tn), lambda l, j: (l, 0, j)),
        out_shape=jax.ShapeDtypeStruct((depth, 8, n6), F32),
        compiler_params=_params(("arbitrary", "arbitrary")),
        name="modulation",
    )(c8, w_mod, b_mod.reshape(depth, 1, n6))


PROJ_TN = 512
_Q_TILES = QK_COLS // PROJ_TN
_V_TILES = A_WIDTH // PROJ_TN
_UV_TILES = 2 * B_WIDTH // PROJ_TN


def _rope_chunk(z, cos, sin_signed, first_half):
    partner = jnp.where(first_half, pltpu.roll(z, LANES - 16, 1), pltpu.roll(z, 16, 1))
    return z * cos + partner * sin_signed


def _proj_kernel(x_ref, g_ref, sh_ref, sc_ref, w_ref, wvt_ref, cos_ref, sin_ref,
                 q_ref, k_ref, vt_ref, uv_ref, h_scr):
    j = pl.program_id(1)

    @pl.when(j == 0)
    def _():
        h_scr[...] = _norm_mod(x_ref[...], g_ref[...], sh_ref[...], sc_ref[...]).astype(BF16)

    def project():
        return jnp.dot(h_scr[...], w_ref[...], preferred_element_type=F32)

    def roped(scale):
        z = project()
        cos = cos_ref[...]
        sin = sin_ref[...]
        lane = lax.broadcasted_iota(I32, cos.shape, 1)
        first_half = (lane % 32) < 16
        parts = []
        for c in range(PROJ_TN // LANES):
            zc = z[:, c * LANES:(c + 1) * LANES]
            parts.append(_rope_chunk(zc, cos, sin, first_half) * scale)
        return jnp.concatenate(parts, axis=1)

    @pl.when(j < _Q_TILES)
    def _():
        q_ref[...] = roped(A_HEAD_DIM ** -0.5).astype(BF16)

    @pl.when((j >= _Q_TILES) & (j < 2 * _Q_TILES))
    def _():
        k_ref[...] = roped(1.0).astype(BF16)

    @pl.when((j >= 2 * _Q_TILES) & (j < 2 * _Q_TILES + _V_TILES))
    def _():
        vt_ref[...] = _nt_dot(wvt_ref[...], h_scr[...]).astype(BF16)

    @pl.when(j >= 2 * _Q_TILES + _V_TILES)
    def _():
        uv_ref[...] = jax.nn.gelu(project())


def _in_projection(x, g, shift, scale, w_in_bf16, w_v_t_bf16, cos_t, sin_t):
    n, d = x.shape
    tm = min(1024, n)
    nq = _Q_TILES
    nj = IN_COLS // PROJ_TN
    row = lambda i, j: (0, 0)
    v_tile = lambda j: jnp.clip(j - 2 * nq, 0, _V_TILES - 1)
    return pl.pallas_call(
        _proj_kernel,
        grid=(n // tm, nj),
        in_specs=[
            pl.BlockSpec((tm, d), lambda i, j: (i, 0)),
            pl.BlockSpec((1, d), row), pl.BlockSpec((1, d), row), pl.BlockSpec((1, d), row),
            pl.BlockSpec((d, PROJ_TN), lambda i, j: (0, j)),
            pl.BlockSpec((PROJ_TN, d), lambda i, j: (v_tile(j), 0)),
            pl.BlockSpec((tm, LANES), lambda i, j: (i, 0)),
            pl.BlockSpec((tm, LANES), lambda i, j: (i, 0)),
        ],
        out_specs=[
            pl.BlockSpec((tm, PROJ_TN), lambda i, j: (i, jnp.clip(j, 0, nq - 1))),
            pl.BlockSpec((tm, PROJ_TN), lambda i, j: (i, jnp.clip(j - nq, 0, nq - 1))),
            pl.BlockSpec((PROJ_TN, tm), lambda i, j: (v_tile(j), i)),
            pl.BlockSpec((tm, PROJ_TN), lambda i, j: (i, jnp.clip(j - 2 * nq - _V_TILES, 0, _UV_TILES - 1))),
        ],
        out_shape=[
            jax.ShapeDtypeStruct((n, QK_COLS), BF16),
            jax.ShapeDtypeStruct((n, QK_COLS), BF16),
            jax.ShapeDtypeStruct((A_WIDTH, n), BF16),
            jax.ShapeDtypeStruct((n, 2 * B_WIDTH), F32),
        ],
        scratch_shapes=[pltpu.VMEM((tm, d), BF16)],
        compiler_params=_params(("arbitrary", "arbitrary")),
        name="in_projection",
    )(x, g, shift, scale, w_in_bf16, w_v_t_bf16, cos_t, sin_t)


def _rope_tables(n, rotate):
    if not rotate:
        return jnp.ones((n, LANES), F32), jnp.zeros((n, LANES), F32)
    t = np.arange(n)
    row = (t // GRID_W).astype(np.float32)
    col = (t % GRID_W).astype(np.float32)
    dim = A_HEAD_DIM // 2
    inv = (np.float32(ROPE_THETA) ** (-np.arange(0, dim, 2, dtype=np.float32) / np.float32(dim))).astype(np.float32)
    ang_r = row[:, None] * inv[None, :]
    ang_c = col[:, None] * inv[None, :]
    ang64 = np.concatenate([ang_r, ang_r, ang_c, ang_c], axis=1)
    sign64 = np.concatenate([-np.ones(16), np.ones(16), -np.ones(16), np.ones(16)]).astype(np.float32)
    ang = np.tile(ang64, (1, LANES // 64))
    sign = np.tile(sign64, LANES // 64)
    return jnp.asarray(np.cos(ang), F32), jnp.asarray(np.sin(ang) * sign[None, :], F32)


def _attn_kernel(q_ref, k_ref, vt_ref, lamv_ref, gs_ref, o_ref, m_scr, l_scr, acc_scr, *, tk, lam_init):
    tq = q_ref.shape[0]
    nk = k_ref.shape[0]
    qt = q_ref[...].astype(F32).T
    row = lax.broadcasted_iota(I32, qt.shape, 0)
    zero = jnp.zeros_like(qt)
    qst = jnp.concatenate([jnp.where(row < A_HEAD_DIM, qt, zero),
                           jnp.where(row >= A_HEAD_DIM, qt, zero)], axis=1).astype(BF16)
    m_scr[...] = jnp.full(m_scr.shape, -jnp.inf, F32)
    l_scr[...] = jnp.zeros(l_scr.shape, F32)
    acc_scr[...] = jnp.zeros(acc_scr.shape, F32)

    def body(j, carry):
        off = pl.multiple_of(j * tk, tk)
        kb = k_ref[pl.ds(off, tk), :]
        vtb = vt_ref[:, pl.ds(off, tk)]
        s = jnp.dot(kb, qst, preferred_element_type=F32)
        m_old = m_scr[...]
        m_new = jnp.maximum(m_old, jnp.max(s, axis=0, keepdims=True))
        alpha = jnp.exp(m_old - m_new)
        p = jnp.exp(s - m_new)
        l_scr[...] = alpha * l_scr[...] + jnp.sum(p, axis=0, keepdims=True)
        acc_scr[...] = alpha * acc_scr[...] + jnp.dot(vtb, p.astype(BF16), preferred_element_type=F32)
        m_scr[...] = m_new
        return carry

    lax.fori_loop(0, nk // tk, body, 0)

    lv = lamv_ref[...]
    lam = (jnp.exp(jnp.sum(lv[0:1] * lv[1:2], axis=-1, keepdims=True))
           - jnp.exp(jnp.sum(lv[2:3] * lv[3:4], axis=-1, keepdims=True)) + lam_init)
    ot = acc_scr[...] / l_scr[...]
    o = (ot[:, :tq] - lam * ot[:, tq:]).T
    a = o * lax.rsqrt(jnp.mean(o * o, axis=-1, keepdims=True) + EPS) * gs_ref[...]
    o_ref[...] = (a * (1.0 - lam_init)).astype(BF16)


def _pick_tile(n, candidates):
    for c in candidates:
        if n % c == 0:
            return c
    raise ValueError(f"no tile for {n}")


def _diff_attention(q, k_all, vt_all, lamv, g_subln, lam_init):
    n = q.shape[0]
    nk = k_all.shape[0]
    tq = 256
    tk = _pick_tile(nk, (768, 512, 256))
    return pl.pallas_call(
        functools.partial(_attn_kernel, tk=tk, lam_init=lam_init),
        grid=(A_HEADS, n // tq),
        in_specs=[
            pl.BlockSpec((tq, A_V_DIM), lambda h, i: (i, h)),
            pl.BlockSpec((nk, A_V_DIM), lambda h, i: (0, h)),
            pl.BlockSpec((A_V_DIM, nk), lambda h, i: (h, 0)),
            pl.BlockSpec((8, LANES), lambda h, i: (0, 0)),
            pl.BlockSpec((1, A_V_DIM), lambda h, i: (0, 0)),
        ],
        out_specs=pl.BlockSpec((tq, A_V_DIM), lambda h, i: (i, h)),
        out_shape=jax.ShapeDtypeStruct((n, A_WIDTH), BF16),
        scratch_shapes=[pltpu.VMEM((1, 2 * tq), F32), pltpu.VMEM((1, 2 * tq), F32),
                        pltpu.VMEM((A_V_DIM, 2 * tq), F32)],
        compiler_params=_params(("arbitrary", "arbitrary")),
        name="diff_attention",
    )(q, k_all, vt_all, lamv, g_subln)


def _finish_kernel(a_ref, uv_ref, x_ref, wout_ref, ws_ref, bs_ref, lng_ref, lnb_ref, gm_ref, o_ref, cat_scr):
    tm = a_ref.shape[0]
    cat_scr[:, :A_WIDTH] = a_ref[...]
    for c in range(tm // CHUNK):
        rows = slice(c * CHUNK, (c + 1) * CHUNK)
        for g in range(B_GROUPS):
            cols = slice(g * B_GROUP_W, (g + 1) * B_GROUP_W)
            u = uv_ref[rows, g * B_GROUP_W:(g + 1) * B_GROUP_W]
            v = uv_ref[rows, B_WIDTH + g * B_GROUP_W:B_WIDTH + (g + 1) * B_GROUP_W]
            mu = jnp.mean(v, axis=-1, keepdims=True)
            var = jnp.mean(jnp.square(v - mu), axis=-1, keepdims=True)
            vn = (v - mu) * lax.rsqrt(var + EPS) * lng_ref[:, cols] + lnb_ref[:, cols]
            mixed = jnp.dot(ws_ref[g], vn.astype(BF16), preferred_element_type=F32) + bs_ref[g]
            cat_scr[rows, A_WIDTH + g * B_GROUP_W:A_WIDTH + (g + 1) * B_GROUP_W] = (u * mixed).astype(BF16)
    y = jnp.dot(cat_scr[...], wout_ref[...], preferred_element_type=F32)
    o_ref[...] = x_ref[...] + gm_ref[...] * y


def _finish_even(a, uv, x, w_out_bf16, ws_bf16, bs_b, ln_g, ln_b, gm):
    n, d = x.shape
    tm = min(512, n)
    row = lambda i: (0, 0)
    return pl.pallas_call(
        _finish_kernel,
        grid=(n // tm,),
        in_specs=[
            pl.BlockSpec((tm, A_WIDTH), lambda i: (i, 0)),
            pl.BlockSpec((tm, 2 * B_WIDTH), lambda i: (i, 0)),
            pl.BlockSpec((tm, d), lambda i: (i, 0)),
            pl.BlockSpec((A_WIDTH + B_WIDTH, d), row),
            pl.BlockSpec((B_GROUPS, CHUNK, CHUNK), lambda i: (0, 0, 0)),
            pl.BlockSpec((B_GROUPS, CHUNK, B_GROUP_W), lambda i: (0, 0, 0)),
            pl.BlockSpec((1, B_WIDTH), row), pl.BlockSpec((1, B_WIDTH), row),
            pl.BlockSpec((1, d), row),
        ],
        out_specs=pl.BlockSpec((tm, d), lambda i: (i, 0)),
        out_shape=jax.ShapeDtypeStruct((n, d), F32),
        scratch_shapes=[pltpu.VMEM((tm, A_WIDTH + B_WIDTH), BF16)],
        compiler_params=_params(("arbitrary",)),
        name="finish_even",
    )(a, uv, x, w_out_bf16, ws_bf16, bs_b, ln_g, ln_b, gm)


def _router_kernel(x_ref, g_ref, sh_ref, sc_ref, wrt_ref, ht_ref, aff_ref):
    h = _norm_mod(x_ref[...], g_ref[...], sh_ref[...], sc_ref[...])
    logits = _nt_dot(wrt_ref[...], h.astype(BF16))
    m = jnp.max(logits, axis=0, keepdims=True)
    e = jnp.exp(logits - m)
    aff_ref[...] = e / jnp.sum(e, axis=0, keepdims=True)
    ht_ref[...] = h.T.astype(BF16)


def _router(x, g, shift, scale, w_router_t_bf16):
    n, d = x.shape
    tm = min(512, n)
    row = lambda i: (0, 0)
    return pl.pallas_call(
        _router_kernel,
        grid=(n // tm,),
        in_specs=[
            pl.BlockSpec((tm, d), lambda i: (i, 0)),
            pl.BlockSpec((1, d), row), pl.BlockSpec((1, d), row), pl.BlockSpec((1, d), row),
            pl.BlockSpec((N_EXPERTS, d), row),
        ],
        out_specs=[pl.BlockSpec((d, tm), lambda i: (0, i)), pl.BlockSpec((N_EXPERTS, tm), lambda i: (0, i))],
        out_shape=[jax.ShapeDtypeStruct((d, n), BF16), jax.ShapeDtypeStruct((N_EXPERTS, n), F32)],
        compiler_params=_params(("arbitrary",)),
        name="router",
    )(x, g, shift, scale, w_router_t_bf16)


def _select_kernel(aff_ref, sel_ref, pos_ref, g_ref, *, cap):
    aff = aff_ref[...]
    n = aff.shape[1]
    idx = lax.broadcasted_iota(I32, aff.shape, 1)
    capf = float(cap)

    def count(mask):
        return jnp.sum(jnp.where(mask, 1.0, 0.0), axis=1, keepdims=True)

    def as_float(bits):
        return pltpu.bitcast(bits, F32)

    def thr_body(i, thr):
        cand = thr | jnp.left_shift(jnp.int32(1), 30 - i)
        return jnp.where(count(aff >= as_float(cand)) >= capf, cand, thr)

    thr = lax.fori_loop(0, 31, thr_body, jnp.zeros((aff.shape[0], 1), I32))
    gt = aff >= as_float(thr + 1)
    eq = (aff >= as_float(thr)) & jnp.logical_not(gt)
    need = capf - count(gt)
    nbits = int(n).bit_length()

    def tie_body(i, lim):
        cand = lim | jnp.left_shift(jnp.int32(1), nbits - 1 - i)
        return jnp.where(count(eq & (idx < cand)) <= need, cand, lim)

    lim = lax.fori_loop(0, nbits, tie_body, jnp.zeros((aff.shape[0], 1), I32))
    sel = jnp.where(gt | (eq & (idx < lim)), 1.0, 0.0)
    sel_ref[...] = sel
    g_ref[...] = aff * sel

    ri = lax.broadcasted_iota(I32, (LANES, LANES), 0)
    ci = lax.broadcasted_iota(I32, (LANES, LANES), 1)
    upper = jnp.where(ri < ci, 1.0, 0.0).astype(BF16)
    carry = jnp.zeros((aff.shape[0], 1), F32)
    for c in range(n // LANES):
        m = sel[:, c * LANES:(c + 1) * LANES]
        within = jnp.dot(m.astype(BF16), upper, preferred_element_type=F32)
        pos_ref[:, c * LANES:(c + 1) * LANES] = (within + carry).astype(I32)
        carry = carry + jnp.sum(m, axis=1, keepdims=True)


def _select(aff_t, cap):
    e, n = aff_t.shape
    full = lambda: (0, 0)
    return pl.pallas_call(
        functools.partial(_select_kernel, cap=cap),
        grid=(),
        in_specs=[pl.BlockSpec((e, n), full)],
        out_specs=[pl.BlockSpec((e, n), full)] * 3,
        out_shape=[jax.ShapeDtypeStruct((e, n), F32), jax.ShapeDtypeStruct((e, n), I32),
                   jax.ShapeDtypeStruct((e, n), F32)],
        compiler_params=pltpu.CompilerParams(vmem_limit_bytes=VMEM_LIMIT_BYTES),
        name="expert_select",
    )(aff_t)


def _one_hot_window(pos_row, sel_row, lo, base):
    rel = jnp.where((sel_row > 0.0) & (pos_row >= lo), pos_row - base, -1)
    r = lax.broadcasted_iota(I32, (ROUTE_WIN, pos_row.shape[1]), 0)
    return jnp.where(r == rel, 1.0, 0.0).astype(BF16)


def _window_plan(p0, p1):
    start = (p0 // LANES) * LANES
    n_chunks = jnp.where(p1 > p0, (p1 - start + ROUTE_WIN - 1) // ROUTE_WIN, 0)
    return start, n_chunks


def _dispatch_kernel(ps_ref, pe_ref, ht_ref, sel_ref, pos_ref, g_ref, xs_ref, gslot_ref, xst_scr, gs_scr, *, tiles_per_step):
    e = pl.program_id(0)
    c = pl.program_id(1)
    capp = xst_scr.shape[1]

    @pl.when(c == 0)
    def _():
        xst_scr[...] = jnp.zeros(xst_scr.shape, F32)
        gs_scr[...] = jnp.zeros(gs_scr.shape, F32)

    def add_window(tt, k):
        tile = c * tiles_per_step + tt
        start, _ = _window_plan(ps_ref[e, tile], pe_ref[e, tile])
        cols = slice(tt * ROUTE_TILE, (tt + 1) * ROUTE_TILE)
        g_row = g_ref[0, :, cols]
        g_hi = g_row.astype(BF16)
        r1 = g_row - g_hi.astype(F32)
        g_mid = r1.astype(BF16)
        g_lo = (r1 - g_mid.astype(F32)).astype(BF16)
        g8 = jnp.concatenate([g_hi, g_mid, g_lo, jnp.zeros((5, ROUTE_TILE), BF16)], axis=0)
        lo = start + k * ROUTE_WIN
        base = pl.multiple_of(jnp.minimum(lo, capp - ROUTE_WIN), LANES)
        onehot = _one_hot_window(pos_ref[0, :, cols], sel_ref[0, :, cols], lo, base)
        xst_scr[:, pl.ds(base, ROUTE_WIN)] += _nt_dot(ht_ref[:, cols], onehot)
        gs_scr[:, pl.ds(base, ROUTE_WIN)] += _nt_dot(g8, onehot)

    for tt in range(tiles_per_step):
        add_window(tt, 0)
    for tt in range(tiles_per_step):
        tile = c * tiles_per_step + tt
        n_chunks = _window_plan(ps_ref[e, tile], pe_ref[e, tile])[1]
        lax.fori_loop(1, n_chunks, lambda k, carry, tt=tt: (add_window(tt, k), carry)[1], 0)

    @pl.when(c == pl.num_programs(1) - 1)
    def _():
        xs_ref[0] = xst_scr[...].T.astype(BF16)
        gs = gs_scr[...]
        gslot_ref[0] = gs[0:1] + gs[1:2] + gs[2:3]


def _dispatch(tile_start, tile_end, h_t, sel3, pos3, g3, capp):
    d, n = h_t.shape
    tc = min(1024, n)
    tiles_per_step = tc // ROUTE_TILE
    row3 = lambda e, c, ps, pe: (e, 0, c)
    return pl.pallas_call(
        functools.partial(_dispatch_kernel, tiles_per_step=tiles_per_step),
        grid_spec=pltpu.PrefetchScalarGridSpec(
            num_scalar_prefetch=2,
            grid=(N_EXPERTS, n // tc),
            in_specs=[
                pl.BlockSpec((d, tc), lambda e, c, ps, pe: (0, c)),
                pl.BlockSpec((1, 1, tc), row3), pl.BlockSpec((1, 1, tc), row3), pl.BlockSpec((1, 1, tc), row3),
            ],
            out_specs=[
                pl.BlockSpec((1, capp, d), lambda e, c, ps, pe: (e, 0, 0)),
                pl.BlockSpec((1, 1, capp), lambda e, c, ps, pe: (e, 0, 0)),
            ],
            scratch_shapes=[pltpu.VMEM((d, capp), F32), pltpu.VMEM((8, capp), F32)],
        ),
        out_shape=[jax.ShapeDtypeStruct((N_EXPERTS, capp, d), BF16),
                   jax.ShapeDtypeStruct((N_EXPERTS, 1, capp), F32)],
        compiler_params=_params(("arbitrary", "arbitrary")),
        name="moe_dispatch",
    )(tile_start, tile_end, h_t, sel3, pos3, g3)


def _ffn_up_kernel(xs_ref, wg_ref, wu_ref, h_ref):
    xs = xs_ref[0]
    a = jnp.dot(xs, wg_ref[0].astype(BF16), preferred_element_type=F32)
    u = jnp.dot(xs, wu_ref[0].astype(BF16), preferred_element_type=F32)
    h_ref[0] = (a * jax.nn.sigmoid(a) * u).astype(BF16)


def _ffn_up(xs, w_gate, w_up, layer):
    e, capp, d = xs.shape
    f = w_gate.shape[3]
    tf = 512
    return pl.pallas_call(
        _ffn_up_kernel,
        grid=(e, f // tf),
        in_specs=[
            pl.BlockSpec((1, capp, d), lambda i, j: (i, 0, 0)),
            pl.BlockSpec((None, 1, d, tf), lambda i, j: (layer, i, 0, j)),
            pl.BlockSpec((None, 1, d, tf), lambda i, j: (layer, i, 0, j)),
        ],
        out_specs=pl.BlockSpec((1, capp, tf), lambda i, j: (i, 0, j)),
        out_shape=jax.ShapeDtypeStruct((e, capp, f), BF16),
        compiler_params=_params(("arbitrary", "arbitrary")),
        name="moe_ffn_up",
    )(xs, w_gate, w_up)


def _ffn_down_kernel(h_ref, wd_ref, gslot_ref, yh_ref, yl_ref):
    y = jnp.dot(h_ref[0], wd_ref[0].astype(BF16), preferred_element_type=F32)
    yw = y.T * gslot_ref[0]
    hi = yw.astype(BF16)
    yh_ref[0] = hi
    yl_ref[0] = (yw - hi.astype(F32)).astype(BF16)


def _ffn_down(h, w_down, gslot, layer):
    e, capp, f = h.shape
    d = w_down.shape[3]
    td = 512
    shp = jax.ShapeDtypeStruct((e, d, capp), BF16)
    return pl.pallas_call(
        _ffn_down_kernel,
        grid=(e, d // td),
        in_specs=[
            pl.BlockSpec((1, capp, f), lambda i, j: (i, 0, 0)),
            pl.BlockSpec((None, 1, f, td), lambda i, j: (layer, i, 0, j)),
            pl.BlockSpec((1, 1, capp), lambda i, j: (i, 0, 0)),
        ],
        out_specs=[pl.BlockSpec((1, td, capp), lambda i, j: (i, j, 0))] * 2,
        out_shape=[shp, shp],
        compiler_params=_params(("arbitrary", "arbitrary")),
        name="moe_ffn_down",
    )(h, w_down, gslot)


COMBINE_TD = 512


def _combine_kernel(ps_ref, pe_ref, yh_ref, yl_ref, sel_ref, pos_ref, o_ref, *, tiles_per_iter):
    e = pl.program_id(1)
    capp = yh_ref.shape[2]
    n_tiles = sel_ref.shape[1]

    @pl.when(e == 0)
    def _():
        o_ref[...] = jnp.zeros(o_ref.shape, F32)

    def add_window(t, k):
        start, _ = _window_plan(ps_ref[e, t], pe_ref[e, t])
        pos_row = pos_ref[0, pl.ds(t, 1), :]
        sel_row = sel_ref[0, pl.ds(t, 1), :]
        lo = start + k * ROUTE_WIN
        base = pl.multiple_of(jnp.minimum(lo, capp - ROUTE_WIN), LANES)
        onehot = _one_hot_window(pos_row, sel_row, lo, base)
        y2 = jnp.concatenate([yh_ref[0, :, pl.ds(base, ROUTE_WIN)], yl_ref[0, :, pl.ds(base, ROUTE_WIN)]], axis=1)
        cols = pl.ds(pl.multiple_of(t * ROUTE_TILE, ROUTE_TILE), ROUTE_TILE)
        o_ref[:, cols] += jnp.dot(y2, jnp.concatenate([onehot, onehot], axis=0), preferred_element_type=F32)

    def body(i, carry):
        tiles = [i * tiles_per_iter + u for u in range(tiles_per_iter)]
        for t in tiles:
            add_window(t, 0)
        for t in tiles:
            n_chunks = _window_plan(ps_ref[e, t], pe_ref[e, t])[1]
            lax.fori_loop(1, n_chunks, lambda k, c, t=t: (add_window(t, k), c)[1], 0)
        return carry

    lax.fori_loop(0, n_tiles // tiles_per_iter, body, 0)


def _combine(tile_start, tile_end, yh, yl, sel_t, pos_t, n):
    e, d, capp = yh.shape
    n_tiles = n // ROUTE_TILE
    td = COMBINE_TD
    return pl.pallas_call(
        functools.partial(_combine_kernel, tiles_per_iter=4 if n_tiles % 4 == 0 else 1),
        grid_spec=pltpu.PrefetchScalarGridSpec(
            num_scalar_prefetch=2,
            grid=(d // td, e),
            in_specs=[
                pl.BlockSpec((1, td, capp), lambda j, i, ps, pe: (i, j, 0)),
                pl.BlockSpec((1, td, capp), lambda j, i, ps, pe: (i, j, 0)),
                pl.BlockSpec((1, n_tiles, ROUTE_TILE), lambda j, i, ps, pe: (i, 0, 0)),
                pl.BlockSpec((1, n_tiles, ROUTE_TILE), lambda j, i, ps, pe: (i, 0, 0)),
            ],
            out_specs=pl.BlockSpec((td, n), lambda j, i, ps, pe: (j, 0)),
        ),
        out_shape=jax.ShapeDtypeStruct((d, n), F32),
        compiler_params=_params(("arbitrary", "arbitrary")),
        name="moe_combine",
    )(tile_start, tile_end, yh, yl, sel_t, pos_t)


def _expert_choice_ffn(x, g, shift, scale, w_router, w_gate, w_up, w_down, layer):
    n, d = x.shape
    cap = CAPACITY_FACTOR * n // N_EXPERTS
    capp = max(cap, ROUTE_WIN)
    h_t, aff_t = _router(x, g, shift, scale, w_router.T.astype(BF16))
    sel, pos, gsel = _select(aff_t, cap)
    tile_start = pos[:, ::ROUTE_TILE]
    tile_end = jnp.concatenate([tile_start[:, 1:], jnp.full((N_EXPERTS, 1), cap, I32)], axis=1)
    as3 = lambda a: a.reshape(N_EXPERTS, 1, n)
    xs, gslot = _dispatch(tile_start, tile_end, h_t, as3(sel), as3(pos), as3(gsel), capp)
    h = _ffn_up(xs, w_gate, w_up, layer)
    yh, yl = _ffn_down(h, w_down, gslot, layer)
    as_tiles = lambda a: a.reshape(N_EXPERTS, n // ROUTE_TILE, ROUTE_TILE)
    return _combine(tile_start, tile_end, yh, yl, as_tiles(sel), as_tiles(pos), n)


def _resid_kernel(x_ref, yt_ref, gate_ref, o_ref):
    o_ref[...] = x_ref[...] + gate_ref[...] * yt_ref[...].T


def _residual(x, y_t, gate):
    n, d = x.shape
    tm = min(512, n)
    return pl.pallas_call(
        _resid_kernel,
        grid=(n // tm,),
        in_specs=[pl.BlockSpec((tm, d), lambda i: (i, 0)), pl.BlockSpec((d, tm), lambda i: (0, i)),
                  pl.BlockSpec((1, d), lambda i: (0, 0))],
        out_specs=pl.BlockSpec((tm, d), lambda i: (i, 0)),
        out_shape=jax.ShapeDtypeStruct((n, d), F32),
        compiler_params=_params(("arbitrary",)),
        name="residual",
    )(x, y_t, gate)


def _final_kernel(x_ref, yt_ref, gate_ref, g_ref, o_ref):
    x = x_ref[...] + gate_ref[...] * yt_ref[...].T
    o_ref[...] = x * lax.rsqrt(jnp.mean(x * x, axis=-1, keepdims=True) + EPS) * g_ref[...]


def _final_norm(x, y_t, gate, g_final):
    n, d = x.shape
    tm = min(512, n)
    row = lambda i: (0, 0)
    return pl.pallas_call(
        _final_kernel,
        grid=(n // tm,),
        in_specs=[pl.BlockSpec((tm, d), lambda i: (i, 0)), pl.BlockSpec((d, tm), lambda i: (0, i)),
                  pl.BlockSpec((1, d), row), pl.BlockSpec((1, d), row)],
        out_specs=pl.BlockSpec((tm, d), lambda i: (i, 0)),
        out_shape=jax.ShapeDtypeStruct((n, d), F32),
        compiler_params=_params(("arbitrary",)),
        name="final_norm",
    )(x, y_t, gate, g_final)


FFT_K1_BLK = 16


def _fourier_in_kernel(x_ref, yt_ref, gate_ref, g_ref, sh_ref, sc_ref, cs_ref, x1_ref, a_ref, b_ref, ab_scr):
    x1 = x_ref[...] + gate_ref[...] * yt_ref[...].T
    x1_ref[...] = x1
    h = _norm_mod(x1, g_ref[...], sh_ref[...], sc_ref[...]).astype(BF16)
    t1_blk = a_ref.shape[1]
    lane_tiles = C_GROUP_W // LANES
    for g in range(C_GROUPS):
        ab = jnp.dot(h[:, g * C_GROUP_W:(g + 1) * C_GROUP_W], cs_ref[...], preferred_element_type=F32)
        for c in range(2 * lane_tiles):
            ab_scr[c] = ab[:, c * LANES:(c + 1) * LANES]

        def regroup(t2, carry):
            for c in range(lane_tiles):
                cols = slice(g * C_GROUP_W + c * LANES, g * C_GROUP_W + (c + 1) * LANES)
                a_ref[t2, :, cols] = ab_scr[c, pl.ds(t2, t1_blk, stride=FFT_N2), :].astype(BF16)
                b_ref[t2, :, cols] = ab_scr[lane_tiles + c, pl.ds(t2, t1_blk, stride=FFT_N2), :].astype(BF16)
            return carry

        lax.fori_loop(0, FFT_N2, regroup, 0)


def _fourier_in(x, moe_t, gate, g, shift, scale, cs):
    n, d = x.shape
    tm = 512
    n1 = n // FFT_N2
    t1_blk = tm // FFT_N2
    row = lambda i: (0, 0)
    tile = pl.BlockSpec((tm, d), lambda i: (i, 0))
    tile_t = pl.BlockSpec((d, tm), lambda i: (0, i))
    ab_blk = pl.BlockSpec((FFT_N2, t1_blk, d), lambda i: (0, i, 0))
    ab_shape = jax.ShapeDtypeStruct((FFT_N2, n1, d), BF16)
    return pl.pallas_call(
        _fourier_in_kernel,
        grid=(n // tm,),
        in_specs=[tile, tile_t, pl.BlockSpec((1, d), row), pl.BlockSpec((1, d), row), pl.BlockSpec((1, d), row),
                  pl.BlockSpec((1, d), row), pl.BlockSpec((C_GROUP_W, 2 * C_GROUP_W), row)],
        out_specs=[tile, ab_blk, ab_blk],
        out_shape=[jax.ShapeDtypeStruct((n, d), F32), ab_shape, ab_shape],
        scratch_shapes=[pltpu.VMEM((2 * C_GROUP_W // LANES, tm, LANES), F32)],
        compiler_params=_params(("arbitrary",)),
        name="fourier_channel_dft",
    )(x, moe_t, gate, g, shift, scale, cs)


def _fourier_stage1_kernel(a_ref, b_ref, ma_ref, mb_ref, ct_ref, st_ref, zr_ref, zi_ref):
    n1 = a_ref.shape[1]
    z = (jnp.dot(ma_ref[...], a_ref[0], preferred_element_type=F32)
         + jnp.dot(mb_ref[...], b_ref[0], preferred_element_type=F32))
    ct = ct_ref[0]
    st = st_ref[0]
    for c in range(a_ref.shape[2] // LANES):
        cols = slice(c * LANES, (c + 1) * LANES)
        zr = z[:n1, cols]
        zi = z[n1:, cols]
        zr_ref[0, :, cols] = (zr * ct + zi * st).astype(BF16)
        zi_ref[0, :, cols] = (zi * ct - zr * st).astype(BF16)


def _fourier_stage1(a3, b3, ma, mb, ct, st):
    n2, n1, d = a3.shape
    blk = pl.BlockSpec((1, n1, d), lambda j: (j, 0, 0))
    mat = pl.BlockSpec((2 * n1, n1), lambda j: (0, 0))
    tw = pl.BlockSpec((1, n1, LANES), lambda j: (j, 0, 0))
    shp = jax.ShapeDtypeStruct((n2, n1, d), BF16)
    return pl.pallas_call(
        _fourier_stage1_kernel,
        grid=(n2,),
        in_specs=[blk, blk, mat, mat, tw, tw],
        out_specs=[blk, blk],
        out_shape=[shp, shp],
        compiler_params=_params(("arbitrary",)),
        name="fourier_stage1",
    )(a3, b3, ma, mb, ct, st)


def _fourier_out_kernel(zr_ref, zi_ref, bc_ref, bs_ref, wo_ref, x_ref, gm_ref, o_ref):
    rows = zr_ref.shape[0] * zr_ref.shape[1]
    d = zr_ref.shape[2]
    zr = zr_ref[...].reshape(rows, d)
    zi = zi_ref[...].reshape(rows, d)
    f = (jnp.dot(bc_ref[...], zr, preferred_element_type=F32)
         + jnp.dot(bs_ref[...], zi, preferred_element_type=F32))
    y = jnp.dot(f.astype(BF16), wo_ref[...], preferred_element_type=F32)
    o_ref[...] = x_ref[...] + gm_ref[...] * y.reshape(o_ref.shape)


def _fourier_out(zr3, zi3, bd_c, bd_s, w_o_bf16, x, gm):
    n2, n1, d = zr3.shape
    n = n1 * n2
    rows = n2 * FFT_K1_BLK
    x3 = x.reshape(n2, n1, d)
    blk = pl.BlockSpec((n2, FFT_K1_BLK, d), lambda i: (0, i, 0))
    const = lambda i: (0, 0)
    out = pl.pallas_call(
        _fourier_out_kernel,
        grid=(n1 // FFT_K1_BLK,),
        in_specs=[blk, blk, pl.BlockSpec((rows, rows), const), pl.BlockSpec((rows, rows), const),
                  pl.BlockSpec((d, d), const), blk, pl.BlockSpec((1, d), const)],
        out_specs=blk,
        out_shape=jax.ShapeDtypeStruct((n2, n1, d), F32),
        compiler_params=_params(("arbitrary",)),
        name="fourier_stage2_out",
    )(zr3, zi3, bd_c, bd_s, w_o_bf16, x3, gm)
    return out.reshape(n, d)


def _fourier_constants(n):
    n1, n2 = n // FFT_N2, FFT_N2
    two_pi = 2.0 * np.pi

    def angles(a, b, period):
        return two_pi * ((np.outer(a, b) % period).astype(np.float64) / period)

    kc = np.arange(C_GROUP_W)
    ang = angles(kc, kc, C_GROUP_W)
    cs = np.concatenate([np.cos(ang), np.sin(ang)], axis=1) / np.sqrt(C_GROUP_W)
    k1 = np.arange(n1)
    ang1 = angles(k1, k1, n1)
    c1, s1 = np.cos(ang1) / np.sqrt(n), np.sin(ang1) / np.sqrt(n)
    ma = np.concatenate([c1, -s1], axis=0)
    mb = np.concatenate([-s1, -c1], axis=0)
    t2 = np.arange(n2)
    angt = angles(t2, k1, n)
    ct = np.repeat(np.cos(angt)[:, :, None], LANES, axis=2)
    st = np.repeat(np.sin(angt)[:, :, None], LANES, axis=2)
    ang2 = angles(t2, t2, n2)
    k1_blk = FFT_K1_BLK
    bd_c = np.zeros((n2 * k1_blk, n2 * k1_blk))
    bd_s = np.zeros((n2 * k1_blk, n2 * k1_blk))
    for kl in range(k1_blk):
        bd_c[kl::k1_blk, kl::k1_blk] = np.cos(ang2)
        bd_s[kl::k1_blk, kl::k1_blk] = np.sin(ang2)
    bf = lambda a: jnp.asarray(a, F32).astype(BF16)
    return bf(cs), bf(ma), bf(mb), jnp.asarray(ct, F32), jnp.asarray(st, F32), bf(bd_c), bf(bd_s)


def _fourier_mix_layer(x, moe_t, gate, g, shift, scale, gm, w_o):
    cs, ma, mb, ct, st, bd_c, bd_s = _fourier_constants(x.shape[0])
    x1, a3, b3 = _fourier_in(x, moe_t, gate, g, shift, scale, cs)
    zr3, zi3 = _fourier_stage1(a3, b3, ma, mb, ct, st)
    return _fourier_out(zr3, zi3, bd_c, bd_s, w_o.astype(BF16), x1, gm)


def _even_layer_mix(x, ctx, mods, g_mix, w_in, w_out, lamv, g_subln, ln_g, ln_b, w_s, b_s, lam_init, need_ctx_out):
    sm, cm, gm = mods["lat"][0:3]
    smc, cmc, gmc = mods["ctx"][0:3]
    n = x.shape[0]
    w_in_b = w_in.astype(BF16)
    w_out_b = w_out.astype(BF16)
    ws_b = w_s.astype(BF16)
    bs_b = jnp.broadcast_to(b_s[:, :, None], (B_GROUPS, CHUNK, B_GROUP_W))
    cos_l, sin_l = _rope_tables(n, True)
    cos_c, sin_c = _rope_tables(ctx.shape[0], False)
    w_v_t = w_in[:, 2 * QK_COLS:2 * QK_COLS + A_WIDTH].T.astype(BF16)
    q_l, k_l, vt_l, uv_l = _in_projection(x, g_mix, sm, cm, w_in_b, w_v_t, cos_l, sin_l)
    q_c, k_c, vt_c, uv_c = _in_projection(ctx, g_mix, smc, cmc, w_in_b, w_v_t, cos_c, sin_c)
    k_all = jnp.concatenate([k_c, k_l], axis=0)
    vt_all = jnp.concatenate([vt_c, vt_l], axis=1)
    a_l = _diff_attention(q_l, k_all, vt_all, lamv, g_subln, lam_init)
    x = _finish_even(a_l, uv_l, x, w_out_b, ws_b, bs_b, ln_g, ln_b, gm)
    if need_ctx_out:
        a_c = _diff_attention(q_c, k_c, vt_c, lamv, g_subln, lam_init)
        ctx = _finish_even(a_c, uv_c, ctx, w_out_b, ws_b, bs_b, ln_g, ln_b, gmc)
    return x, ctx


def kernel(x, c, ctx, c_ctx, w_mod, b_mod, g_norm_mix, g_norm_ffn, w_in, w_out, lam_q1, lam_k1, lam_q2, lam_k2,
           g_subln, sgu_ln_g, sgu_ln_b, w_spatial, b_spatial, w_fourier_out, w_router, w_gate, w_up, w_down, g_final):
    assert x.shape[0] == 1 and DEPTH == 2
    d = D_MODEL
    x2 = x[0]
    ctx2 = ctx[0]
    c8 = jnp.zeros((8, d), F32).at[0].set(c[0]).at[1].set(c_ctx)
    mod_all = _modulation(c8, w_mod, b_mod)
    row = lambda v: v.reshape(1, -1)

    def mods_of(i):
        lat = [mod_all[i, 0:1, k * d:(k + 1) * d] for k in range(6)]
        cx = [mod_all[i, 1:2, k * d:(k + 1) * d] for k in range(6)]
        return {"lat": lat, "ctx": cx}

    m0 = mods_of(0)
    lam_init0 = 0.8 - 0.6 * math.exp(-0.3 * 0)
    lamv = jnp.zeros((8, LANES), F32)
    for r, v in enumerate((lam_q1[0], lam_k1[0], lam_q2[0], lam_k2[0])):
        lamv = lamv.at[r, :A_HEAD_DIM].set(v)
    x2, ctx2 = _even_layer_mix(x2, ctx2, m0, row(g_norm_mix[0]), w_in[0], w_out[0], lamv, row(g_subln[0]),
                               row(sgu_ln_g[0]), row(sgu_ln_b[0]), w_spatial[0], b_spatial[0], lam_init0, True)
    moe_x = _expert_choice_ffn(x2, row(g_norm_ffn[0]), m0["lat"][3], m0["lat"][4],
                               w_router[0], w_gate, w_up, w_down, 0)
    moe_c = _expert_choice_ffn(ctx2, row(g_norm_ffn[0]), m0["ctx"][3], m0["ctx"][4],
                               w_router[0], w_gate, w_up, w_down, 0)
    ctx2 = _residual(ctx2, moe_c, m0["ctx"][5])

    m1 = mods_of(1)
    x2 = _fourier_mix_layer(x2, moe_x, m0["lat"][5], row(g_norm_mix[1]), m1["lat"][0], m1["lat"][1],
                            m1["lat"][2], w_fourier_out[0])
    moe_x = _expert_choice_ffn(x2, row(g_norm_ffn[1]), m1["lat"][3], m1["lat"][4],
                               w_router[1], w_gate, w_up, w_down, 1)
    out = _final_norm(x2, moe_x, m1["lat"][5], row(g_final))
    del ctx2
    return out[None]
```

```python
import functools
import math

import numpy as np
import jax
import jax.numpy as jnp
from jax import lax
from jax.experimental import pallas as pl
from jax.experimental.pallas import tpu as pltpu

F32 = jnp.float32
BF16 = jnp.bfloat16
I32 = jnp.int32

D_MODEL = 2048
DEPTH = 2
GRID_W = 64
EPS = 1e-6

A_HEADS = 8
A_HEAD_DIM = 64
A_V_DIM = 2 * A_HEAD_DIM
A_WIDTH = A_HEADS * A_V_DIM
QK_COLS = A_HEADS * 2 * A_HEAD_DIM
ROPE_THETA = 10000.0

B_GROUPS = 8
B_GROUP_W = 128
B_WIDTH = B_GROUPS * B_GROUP_W
CHUNK = 128
IN_COLS = 2 * QK_COLS + A_WIDTH + 2 * B_WIDTH

C_GROUPS = 4
C_GROUP_W = D_MODEL // C_GROUPS

N_EXPERTS = 16
CAPACITY_FACTOR = 2
F_EXPERT = D_MODEL // 2

LANES = 128
MXU_DIM = 256
VMEM_LIMIT_BYTES = 56 * 1024 * 1024

FFT_N2 = 32
ROUTE_TILE = 256
ROUTE_WIN = 256


def _params(sem):
    return pltpu.CompilerParams(dimension_semantics=sem, vmem_limit_bytes=VMEM_LIMIT_BYTES)


def _nt_dot(a, b):
    return lax.dot_general(a, b, (((1,), (1,)), ((), ())), preferred_element_type=F32)


def _norm_mod(x, g, shift, scale):
    ms = jnp.mean(x * x, axis=-1, keepdims=True)
    y = x * lax.rsqrt(ms + EPS) * g
    return y * (1.0 + scale) + shift


def _mod_kernel(c_ref, w_ref, b_ref, o_ref):
    c = c_ref[...]
    s = c * jax.nn.sigmoid(c)
    o_ref[...] = jnp.dot(s.astype(BF16), w_ref[...].astype(BF16), preferred_element_type=F32) + b_ref[...]


def _modulation(c8, w_mod, b_mod):
    depth, d, n6 = w_mod.shape
    tn = 1024
    return pl.pallas_call(
        _mod_kernel,
        grid=(depth, n6 // tn),
        in_specs=[
            pl.BlockSpec((8, d), lambda l, j: (0, 0)),
            pl.BlockSpec((None, d, tn), lambda l, j: (l, 0, j)),
            pl.BlockSpec((None, 1, tn), lambda l, j: (l, 0, j)),
        ],
        out_specs=pl.BlockSpec((None, 8, tn), lambda l, j: (l, 0, j)),
        out_shape=jax.ShapeDtypeStruct((depth, 8, n6), F32),
        compiler_params=_params(("arbitrary", "arbitrary")),
        name="modulation",
    )(c8, w_mod, b_mod.reshape(depth, 1, n6))


PROJ_TN = 512
_Q_TILES = QK_COLS // PROJ_TN
_V_TILES = A_WIDTH // PROJ_TN
_UV_TILES = 2 * B_WIDTH // PROJ_TN


def _rope_chunk(z, cos, sin_signed, first_half):
    partner = jnp.where(first_half, pltpu.roll(z, LANES - 16, 1), pltpu.roll(z, 16, 1))
    return z * cos + partner * sin_signed


def _proj_kernel(x_ref, g_ref, sh_ref, sc_ref, w_ref, wvt_ref, cos_ref, sin_ref,
                 q_ref, k_ref, vt_ref, uv_ref, h_scr):
    j = pl.program_id(1)

    @pl.when(j == 0)
    def _():
        h_scr[...] = _norm_mod(x_ref[...], g_ref[...], sh_ref[...], sc_ref[...]).astype(BF16)

    def project():
        return jnp.dot(h_scr[...], w_ref[...], preferred_element_type=F32)

    def roped(scale):
        z = project()
        cos = cos_ref[...]
        sin = sin_ref[...]
        lane = lax.broadcasted_iota(I32, cos.shape, 1)
        first_half = (lane % 32) < 16
        parts = []
        for c in range(PROJ_TN // LANES):
            zc = z[:, c * LANES:(c + 1) * LANES]
            parts.append(_rope_chunk(zc, cos, sin, first_half) * scale)
        return jnp.concatenate(parts, axis=1)

    @pl.when(j < _Q_TILES)
    def _():
        q_ref[...] = roped(A_HEAD_DIM ** -0.5 * math.log2(math.e)).astype(BF16)

    @pl.when((j >= _Q_TILES) & (j < 2 * _Q_TILES))
    def _():
        k_ref[...] = roped(1.0).astype(BF16)

    @pl.when((j >= 2 * _Q_TILES) & (j < 2 * _Q_TILES + _V_TILES))
    def _():
        vt_ref[...] = _nt_dot(wvt_ref[...], h_scr[...]).astype(BF16)

    @pl.when(j >= 2 * _Q_TILES + _V_TILES)
    def _():
        uv_ref[...] = jax.nn.gelu(project())


def _in_projection(x, g, shift, scale, w_in_bf16, w_v_t_bf16, cos_t, sin_t):
    n, d = x.shape
    tm = min(1024, n)
    nq = _Q_TILES
    nj = IN_COLS // PROJ_TN
    row = lambda i, j: (0, 0)
    v_tile = lambda j: jnp.clip(j - 2 * nq, 0, _V_TILES - 1)
    return pl.pallas_call(
        _proj_kernel,
        grid=(n // tm, nj),
        in_specs=[
            pl.BlockSpec((tm, d), lambda i, j: (i, 0)),
            pl.BlockSpec((1, d), row), pl.BlockSpec((1, d), row), pl.BlockSpec((1, d), row),
            pl.BlockSpec((d, PROJ_TN), lambda i, j: (0, j)),
            pl.BlockSpec((PROJ_TN, d), lambda i, j: (v_tile(j), 0)),
            pl.BlockSpec((tm, LANES), lambda i, j: (i, 0)),
            pl.BlockSpec((tm, LANES), lambda i, j: (i, 0)),
        ],
        out_specs=[
            pl.BlockSpec((tm, PROJ_TN), lambda i, j: (i, jnp.clip(j, 0, nq - 1))),
            pl.BlockSpec((tm, PROJ_TN), lambda i, j: (i, jnp.clip(j - nq, 0, nq - 1))),
            pl.BlockSpec((PROJ_TN, tm), lambda i, j: (v_tile(j), i)),
            pl.BlockSpec((tm, PROJ_TN), lambda i, j: (i, jnp.clip(j - 2 * nq - _V_TILES, 0, _UV_TILES - 1))),
        ],
        out_shape=[
            jax.ShapeDtypeStruct((n, QK_COLS), BF16),
            jax.ShapeDtypeStruct((n, QK_COLS), BF16),
            jax.ShapeDtypeStruct((A_WIDTH, n), BF16),
            jax.ShapeDtypeStruct((n, 2 * B_WIDTH), F32),
        ],
        scratch_shapes=[pltpu.VMEM((tm, d), BF16)],
        compiler_params=_params(("arbitrary", "arbitrary")),
        name="in_projection",
    )(x, g, shift, scale, w_in_bf16, w_v_t_bf16, cos_t, sin_t)


def _rope_tables(n, rotate):
    if not rotate:
        return jnp.ones((n, LANES), F32), jnp.zeros((n, LANES), F32)
    t = np.arange(n)
    row = (t // GRID_W).astype(np.float32)
    col = (t % GRID_W).astype(np.float32)
    dim = A_HEAD_DIM // 2
    inv = (np.float32(ROPE_THETA) ** (-np.arange(0, dim, 2, dtype=np.float32) / np.float32(dim))).astype(np.float32)
    ang_r = row[:, None] * inv[None, :]
    ang_c = col[:, None] * inv[None, :]
    ang64 = np.concatenate([ang_r, ang_r, ang_c, ang_c], axis=1)
    sign64 = np.concatenate([-np.ones(16), np.ones(16), -np.ones(16), np.ones(16)]).astype(np.float32)
    ang = np.tile(ang64, (1, LANES // 64))
    sign = np.tile(sign64, LANES // 64)
    return jnp.asarray(np.cos(ang), F32), jnp.asarray(np.sin(ang) * sign[None, :], F32)


def _attn_kernel(q_ref, k_ref, vt_ref, lamv_ref, gs_ref, o_ref, m_scr, acc_scr, sa_scr, sb_scr, *, tk, lam_init):
    tq = q_ref.shape[0]
    n_kv = k_ref.shape[0] // tk
    qt = q_ref[...].astype(F32).T
    row = lax.broadcasted_iota(I32, qt.shape, 0)
    zero = jnp.zeros_like(qt)
    qst = jnp.concatenate([jnp.where(row < A_HEAD_DIM, qt, zero),
                           jnp.where(row >= A_HEAD_DIM, qt, zero)], axis=1).astype(BF16)
    m_scr[...] = jnp.full(m_scr.shape, -jnp.inf, F32)
    acc_scr[...] = jnp.zeros(acc_scr.shape, F32)

    def scores(j):
        off = pl.multiple_of(j * tk, tk)
        return jnp.dot(k_ref[pl.ds(off, tk), :], qst, preferred_element_type=F32)

    ones_rows = jnp.ones((16, tk), BF16)

    def consume(s, j):
        off = pl.multiple_of(j * tk, tk)
        vtb = jnp.concatenate([vt_ref[:, pl.ds(off, tk)], ones_rows], axis=0)
        m_old = m_scr[...]
        m_new = jnp.maximum(m_old, jnp.max(s, axis=0, keepdims=True))
        alpha = jnp.exp2(m_old - m_new)
        p = jnp.exp2(s - m_new).astype(BF16)
        acc_scr[...] = alpha * acc_scr[...] + jnp.dot(vtb, p, preferred_element_type=F32)
        m_scr[...] = m_new

    sa_scr[...] = scores(0)

    def pair(i, carry):
        j = 2 * i
        sb_scr[...] = scores(j + 1)
        consume(sa_scr[...], j)
        sa_scr[...] = scores(j + 2)
        consume(sb_scr[...], j + 1)
        return carry

    if n_kv % 2 == 1:
        lax.fori_loop(0, (n_kv - 1) // 2, pair, 0)
        consume(sa_scr[...], n_kv - 1)
    else:
        lax.fori_loop(0, n_kv // 2 - 1, pair, 0)
        sb_scr[...] = scores(n_kv - 1)
        consume(sa_scr[...], n_kv - 2)
        consume(sb_scr[...], n_kv - 1)

    lv = lamv_ref[...]
    lam = (jnp.exp(jnp.sum(lv[0:1] * lv[1:2], axis=-1, keepdims=True))
           - jnp.exp(jnp.sum(lv[2:3] * lv[3:4], axis=-1, keepdims=True)) + lam_init)
    ot = acc_scr[:A_V_DIM, :] / acc_scr[A_V_DIM:A_V_DIM + 1, :]
    o = (ot[:, :tq] - lam * ot[:, tq:]).T
    a = o * lax.rsqrt(jnp.mean(o * o, axis=-1, keepdims=True) + EPS) * gs_ref[...]
    o_ref[...] = (a * (1.0 - lam_init)).astype(BF16)


def _pick_tile(n, candidates):
    for c in candidates:
        if n % c == 0:
            return c
    raise ValueError(f"no tile for {n}")


def _diff_attention(q, k_all, vt_all, lamv, g_subln, lam_init):
    n = q.shape[0]
    nk = k_all.shape[0]
    tq = 256
    tk = _pick_tile(nk, (1408, 768, 512, 256))
    return pl.pallas_call(
        functools.partial(_attn_kernel, tk=tk, lam_init=lam_init),
        grid=(A_HEADS, n // tq),
        in_specs=[
            pl.BlockSpec((tq, A_V_DIM), lambda h, i: (i, h)),
            pl.BlockSpec((nk, A_V_DIM), lambda h, i: (0, h)),
            pl.BlockSpec((A_V_DIM, nk), lambda h, i: (h, 0)),
            pl.BlockSpec((8, LANES), lambda h, i: (0, 0)),
            pl.BlockSpec((1, A_V_DIM), lambda h, i: (0, 0)),
        ],
        out_specs=pl.BlockSpec((tq, A_V_DIM), lambda h, i: (i, h)),
        out_shape=jax.ShapeDtypeStruct((n, A_WIDTH), BF16),
        scratch_shapes=[pltpu.VMEM((1, 2 * tq), F32),
                        pltpu.VMEM((A_V_DIM + 16, 2 * tq), F32),
                        pltpu.VMEM((tk, 2 * tq), F32), pltpu.VMEM((tk, 2 * tq), F32)],
        compiler_params=_params(("arbitrary", "arbitrary")),
        name="diff_attention",
    )(q, k_all, vt_all, lamv, g_subln)


def _finish_kernel(a_ref, uv_ref, x_ref, wout_ref, ws_ref, bs_ref, lng_ref, lnb_ref, gm_ref, o_ref, cat_scr):
    tm = a_ref.shape[0]
    cat_scr[:, :A_WIDTH] = a_ref[...]
    for c in range(tm // CHUNK):
        rows = slice(c * CHUNK, (c + 1) * CHUNK)
        for g in range(B_GROUPS):
            cols = slice(g * B_GROUP_W, (g + 1) * B_GROUP_W)
            u = uv_ref[rows, g * B_GROUP_W:(g + 1) * B_GROUP_W]
            v = uv_ref[rows, B_WIDTH + g * B_GROUP_W:B_WIDTH + (g + 1) * B_GROUP_W]
            mu = jnp.mean(v, axis=-1, keepdims=True)
            var = jnp.mean(jnp.square(v - mu), axis=-1, keepdims=True)
            vn = (v - mu) * lax.rsqrt(var + EPS) * lng_ref[:, cols] + lnb_ref[:, cols]
            mixed = jnp.dot(ws_ref[g], vn.astype(BF16), preferred_element_type=F32) + bs_ref[g]
            cat_scr[rows, A_WIDTH + g * B_GROUP_W:A_WIDTH + (g + 1) * B_GROUP_W] = (u * mixed).astype(BF16)
    y = jnp.dot(cat_scr[...], wout_ref[...], preferred_element_type=F32)
    o_ref[...] = x_ref[...] + gm_ref[...] * y


def _finish_even(a, uv, x, w_out_bf16, ws_bf16, bs_b, ln_g, ln_b, gm):
    n, d = x.shape
    tm = min(512, n)
    row = lambda i: (0, 0)
    return pl.pallas_call(
        _finish_kernel,
        grid=(n // tm,),
        in_specs=[
            pl.BlockSpec((tm, A_WIDTH), lambda i: (i, 0)),
            pl.BlockSpec((tm, 2 * B_WIDTH), lambda i: (i, 0)),
            pl.BlockSpec((tm, d), lambda i: (i, 0)),
            pl.BlockSpec((A_WIDTH + B_WIDTH, d), row),
            pl.BlockSpec((B_GROUPS, CHUNK, CHUNK), lambda i: (0, 0, 0)),
            pl.BlockSpec((B_GROUPS, CHUNK, B_GROUP_W), lambda i: (0, 0, 0)),
            pl.BlockSpec((1, B_WIDTH), row), pl.BlockSpec((1, B_WIDTH), row),
            pl.BlockSpec((1, d), row),
        ],
        out_specs=pl.BlockSpec((tm, d), lambda i: (i, 0)),
        out_shape=jax.ShapeDtypeStruct((n, d), F32),
        scratch_shapes=[pltpu.VMEM((tm, A_WIDTH + B_WIDTH), BF16)],
        compiler_params=_params(("arbitrary",)),
        name="finish_even",
    )(a, uv, x, w_out_bf16, ws_bf16, bs_b, ln_g, ln_b, gm)


def _router_kernel(x_ref, g_ref, sh_ref, sc_ref, wrt_ref, ht_ref, aff_ref):
    h = _norm_mod(x_ref[...], g_ref[...], sh_ref[...], sc_ref[...])
    logits = _nt_dot(wrt_ref[...], h.astype(BF16))
    m = jnp.max(logits, axis=0, keepdims=True)
    e = jnp.exp(logits - m)
    aff_ref[...] = e / jnp.sum(e, axis=0, keepdims=True)
    ht_ref[...] = h.T.astype(BF16)


def _router(x, g, shift, scale, w_router_t_bf16):
    n, d = x.shape
    tm = min(512, n)
    row = lambda i: (0, 0)
    return pl.pallas_call(
        _router_kernel,
        grid=(n // tm,),
        in_specs=[
            pl.BlockSpec((tm, d), lambda i: (i, 0)),
            pl.BlockSpec((1, d), row), pl.BlockSpec((1, d), row), pl.BlockSpec((1, d), row),
            pl.BlockSpec((N_EXPERTS, d), row),
        ],
        out_specs=[pl.BlockSpec((d, tm), lambda i: (0, i)), pl.BlockSpec((N_EXPERTS, tm), lambda i: (0, i))],
        out_shape=[jax.ShapeDtypeStruct((d, n), BF16), jax.ShapeDtypeStruct((N_EXPERTS, n), F32)],
        compiler_params=_params(("arbitrary",)),
        name="router",
    )(x, g, shift, scale, w_router_t_bf16)


def _select_kernel(aff_ref, sel_ref, pos_ref, g_ref, *, cap):
    aff = aff_ref[...]
    n = aff.shape[1]
    idx = lax.broadcasted_iota(I32, aff.shape, 1)
    capf = float(cap)

    def count(mask):
        return jnp.sum(jnp.where(mask, 1.0, 0.0), axis=1, keepdims=True)

    def as_float(bits):
        return pltpu.bitcast(bits, F32)

    def thr_body(i, thr):
        cand = thr | jnp.left_shift(jnp.int32(1), 30 - i)
        return jnp.where(count(aff >= as_float(cand)) >= capf, cand, thr)

    thr = lax.fori_loop(0, 31, thr_body, jnp.zeros((aff.shape[0], 1), I32))
    gt = aff >= as_float(thr + 1)
    eq = (aff >= as_float(thr)) & jnp.logical_not(gt)
    need = capf - count(gt)
    nbits = int(n).bit_length()

    def tie_body(i, lim):
        cand = lim | jnp.left_shift(jnp.int32(1), nbits - 1 - i)
        return jnp.where(count(eq & (idx < cand)) <= need, cand, lim)

    lim = lax.fori_loop(0, nbits, tie_body, jnp.zeros((aff.shape[0], 1), I32))
    sel = jnp.where(gt | (eq & (idx < lim)), 1.0, 0.0)
    sel_ref[...] = sel
    g_ref[...] = aff * sel

    ri = lax.broadcasted_iota(I32, (LANES, LANES), 0)
    ci = lax.broadcasted_iota(I32, (LANES, LANES), 1)
    upper = jnp.where(ri < ci, 1.0, 0.0).astype(BF16)
    carry = jnp.zeros((aff.shape[0], 1), F32)
    for c in range(n // LANES):
        m = sel[:, c * LANES:(c + 1) * LANES]
        within = jnp.dot(m.astype(BF16), upper, preferred_element_type=F32)
        pos_ref[:, c * LANES:(c + 1) * LANES] = (within + carry).astype(I32)
        carry = carry + jnp.sum(m, axis=1, keepdims=True)


def _select(aff_t, cap):
    e, n = aff_t.shape
    full = lambda: (0, 0)
    return pl.pallas_call(
        functools.partial(_select_kernel, cap=cap),
        grid=(),
        in_specs=[pl.BlockSpec((e, n), full)],
        out_specs=[pl.BlockSpec((e, n), full)] * 3,
        out_shape=[jax.ShapeDtypeStruct((e, n), F32), jax.ShapeDtypeStruct((e, n), I32),
                   jax.ShapeDtypeStruct((e, n), F32)],
        compiler_params=pltpu.CompilerParams(vmem_limit_bytes=VMEM_LIMIT_BYTES),
        name="expert_select",
    )(aff_t)


def _one_hot_window(pos_row, sel_row, lo, base, win=ROUTE_WIN):
    rel = jnp.where((sel_row > 0.0) & (pos_row >= lo), pos_row - base, -1)
    r = lax.broadcasted_iota(I32, (win, pos_row.shape[1]), 0)
    return jnp.where(r == rel, 1.0, 0.0).astype(BF16)


def _window_plan(p0, p1, win=ROUTE_WIN):
    start = (p0 // LANES) * LANES
    n_chunks = jnp.where(p1 > p0, (p1 - start + win - 1) // win, 0)
    return start, n_chunks


def _dispatch_kernel(ps_ref, pe_ref, ht_ref, sel_ref, pos_ref, g_ref, xs_ref, gslot_ref, xst_scr, gs_scr, *, tiles_per_step):
    e = pl.program_id(0)
    c = pl.program_id(1)
    capp = xst_scr.shape[1]

    @pl.when(c == 0)
    def _():
        xst_scr[...] = jnp.zeros(xst_scr.shape, F32)
        gs_scr[...] = jnp.zeros(gs_scr.shape, F32)

    def add_window(tt, k):
        tile = c * tiles_per_step + tt
        start, _ = _window_plan(ps_ref[e, tile], pe_ref[e, tile])
        cols = slice(tt * ROUTE_TILE, (tt + 1) * ROUTE_TILE)
        g_row = g_ref[0, :, cols]
        g_hi = g_row.astype(BF16)
        r1 = g_row - g_hi.astype(F32)
        g_mid = r1.astype(BF16)
        g_lo = (r1 - g_mid.astype(F32)).astype(BF16)
        g8 = jnp.concatenate([g_hi, g_mid, g_lo, jnp.zeros((5, ROUTE_TILE), BF16)], axis=0)
        lo = start + k * ROUTE_WIN
        base = pl.multiple_of(jnp.minimum(lo, capp - ROUTE_WIN), LANES)
        onehot = _one_hot_window(pos_ref[0, :, cols], sel_ref[0, :, cols], lo, base)
        xst_scr[:, pl.ds(base, ROUTE_WIN)] += _nt_dot(ht_ref[:, cols], onehot)
        gs_scr[:, pl.ds(base, ROUTE_WIN)] += _nt_dot(g8, onehot)

    for tt in range(tiles_per_step):
        add_window(tt, 0)
    for tt in range(tiles_per_step):
        tile = c * tiles_per_step + tt
        n_chunks = _window_plan(ps_ref[e, tile], pe_ref[e, tile])[1]
        lax.fori_loop(1, n_chunks, lambda k, carry, tt=tt: (add_window(tt, k), carry)[1], 0)

    @pl.when(c == pl.num_programs(1) - 1)
    def _():
        xs_ref[0] = xst_scr[...].T.astype(BF16)
        gs = gs_scr[...]
        gslot_ref[0] = gs[0:1] + gs[1:2] + gs[2:3]


def _dispatch(tile_start, tile_end, h_t, sel3, pos3, g3, capp):
    d, n = h_t.shape
    tc = min(1024, n)
    tiles_per_step = tc // ROUTE_TILE
    row3 = lambda e, c, ps, pe: (e, 0, c)
    return pl.pallas_call(
        functools.partial(_dispatch_kernel, tiles_per_step=tiles_per_step),
        grid_spec=pltpu.PrefetchScalarGridSpec(
            num_scalar_prefetch=2,
            grid=(N_EXPERTS, n // tc),
            in_specs=[
                pl.BlockSpec((d, tc), lambda e, c, ps, pe: (0, c)),
                pl.BlockSpec((1, 1, tc), row3), pl.BlockSpec((1, 1, tc), row3), pl.BlockSpec((1, 1, tc), row3),
            ],
            out_specs=[
                pl.BlockSpec((1, capp, d), lambda e, c, ps, pe: (e, 0, 0)),
                pl.BlockSpec((1, 1, capp), lambda e, c, ps, pe: (e, 0, 0)),
            ],
            scratch_shapes=[pltpu.VMEM((d, capp), F32), pltpu.VMEM((8, capp), F32)],
        ),
        out_shape=[jax.ShapeDtypeStruct((N_EXPERTS, capp, d), BF16),
                   jax.ShapeDtypeStruct((N_EXPERTS, 1, capp), F32)],
        compiler_params=_params(("arbitrary", "arbitrary")),
        name="moe_dispatch",
    )(tile_start, tile_end, h_t, sel3, pos3, g3)


def _ffn_up_kernel(xs_ref, wg_ref, wu_ref, h_ref):
    xs = xs_ref[0]
    a = jnp.dot(xs, wg_ref[0].astype(BF16), preferred_element_type=F32)
    u = jnp.dot(xs, wu_ref[0].astype(BF16), preferred_element_type=F32)
    h_ref[0] = (a * jax.nn.sigmoid(a) * u).astype(BF16)


def _ffn_up(xs, w_gate, w_up, layer):
    e, capp, d = xs.shape
    f = w_gate.shape[3]
    tf = 512
    return pl.pallas_call(
        _ffn_up_kernel,
        grid=(e, f // tf),
        in_specs=[
            pl.BlockSpec((1, capp, d), lambda i, j: (i, 0, 0)),
            pl.BlockSpec((None, 1, d, tf), lambda i, j: (layer, i, 0, j)),
            pl.BlockSpec((None, 1, d, tf), lambda i, j: (layer, i, 0, j)),
        ],
        out_specs=pl.BlockSpec((1, capp, tf), lambda i, j: (i, 0, j)),
        out_shape=jax.ShapeDtypeStruct((e, capp, f), BF16),
        compiler_params=_params(("arbitrary", "arbitrary")),
        name="moe_ffn_up",
    )(xs, w_gate, w_up)


def _ffn_down_kernel(h_ref, wd_ref, gslot_ref, yh_ref, yl_ref):
    y = jnp.dot(h_ref[0], wd_ref[0].astype(BF16), preferred_element_type=F32)
    yw = y.T * gslot_ref[0]
    hi = yw.astype(BF16)
    yh_ref[0] = hi
    yl_ref[0] = (yw - hi.astype(F32)).astype(BF16)


def _ffn_down(h, w_down, gslot, layer):
    e, capp, f = h.shape
    d = w_down.shape[3]
    td = 512
    shp = jax.ShapeDtypeStruct((e, d, capp), BF16)
    return pl.pallas_call(
        _ffn_down_kernel,
        grid=(e, d // td),
        in_specs=[
            pl.BlockSpec((1, capp, f), lambda i, j: (i, 0, 0)),
            pl.BlockSpec((None, 1, f, td), lambda i, j: (layer, i, 0, j)),
            pl.BlockSpec((1, 1, capp), lambda i, j: (i, 0, 0)),
        ],
        out_specs=[pl.BlockSpec((1, td, capp), lambda i, j: (i, j, 0))] * 2,
        out_shape=[shp, shp],
        compiler_params=_params(("arbitrary", "arbitrary")),
        name="moe_ffn_down",
    )(h, w_down, gslot)


COMBINE_TD = 512
COMBINE_WIN = MXU_DIM // 2


def _combine_kernel(ps_ref, pe_ref, yh_ref, yl_ref, sel_ref, pos_ref, o_ref, *, tiles_per_iter):
    e = pl.program_id(1)
    capp = yh_ref.shape[2]
    n_tiles = sel_ref.shape[1]

    @pl.when(e == 0)
    def _():
        o_ref[...] = jnp.zeros(o_ref.shape, F32)

    def add_window(t, k):
        start, _ = _window_plan(ps_ref[e, t], pe_ref[e, t], COMBINE_WIN)
        pos_row = pos_ref[0, pl.ds(t, 1), :]
        sel_row = sel_ref[0, pl.ds(t, 1), :]
        lo = start + k * COMBINE_WIN
        base = pl.multiple_of(jnp.minimum(lo, capp - COMBINE_WIN), LANES)
        onehot = _one_hot_window(pos_row, sel_row, lo, base, COMBINE_WIN)
        y2 = jnp.concatenate([yh_ref[0, :, pl.ds(base, COMBINE_WIN)], yl_ref[0, :, pl.ds(base, COMBINE_WIN)]], axis=1)
        cols = pl.ds(pl.multiple_of(t * ROUTE_TILE, ROUTE_TILE), ROUTE_TILE)
        o_ref[:, cols] += jnp.dot(y2, jnp.concatenate([onehot, onehot], axis=0), preferred_element_type=F32)

    def body(i, carry):
        tiles = [i * tiles_per_iter + u for u in range(tiles_per_iter)]
        for t in tiles:
            add_window(t, 0)
        for t in tiles:
            n_chunks = _window_plan(ps_ref[e, t], pe_ref[e, t], COMBINE_WIN)[1]
            lax.fori_loop(1, n_chunks, lambda k, c, t=t: (add_window(t, k), c)[1], 0)
        return carry

    lax.fori_loop(0, n_tiles // tiles_per_iter, body, 0)


def _combine(tile_start, tile_end, yh, yl, sel_t, pos_t, n):
    e, d, capp = yh.shape
    n_tiles = n // ROUTE_TILE
    td = COMBINE_TD
    return pl.pallas_call(
        functools.partial(_combine_kernel, tiles_per_iter=4 if n_tiles % 4 == 0 else 1),
        grid_spec=pltpu.PrefetchScalarGridSpec(
            num_scalar_prefetch=2,
            grid=(d // td, e),
            in_specs=[
                pl.BlockSpec((1, td, capp), lambda j, i, ps, pe: (i, j, 0)),
                pl.BlockSpec((1, td, capp), lambda j, i, ps, pe: (i, j, 0)),
                pl.BlockSpec((1, n_tiles, ROUTE_TILE), lambda j, i, ps, pe: (i, 0, 0)),
                pl.BlockSpec((1, n_tiles, ROUTE_TILE), lambda j, i, ps, pe: (i, 0, 0)),
            ],
            out_specs=pl.BlockSpec((td, n), lambda j, i, ps, pe: (j, 0)),
        ),
        out_shape=jax.ShapeDtypeStruct((d, n), F32),
        compiler_params=_params(("arbitrary", "arbitrary")),
        name="moe_combine",
    )(tile_start, tile_end, yh, yl, sel_t, pos_t)


def _expert_choice_ffn(x, g, shift, scale, w_router, w_gate, w_up, w_down, layer):
    n, d = x.shape
    cap = CAPACITY_FACTOR * n // N_EXPERTS
    capp = max(cap, ROUTE_WIN)
    h_t, aff_t = _router(x, g, shift, scale, w_router.T.astype(BF16))
    sel, pos, gsel = _select(aff_t, cap)
    tile_start = pos[:, ::ROUTE_TILE]
    tile_end = jnp.concatenate([tile_start[:, 1:], jnp.full((N_EXPERTS, 1), cap, I32)], axis=1)
    as3 = lambda a: a.reshape(N_EXPERTS, 1, n)
    xs, gslot = _dispatch(tile_start, tile_end, h_t, as3(sel), as3(pos), as3(gsel), capp)
    h = _ffn_up(xs, w_gate, w_up, layer)
    yh, yl = _ffn_down(h, w_down, gslot, layer)
    as_tiles = lambda a: a.reshape(N_EXPERTS, n // ROUTE_TILE, ROUTE_TILE)
    return _combine(tile_start, tile_end, yh, yl, as_tiles(sel), as_tiles(pos), n)


def _resid_kernel(x_ref, yt_ref, gate_ref, o_ref):
    o_ref[...] = x_ref[...] + gate_ref[...] * yt_ref[...].T


def _residual(x, y_t, gate):
    n, d = x.shape
    tm = min(512, n)
    return pl.pallas_call(
        _resid_kernel,
        grid=(n // tm,),
        in_specs=[pl.BlockSpec((tm, d), lambda i: (i, 0)), pl.BlockSpec((d, tm), lambda i: (0, i)),
                  pl.BlockSpec((1, d), lambda i: (0, 0))],
        out_specs=pl.BlockSpec((tm, d), lambda i: (i, 0)),
        out_shape=jax.ShapeDtypeStruct((n, d), F32),
        compiler_params=_params(("arbitrary",)),
        name="residual",
    )(x, y_t, gate)


def _final_kernel(x_ref, yt_ref, gate_ref, g_ref, o_ref):
    x = x_ref[...] + gate_ref[...] * yt_ref[...].T
    o_ref[...] = x * lax.rsqrt(jnp.mean(x * x, axis=-1, keepdims=True) + EPS) * g_ref[...]


def _final_norm(x, y_t, gate, g_final):
    n, d = x.shape
    tm = min(512, n)
    row = lambda i: (0, 0)
    return pl.pallas_call(
        _final_kernel,
        grid=(n // tm,),
        in_specs=[pl.BlockSpec((tm, d), lambda i: (i, 0)), pl.BlockSpec((d, tm), lambda i: (0, i)),
                  pl.BlockSpec((1, d), row), pl.BlockSpec((1, d), row)],
        out_specs=pl.BlockSpec((tm, d), lambda i: (i, 0)),
        out_shape=jax.ShapeDtypeStruct((n, d), F32),
        compiler_params=_params(("arbitrary",)),
        name="final_norm",
    )(x, y_t, gate, g_final)


FFT_K1_BLK = 16


def _fourier_in_kernel(x_ref, yt_ref, gate_ref, g_ref, sh_ref, sc_ref, cs_ref, perm_ref, x1_ref, a_ref, b_ref):
    x1 = x_ref[...] + gate_ref[...] * yt_ref[...].T
    x1_ref[...] = x1
    h = _norm_mod(x1, g_ref[...], sh_ref[...], sc_ref[...]).astype(BF16)
    hp = jnp.dot(perm_ref[...], h, preferred_element_type=F32).astype(BF16)
    n2, t1_blk = a_ref.shape[0], a_ref.shape[1]
    for g in range(C_GROUPS):
        cols = slice(g * C_GROUP_W, (g + 1) * C_GROUP_W)
        ab = jnp.dot(hp[:, cols], cs_ref[...], preferred_element_type=F32)
        a_ref[:, :, cols] = ab[:, :C_GROUP_W].astype(BF16).reshape(n2, t1_blk, C_GROUP_W)
        b_ref[:, :, cols] = ab[:, C_GROUP_W:].astype(BF16).reshape(n2, t1_blk, C_GROUP_W)


def _fourier_in(x, moe_t, gate, g, shift, scale, cs):
    n, d = x.shape
    tm = 512
    n1 = n // FFT_N2
    t1_blk = tm // FFT_N2
    row = lambda i: (0, 0)
    tile = pl.BlockSpec((tm, d), lambda i: (i, 0))
    tile_t = pl.BlockSpec((d, tm), lambda i: (0, i))
    ab_blk = pl.BlockSpec((FFT_N2, t1_blk, d), lambda i: (0, i, 0))
    ab_shape = jax.ShapeDtypeStruct((FFT_N2, n1, d), BF16)
    perm = np.zeros((tm, tm), np.float32)
    src = np.arange(tm)
    perm[(src % FFT_N2) * t1_blk + src // FFT_N2, src] = 1.0
    return pl.pallas_call(
        _fourier_in_kernel,
        grid=(n // tm,),
        in_specs=[tile, tile_t, pl.BlockSpec((1, d), row), pl.BlockSpec((1, d), row), pl.BlockSpec((1, d), row),
                  pl.BlockSpec((1, d), row), pl.BlockSpec((C_GROUP_W, 2 * C_GROUP_W), row),
                  pl.BlockSpec((tm, tm), row)],
        out_specs=[tile, ab_blk, ab_blk],
        out_shape=[jax.ShapeDtypeStruct((n, d), F32), ab_shape, ab_shape],
        compiler_params=_params(("arbitrary",)),
        name="fourier_channel_dft",
    )(x, moe_t, gate, g, shift, scale, cs, jnp.asarray(perm, BF16))


def _fourier_stage1_kernel(a_ref, b_ref, ma_ref, mb_ref, ct_ref, st_ref, zr_ref, zi_ref):
    n1 = a_ref.shape[1]
    z = (jnp.dot(ma_ref[...], a_ref[0], preferred_element_type=F32)
         + jnp.dot(mb_ref[...], b_ref[0], preferred_element_type=F32))
    ct = ct_ref[0]
    st = st_ref[0]
    for c in range(a_ref.shape[2] // LANES):
        cols = slice(c * LANES, (c + 1) * LANES)
        zr = z[:n1, cols]
        zi = z[n1:, cols]
        zr_ref[0, :, cols] = (zr * ct + zi * st).astype(BF16)
        zi_ref[0, :, cols] = (zi * ct - zr * st).astype(BF16)


def _fourier_stage1(a3, b3, ma, mb, ct, st):
    n2, n1, d = a3.shape
    blk = pl.BlockSpec((1, n1, d), lambda j: (j, 0, 0))
    mat = pl.BlockSpec((2 * n1, n1), lambda j: (0, 0))
    tw = pl.BlockSpec((1, n1, LANES), lambda j: (j, 0, 0))
    shp = jax.ShapeDtypeStruct((n2, n1, d), BF16)
    return pl.pallas_call(
        _fourier_stage1_kernel,
        grid=(n2,),
        in_specs=[blk, blk, mat, mat, tw, tw],
        out_specs=[blk, blk],
        out_shape=[shp, shp],
        compiler_params=_params(("arbitrary",)),
        name="fourier_stage1",
    )(a3, b3, ma, mb, ct, st)


def _fourier_out_kernel(zr_ref, zi_ref, bc_ref, bs_ref, wo_ref, x_ref, gm_ref, o_ref):
    rows = zr_ref.shape[0] * zr_ref.shape[1]
    d = zr_ref.shape[2]
    zr = zr_ref[...].reshape(rows, d)
    zi = zi_ref[...].reshape(rows, d)
    f = (jnp.dot(bc_ref[...], zr, preferred_element_type=F32)
         + jnp.dot(bs_ref[...], zi, preferred_element_type=F32))
    y = jnp.dot(f.astype(BF16), wo_ref[...], preferred_element_type=F32)
    o_ref[...] = x_ref[...] + gm_ref[...] * y.reshape(o_ref.shape)


def _fourier_out(zr3, zi3, bd_c, bd_s, w_o_bf16, x, gm):
    n2, n1, d = zr3.shape
    n = n1 * n2
    rows = n2 * FFT_K1_BLK
    x3 = x.reshape(n2, n1, d)
    blk = pl.BlockSpec((n2, FFT_K1_BLK, d), lambda i: (0, i, 0))
    const = lambda i: (0, 0)
    out = pl.pallas_call(
        _fourier_out_kernel,
        grid=(n1 // FFT_K1_BLK,),
        in_specs=[blk, blk, pl.BlockSpec((rows, rows), const), pl.BlockSpec((rows, rows), const),
                  pl.BlockSpec((d, d), const), blk, pl.BlockSpec((1, d), const)],
        out_specs=blk,
        out_shape=jax.ShapeDtypeStruct((n2, n1, d), F32),
        compiler_params=_params(("arbitrary",)),
        name="fourier_stage2_out",
    )(zr3, zi3, bd_c, bd_s, w_o_bf16, x3, gm)
    return out.reshape(n, d)


def _fourier_constants(n):
    n1, n2 = n // FFT_N2, FFT_N2
    two_pi = 2.0 * np.pi

    def angles(a, b, period):
        return two_pi * ((np.outer(a, b) % period).astype(np.float64) / period)

    kc = np.arange(C_GROUP_W)
    ang = angles(kc, kc, C_GROUP_W)
    cs = np.concatenate([np.cos(ang), np.sin(ang)], axis=1) / np.sqrt(C_GROUP_W)
    k1 = np.arange(n1)
    ang1 = angles(k1, k1, n1)
    c1, s1 = np.cos(ang1) / np.sqrt(n), np.sin(ang1) / np.sqrt(n)
    ma = np.concatenate([c1, -s1], axis=0)
    mb = np.concatenate([-s1, -c1], axis=0)
    t2 = np.arange(n2)
    angt = angles(t2, k1, n)
    ct = np.repeat(np.cos(angt)[:, :, None], LANES, axis=2)
    st = np.repeat(np.sin(angt)[:, :, None], LANES, axis=2)
    ang2 = angles(t2, t2, n2)
    k1_blk = FFT_K1_BLK
    bd_c = np.zeros((n2 * k1_blk, n2 * k1_blk))
    bd_s = np.zeros((n2 * k1_blk, n2 * k1_blk))
    for kl in range(k1_blk):
        bd_c[kl::k1_blk, kl::k1_blk] = np.cos(ang2)
        bd_s[kl::k1_blk, kl::k1_blk] = np.sin(ang2)
    bf = lambda a: jnp.asarray(a, F32).astype(BF16)
    return bf(cs), bf(ma), bf(mb), jnp.asarray(ct, F32), jnp.asarray(st, F32), bf(bd_c), bf(bd_s)


def _fourier_mix_layer(x, moe_t, gate, g, shift, scale, gm, w_o):
    cs, ma, mb, ct, st, bd_c, bd_s = _fourier_constants(x.shape[0])
    x1, a3, b3 = _fourier_in(x, moe_t, gate, g, shift, scale, cs)
    zr3, zi3 = _fourier_stage1(a3, b3, ma, mb, ct, st)
    return _fourier_out(zr3, zi3, bd_c, bd_s, w_o.astype(BF16), x1, gm)


def _even_layer_mix(x, ctx, mods, g_mix, w_in, w_out, lamv, g_subln, ln_g, ln_b, w_s, b_s, lam_init, need_ctx_out):
    sm, cm, gm = mods["lat"][0:3]
    smc, cmc, gmc = mods["ctx"][0:3]
    n = x.shape[0]
    w_in_b = w_in.astype(BF16)
    w_out_b = w_out.astype(BF16)
    ws_b = w_s.astype(BF16)
    bs_b = jnp.broadcast_to(b_s[:, :, None], (B_GROUPS, CHUNK, B_GROUP_W))
    cos_l, sin_l = _rope_tables(n, True)
    cos_c, sin_c = _rope_tables(ctx.shape[0], False)
    w_v_t = w_in[:, 2 * QK_COLS:2 * QK_COLS + A_WIDTH].T.astype(BF16)
    q_l, k_l, vt_l, uv_l = _in_projection(x, g_mix, sm, cm, w_in_b, w_v_t, cos_l, sin_l)
    q_c, k_c, vt_c, uv_c = _in_projection(ctx, g_mix, smc, cmc, w_in_b, w_v_t, cos_c, sin_c)
    k_all = jnp.concatenate([k_c, k_l], axis=0)
    vt_all = jnp.concatenate([vt_c, vt_l], axis=1)
    a_l = _diff_attention(q_l, k_all, vt_all, lamv, g_subln, lam_init)
    x = _finish_even(a_l, uv_l, x, w_out_b, ws_b, bs_b, ln_g, ln_b, gm)
    if need_ctx_out:
        a_c = _diff_attention(q_c, k_c, vt_c, lamv, g_subln, lam_init)
        ctx = _finish_even(a_c, uv_c, ctx, w_out_b, ws_b, bs_b, ln_g, ln_b, gmc)
    return x, ctx


def kernel(x, c, ctx, c_ctx, w_mod, b_mod, g_norm_mix, g_norm_ffn, w_in, w_out, lam_q1, lam_k1, lam_q2, lam_k2,
           g_subln, sgu_ln_g, sgu_ln_b, w_spatial, b_spatial, w_fourier_out, w_router, w_gate, w_up, w_down, g_final):
    assert x.shape[0] == 1 and DEPTH == 2
    d = D_MODEL
    x2 = x[0]
    ctx2 = ctx[0]
    c8 = jnp.zeros((8, d), F32).at[0].set(c[0]).at[1].set(c_ctx)
    mod_all = _modulation(c8, w_mod, b_mod)
    row = lambda v: v.reshape(1, -1)

    def mods_of(i):
        lat = [mod_all[i, 0:1, k * d:(k + 1) * d] for k in range(6)]
        cx = [mod_all[i, 1:2, k * d:(k + 1) * d] for k in range(6)]
        return {"lat": lat, "ctx": cx}

    m0 = mods_of(0)
    lam_init0 = 0.8 - 0.6 * math.exp(-0.3 * 0)
    lamv = jnp.zeros((8, LANES), F32)
    for r, v in enumerate((lam_q1[0], lam_k1[0], lam_q2[0], lam_k2[0])):
        lamv = lamv.at[r, :A_HEAD_DIM].set(v)
    x2, ctx2 = _even_layer_mix(x2, ctx2, m0, row(g_norm_mix[0]), w_in[0], w_out[0], lamv, row(g_subln[0]),
                               row(sgu_ln_g[0]), row(sgu_ln_b[0]), w_spatial[0], b_spatial[0], lam_init0, True)
    moe_x = _expert_choice_ffn(x2, row(g_norm_ffn[0]), m0["lat"][3], m0["lat"][4],
                               w_router[0], w_gate, w_up, w_down, 0)
    moe_c = _expert_choice_ffn(ctx2, row(g_norm_ffn[0]), m0["ctx"][3], m0["ctx"][4],
                               w_router[0], w_gate, w_up, w_down, 0)
    ctx2 = _residual(ctx2, moe_c, m0["ctx"][5])

    m1 = mods_of(1)
    x2 = _fourier_mix_layer(x2, moe_x, m0["lat"][5], row(g_norm_mix[1]), m1["lat"][0], m1["lat"][1],
                            m1["lat"][2], w_fourier_out[0])
    moe_x = _expert_choice_ffn(x2, row(g_norm_ffn[1]), m1["lat"][3], m1["lat"][4],
                               w_router[1], w_gate, w_up, w_down, 1)
    out = _final_norm(x2, moe_x, m1["lat"][5], row(g_final))
    del ctx2
    return out[None]
```

```python
import functools
import math

import numpy as np
import jax
import jax.numpy as jnp
from jax import lax
from jax.experimental import pallas as pl
from jax.experimental.pallas import tpu as pltpu

F32 = jnp.float32
BF16 = jnp.bfloat16
I32 = jnp.int32

D_MODEL = 2048
DEPTH = 2
GRID_W = 64
EPS = 1e-6

A_HEADS = 8
A_HEAD_DIM = 64
A_V_DIM = 2 * A_HEAD_DIM
A_WIDTH = A_HEADS * A_V_DIM
QK_COLS = A_HEADS * 2 * A_HEAD_DIM
ROPE_THETA = 10000.0

B_GROUPS = 8
B_GROUP_W = 128
B_WIDTH = B_GROUPS * B_GROUP_W
CHUNK = 128
IN_COLS = 2 * QK_COLS + A_WIDTH + 2 * B_WIDTH

C_GROUPS = 4
C_GROUP_W = D_MODEL // C_GROUPS

N_EXPERTS = 16
CAPACITY_FACTOR = 2
F_EXPERT = D_MODEL // 2

LANES = 128
MXU_DIM = 256
VMEM_LIMIT_BYTES = 56 * 1024 * 1024

FFT_N2 = 32
ROUTE_TILE = 256
ROUTE_WIN = 256


def _params(sem):
    return pltpu.CompilerParams(dimension_semantics=sem, vmem_limit_bytes=VMEM_LIMIT_BYTES)


def _nt_dot(a, b):
    return lax.dot_general(a, b, (((1,), (1,)), ((), ())), preferred_element_type=F32)


def _norm_mod(x, g, shift, scale):
    ms = jnp.mean(x * x, axis=-1, keepdims=True)
    y = x * lax.rsqrt(ms + EPS) * g
    return y * (1.0 + scale) + shift


def _mod_kernel(c_ref, w_ref, b_ref, o_ref):
    c = c_ref[...]
    s = c * jax.nn.sigmoid(c)
    o_ref[...] = jnp.dot(s.astype(BF16), w_ref[...].astype(BF16), preferred_element_type=F32) + b_ref[...]


def _modulation(c8, w_mod, b_mod):
    depth, d, n6 = w_mod.shape
    tn = 1024
    return pl.pallas_call(
        _mod_kernel,
        grid=(depth, n6 // tn),
        in_specs=[
            pl.BlockSpec((8, d), lambda l, j: (0, 0)),
            pl.BlockSpec((None, d, tn), lambda l, j: (l, 0, j)),
            pl.BlockSpec((None, 1, tn), lambda l, j: (l, 0, j)),
        ],
        out_specs=pl.BlockSpec((None, 8, tn), lambda l, j: (l, 0, j)),
        out_shape=jax.ShapeDtypeStruct((depth, 8, n6), F32),
        compiler_params=_params(("arbitrary", "arbitrary")),
        name="modulation",
    )(c8, w_mod, b_mod.reshape(depth, 1, n6))


PROJ_TN = 512
_Q_TILES = QK_COLS // PROJ_TN
_V_TILES = A_WIDTH // PROJ_TN
_UV_TILES = 2 * B_WIDTH // PROJ_TN


def _rope_chunk(z, cos, sin_signed, first_half):
    partner = jnp.where(first_half, pltpu.roll(z, LANES - 16, 1), pltpu.roll(z, 16, 1))
    return z * cos + partner * sin_signed


def _proj_kernel(x_ref, g_ref, sh_ref, sc_ref, w_ref, wvt_ref, cos_ref, sin_ref,
                 q_ref, k_ref, vt_ref, uv_ref, h_scr):
    j = pl.program_id(1)

    @pl.when(j == 0)
    def _():
        h_scr[...] = _norm_mod(x_ref[...], g_ref[...], sh_ref[...], sc_ref[...]).astype(BF16)

    def project():
        return jnp.dot(h_scr[...], w_ref[...], preferred_element_type=F32)

    def roped(scale):
        z = project()
        cos = cos_ref[...]
        sin = sin_ref[...]
        lane = lax.broadcasted_iota(I32, cos.shape, 1)
        first_half = (lane % 32) < 16
        parts = []
        for c in range(PROJ_TN // LANES):
            zc = z[:, c * LANES:(c + 1) * LANES]
            parts.append(_rope_chunk(zc, cos, sin, first_half) * scale)
        return jnp.concatenate(parts, axis=1)

    @pl.when(j < _Q_TILES)
    def _():
        q_ref[...] = roped(A_HEAD_DIM ** -0.5 * math.log2(math.e)).astype(BF16)

    @pl.when((j >= _Q_TILES) & (j < 2 * _Q_TILES))
    def _():
        k_ref[...] = roped(1.0).astype(BF16)

    @pl.when((j >= 2 * _Q_TILES) & (j < 2 * _Q_TILES + _V_TILES))
    def _():
        vt_ref[...] = _nt_dot(wvt_ref[...], h_scr[...]).astype(BF16)

    @pl.when(j >= 2 * _Q_TILES + _V_TILES)
    def _():
        uv_ref[...] = jax.nn.gelu(project())


def _in_projection(x, g, shift, scale, w_in_bf16, w_v_t_bf16, cos_t, sin_t):
    n, d = x.shape
    tm = min(1024, n)
    nq = _Q_TILES
    nj = IN_COLS // PROJ_TN
    row = lambda i, j: (0, 0)
    v_tile = lambda j: jnp.clip(j - 2 * nq, 0, _V_TILES - 1)
    return pl.pallas_call(
        _proj_kernel,
        grid=(n // tm, nj),
        in_specs=[
            pl.BlockSpec((tm, d), lambda i, j: (i, 0)),
            pl.BlockSpec((1, d), row), pl.BlockSpec((1, d), row), pl.BlockSpec((1, d), row),
            pl.BlockSpec((d, PROJ_TN), lambda i, j: (0, j)),
            pl.BlockSpec((PROJ_TN, d), lambda i, j: (v_tile(j), 0)),
            pl.BlockSpec((tm, LANES), lambda i, j: (i, 0)),
            pl.BlockSpec((tm, LANES), lambda i, j: (i, 0)),
        ],
        out_specs=[
            pl.BlockSpec((tm, PROJ_TN), lambda i, j: (i, jnp.clip(j, 0, nq - 1))),
            pl.BlockSpec((tm, PROJ_TN), lambda i, j: (i, jnp.clip(j - nq, 0, nq - 1))),
            pl.BlockSpec((PROJ_TN, tm), lambda i, j: (v_tile(j), i)),
            pl.BlockSpec((tm, PROJ_TN), lambda i, j: (i, jnp.clip(j - 2 * nq - _V_TILES, 0, _UV_TILES - 1))),
        ],
        out_shape=[
            jax.ShapeDtypeStruct((n, QK_COLS), BF16),
            jax.ShapeDtypeStruct((n, QK_COLS), BF16),
            jax.ShapeDtypeStruct((A_WIDTH, n), BF16),
            jax.ShapeDtypeStruct((n, 2 * B_WIDTH), F32),
        ],
        scratch_shapes=[pltpu.VMEM((tm, d), BF16)],
        compiler_params=_params(("arbitrary", "arbitrary")),
        name="in_projection",
    )(x, g, shift, scale, w_in_bf16, w_v_t_bf16, cos_t, sin_t)


def _rope_tables(n, rotate):
    if not rotate:
        return jnp.ones((n, LANES), F32), jnp.zeros((n, LANES), F32)
    t = np.arange(n)
    row = (t // GRID_W).astype(np.float32)
    col = (t % GRID_W).astype(np.float32)
    dim = A_HEAD_DIM // 2
    inv = (np.float32(ROPE_THETA) ** (-np.arange(0, dim, 2, dtype=np.float32) / np.float32(dim))).astype(np.float32)
    ang_r = row[:, None] * inv[None, :]
    ang_c = col[:, None] * inv[None, :]
    ang64 = np.concatenate([ang_r, ang_r, ang_c, ang_c], axis=1)
    sign64 = np.concatenate([-np.ones(16), np.ones(16), -np.ones(16), np.ones(16)]).astype(np.float32)
    ang = np.tile(ang64, (1, LANES // 64))
    sign = np.tile(sign64, LANES // 64)
    return jnp.asarray(np.cos(ang), F32), jnp.asarray(np.sin(ang) * sign[None, :], F32)


def _attn_kernel(q_ref, k_ref, vt_ref, lamv_ref, gs_ref, o_ref, m_scr, acc_scr, sa_scr, sb_scr, *, tk, lam_init):
    tq = q_ref.shape[0]
    n_kv = k_ref.shape[0] // tk
    qt = q_ref[...].astype(F32).T
    row = lax.broadcasted_iota(I32, qt.shape, 0)
    zero = jnp.zeros_like(qt)
    qst = jnp.concatenate([jnp.where(row < A_HEAD_DIM, qt, zero),
                           jnp.where(row >= A_HEAD_DIM, qt, zero)], axis=1).astype(BF16)
    m_scr[...] = jnp.full(m_scr.shape, -jnp.inf, F32)
    acc_scr[...] = jnp.zeros(acc_scr.shape, F32)

    def scores(j):
        off = pl.multiple_of(j * tk, tk)
        return jnp.dot(k_ref[pl.ds(off, tk), :], qst, preferred_element_type=F32)

    ones_rows = jnp.ones((16, tk), BF16)

    def consume(s, j):
        off = pl.multiple_of(j * tk, tk)
        vtb = jnp.concatenate([vt_ref[:, pl.ds(off, tk)], ones_rows], axis=0)
        m_old = m_scr[...]
        m_new = jnp.maximum(m_old, jnp.max(s, axis=0, keepdims=True))
        alpha = jnp.exp2(m_old - m_new)
        p = jnp.exp2(s - m_new).astype(BF16)
        acc_scr[...] = alpha * acc_scr[...] + jnp.dot(vtb, p, preferred_element_type=F32)
        m_scr[...] = m_new

    sa_scr[...] = scores(0)

    def pair(i, carry):
        j = 2 * i
        sb_scr[...] = scores(j + 1)
        consume(sa_scr[...], j)
        sa_scr[...] = scores(j + 2)
        consume(sb_scr[...], j + 1)
        return carry

    if n_kv % 2 == 1:
        lax.fori_loop(0, (n_kv - 1) // 2, pair, 0)
        consume(sa_scr[...], n_kv - 1)
    else:
        lax.fori_loop(0, n_kv // 2 - 1, pair, 0)
        sb_scr[...] = scores(n_kv - 1)
        consume(sa_scr[...], n_kv - 2)
        consume(sb_scr[...], n_kv - 1)

    lv = lamv_ref[...]
    lam = (jnp.exp(jnp.sum(lv[0:1] * lv[1:2], axis=-1, keepdims=True))
           - jnp.exp(jnp.sum(lv[2:3] * lv[3:4], axis=-1, keepdims=True)) + lam_init)
    ot = acc_scr[:A_V_DIM, :] / acc_scr[A_V_DIM:A_V_DIM + 1, :]
    o = (ot[:, :tq] - lam * ot[:, tq:]).T
    a = o * lax.rsqrt(jnp.mean(o * o, axis=-1, keepdims=True) + EPS) * gs_ref[...]
    o_ref[...] = (a * (1.0 - lam_init)).astype(BF16)


def _pick_tile(n, candidates):
    for c in candidates:
        if n % c == 0:
            return c
    raise ValueError(f"no tile for {n}")


def _diff_attention(q, k_all, vt_all, lamv, g_subln, lam_init):
    n = q.shape[0]
    nk = k_all.shape[0]
    tq = 256
    tk = _pick_tile(nk, (1408, 768, 512, 256))
    return pl.pallas_call(
        functools.partial(_attn_kernel, tk=tk, lam_init=lam_init),
        grid=(A_HEADS, n // tq),
        in_specs=[
            pl.BlockSpec((tq, A_V_DIM), lambda h, i: (i, h)),
            pl.BlockSpec((nk, A_V_DIM), lambda h, i: (0, h)),
            pl.BlockSpec((A_V_DIM, nk), lambda h, i: (h, 0)),
            pl.BlockSpec((8, LANES), lambda h, i: (0, 0)),
            pl.BlockSpec((1, A_V_DIM), lambda h, i: (0, 0)),
        ],
        out_specs=pl.BlockSpec((tq, A_V_DIM), lambda h, i: (i, h)),
        out_shape=jax.ShapeDtypeStruct((n, A_WIDTH), BF16),
        scratch_shapes=[pltpu.VMEM((1, 2 * tq), F32),
                        pltpu.VMEM((A_V_DIM + 16, 2 * tq), F32),
                        pltpu.VMEM((tk, 2 * tq), F32), pltpu.VMEM((tk, 2 * tq), F32)],
        compiler_params=_params(("arbitrary", "arbitrary")),
        name="diff_attention",
    )(q, k_all, vt_all, lamv, g_subln)


def _finish_kernel(a_ref, uv_ref, x_ref, wout_ref, ws_ref, bs_ref, lng_ref, lnb_ref, gm_ref, o_ref, cat_scr):
    tm = a_ref.shape[0]
    cat_scr[:, :A_WIDTH] = a_ref[...]
    for c in range(tm // CHUNK):
        rows = slice(c * CHUNK, (c + 1) * CHUNK)
        for g in range(B_GROUPS):
            cols = slice(g * B_GROUP_W, (g + 1) * B_GROUP_W)
            u = uv_ref[rows, g * B_GROUP_W:(g + 1) * B_GROUP_W]
            v = uv_ref[rows, B_WIDTH + g * B_GROUP_W:B_WIDTH + (g + 1) * B_GROUP_W]
            mu = jnp.mean(v, axis=-1, keepdims=True)
            var = jnp.mean(jnp.square(v - mu), axis=-1, keepdims=True)
            vn = (v - mu) * lax.rsqrt(var + EPS) * lng_ref[:, cols] + lnb_ref[:, cols]
            mixed = jnp.dot(ws_ref[g], vn.astype(BF16), preferred_element_type=F32) + bs_ref[g]
            cat_scr[rows, A_WIDTH + g * B_GROUP_W:A_WIDTH + (g + 1) * B_GROUP_W] = (u * mixed).astype(BF16)
    y = jnp.dot(cat_scr[...], wout_ref[...], preferred_element_type=F32)
    o_ref[...] = x_ref[...] + gm_ref[...] * y


def _finish_even(a, uv, x, w_out_bf16, ws_bf16, bs_b, ln_g, ln_b, gm):
    n, d = x.shape
    tm = min(512, n)
    row = lambda i: (0, 0)
    return pl.pallas_call(
        _finish_kernel,
        grid=(n // tm,),
        in_specs=[
            pl.BlockSpec((tm, A_WIDTH), lambda i: (i, 0)),
            pl.BlockSpec((tm, 2 * B_WIDTH), lambda i: (i, 0)),
            pl.BlockSpec((tm, d), lambda i: (i, 0)),
            pl.BlockSpec((A_WIDTH + B_WIDTH, d), row),
            pl.BlockSpec((B_GROUPS, CHUNK, CHUNK), lambda i: (0, 0, 0)),
            pl.BlockSpec((B_GROUPS, CHUNK, B_GROUP_W), lambda i: (0, 0, 0)),
            pl.BlockSpec((1, B_WIDTH), row), pl.BlockSpec((1, B_WIDTH), row),
            pl.BlockSpec((1, d), row),
        ],
        out_specs=pl.BlockSpec((tm, d), lambda i: (i, 0)),
        out_shape=jax.ShapeDtypeStruct((n, d), F32),
        scratch_shapes=[pltpu.VMEM((tm, A_WIDTH + B_WIDTH), BF16)],
        compiler_params=_params(("arbitrary",)),
        name="finish_even",
    )(a, uv, x, w_out_bf16, ws_bf16, bs_b, ln_g, ln_b, gm)


def _router_kernel(x_ref, g_ref, sh_ref, sc_ref, wrt_ref, ht_ref, aff_ref):
    h = _norm_mod(x_ref[...], g_ref[...], sh_ref[...], sc_ref[...])
    logits = _nt_dot(wrt_ref[...], h.astype(BF16))
    m = jnp.max(logits, axis=0, keepdims=True)
    e = jnp.exp(logits - m)
    aff_ref[...] = e / jnp.sum(e, axis=0, keepdims=True)
    ht_ref[...] = h


def _router(x, g, shift, scale, w_router_t_bf16):
    n, d = x.shape
    tm = min(512, n)
    row = lambda i: (0, 0)
    return pl.pallas_call(
        _router_kernel,
        grid=(n // tm,),
        in_specs=[
            pl.BlockSpec((tm, d), lambda i: (i, 0)),
            pl.BlockSpec((1, d), row), pl.BlockSpec((1, d), row), pl.BlockSpec((1, d), row),
            pl.BlockSpec((N_EXPERTS, d), row),
        ],
        out_specs=[pl.BlockSpec((tm, d), lambda i: (i, 0)), pl.BlockSpec((N_EXPERTS, tm), lambda i: (0, i))],
        out_shape=[jax.ShapeDtypeStruct((n, d), F32), jax.ShapeDtypeStruct((N_EXPERTS, n), F32)],
        compiler_params=_params(("arbitrary",)),
        name="router",
    )(x, g, shift, scale, w_router_t_bf16)


def _select_kernel(aff_ref, sel_ref, pos_ref, g_ref, rank_ref, cnt_ref, *, cap):
    aff = aff_ref[...]
    n = aff.shape[1]
    idx = lax.broadcasted_iota(I32, aff.shape, 1)
    capf = float(cap)

    def count(mask):
        return jnp.sum(jnp.where(mask, 1.0, 0.0), axis=1, keepdims=True)

    def as_float(bits):
        return pltpu.bitcast(bits, F32)

    def thr_body(i, thr):
        cand = thr | jnp.left_shift(jnp.int32(1), 30 - i)
        return jnp.where(count(aff >= as_float(cand)) >= capf, cand, thr)

    thr = lax.fori_loop(0, 31, thr_body, jnp.zeros((aff.shape[0], 1), I32))
    gt = aff >= as_float(thr + 1)
    eq = (aff >= as_float(thr)) & jnp.logical_not(gt)
    need = capf - count(gt)
    nbits = int(n).bit_length()

    def tie_body(i, lim):
        cand = lim | jnp.left_shift(jnp.int32(1), nbits - 1 - i)
        return jnp.where(count(eq & (idx < cand)) <= need, cand, lim)

    lim = lax.fori_loop(0, nbits, tie_body, jnp.zeros((aff.shape[0], 1), I32))
    sel = jnp.where(gt | (eq & (idx < lim)), 1.0, 0.0)
    sel_ref[...] = sel
    g_ref[...] = aff * sel

    ri = lax.broadcasted_iota(I32, (LANES, LANES), 0)
    ci = lax.broadcasted_iota(I32, (LANES, LANES), 1)
    upper = jnp.where(ri < ci, 1.0, 0.0).astype(BF16)
    carry = jnp.zeros((aff.shape[0], 1), F32)
    for c in range(n // LANES):
        m = sel[:, c * LANES:(c + 1) * LANES]
        within = jnp.dot(m.astype(BF16), upper, preferred_element_type=F32)
        pos_ref[:, c * LANES:(c + 1) * LANES] = (within + carry).astype(I32)
        carry = carry + jnp.sum(m, axis=1, keepdims=True)

    running = jnp.zeros((1, n), F32)
    for e in range(aff.shape[0]):
        rank_ref[e:e + 1, :] = running.astype(I32)
        running = running + sel[e:e + 1, :]
    cnt_ref[...] = jnp.broadcast_to(running, cnt_ref.shape)


def _select(aff_t, cap):
    e, n = aff_t.shape
    full = lambda: (0, 0)
    return pl.pallas_call(
        functools.partial(_select_kernel, cap=cap),
        grid=(),
        in_specs=[pl.BlockSpec((e, n), full)],
        out_specs=[pl.BlockSpec((e, n), full)] * 4 + [pl.BlockSpec((8, n), full)],
        out_shape=[jax.ShapeDtypeStruct((e, n), F32), jax.ShapeDtypeStruct((e, n), I32),
                   jax.ShapeDtypeStruct((e, n), F32), jax.ShapeDtypeStruct((e, n), I32),
                   jax.ShapeDtypeStruct((8, n), F32)],
        compiler_params=pltpu.CompilerParams(vmem_limit_bytes=VMEM_LIMIT_BYTES),
        name="expert_select",
    )(aff_t)


def _one_hot_window(pos_row, sel_row, lo, base, win=ROUTE_WIN):
    rel = jnp.where((sel_row > 0.0) & (pos_row >= lo), pos_row - base, -1)
    r = lax.broadcasted_iota(I32, (win, pos_row.shape[1]), 0)
    return jnp.where(r == rel, 1.0, 0.0).astype(BF16)


def _window_plan(p0, p1, win=ROUTE_WIN):
    start = (p0 // LANES) * LANES
    n_chunks = jnp.where(p1 > p0, (p1 - start + win - 1) // win, 0)
    return start, n_chunks


def _slot_meta_kernel(ps_ref, pe_ref, sel_ref, pos_ref, g_ref, rank_ref, idx_ref, rnk_ref, gcol_ref, ms_scr,
                      *, tiles_per_iter):
    e = pl.program_id(0)
    capp = ms_scr.shape[1]
    n_tiles = sel_ref.shape[1]
    ms_scr[...] = jnp.zeros(ms_scr.shape, F32)

    def add_window(t, k):
        start, _ = _window_plan(ps_ref[e, t], pe_ref[e, t])
        g_row = g_ref[0, pl.ds(t, 1), :]
        g_hi = g_row.astype(BF16)
        r1 = g_row - g_hi.astype(F32)
        g_mid = r1.astype(BF16)
        g_lo = (r1 - g_mid.astype(F32)).astype(BF16)
        tok = t * ROUTE_TILE + lax.broadcasted_iota(I32, (1, ROUTE_TILE), 1)
        as_bf16 = lambda v: v.astype(F32).astype(BF16)
        meta = jnp.concatenate([g_hi, g_mid, g_lo, as_bf16(tok % 256), as_bf16(tok // 256),
                                as_bf16(rank_ref[0, pl.ds(t, 1), :]), jnp.zeros((2, ROUTE_TILE), BF16)], axis=0)
        lo = start + k * ROUTE_WIN
        base = pl.multiple_of(jnp.minimum(lo, capp - ROUTE_WIN), LANES)
        onehot = _one_hot_window(pos_ref[0, pl.ds(t, 1), :], sel_ref[0, pl.ds(t, 1), :], lo, base)
        ms_scr[:, pl.ds(base, ROUTE_WIN)] += _nt_dot(meta, onehot)

    def body(i, carry):
        tiles = [i * tiles_per_iter + u for u in range(tiles_per_iter)]
        for t in tiles:
            add_window(t, 0)
        for t in tiles:
            n_chunks = _window_plan(ps_ref[e, t], pe_ref[e, t])[1]
            lax.fori_loop(1, n_chunks, lambda k, c, t=t: (add_window(t, k), c)[1], 0)
        return carry

    lax.fori_loop(0, n_tiles // tiles_per_iter, body, 0)

    ms = ms_scr[...]
    idx_ref[0] = (ms[4:5] * 256.0 + ms[3:4]).astype(I32)
    rnk_ref[0] = ms[5:6].astype(I32)
    gate = ms[0:1] + ms[1:2] + ms[2:3]
    for b in range(capp // LANES):
        gcol_ref[0, b * LANES:(b + 1) * LANES, :] = jnp.broadcast_to(gate[:, b * LANES:(b + 1) * LANES], (LANES, LANES)).T


def _slot_meta(tile_start, tile_end, sel_t, pos_t, g_t, rank_t, capp):
    e, n_tiles, _ = sel_t.shape
    blk = pl.BlockSpec((1, n_tiles, ROUTE_TILE), lambda i, ps, pe: (i, 0, 0))
    row = pl.BlockSpec((1, 1, capp), lambda i, ps, pe: (i, 0, 0))
    return pl.pallas_call(
        functools.partial(_slot_meta_kernel, tiles_per_iter=4 if n_tiles % 4 == 0 else 1),
        grid_spec=pltpu.PrefetchScalarGridSpec(
            num_scalar_prefetch=2,
            grid=(e,),
            in_specs=[blk, blk, blk, blk],
            out_specs=[row, row, pl.BlockSpec((1, capp, LANES), lambda i, ps, pe: (i, 0, 0))],
            scratch_shapes=[pltpu.VMEM((8, capp), F32)],
        ),
        out_shape=[jax.ShapeDtypeStruct((e, 1, capp), I32), jax.ShapeDtypeStruct((e, 1, capp), I32),
                   jax.ShapeDtypeStruct((e, capp, LANES), F32)],
        compiler_params=_params(("arbitrary",)),
        name="moe_slot_meta",
    )(tile_start, tile_end, sel_t, pos_t, g_t, rank_t)


def _row_copy(src_hbm, src_row, dst_ref, dst_row, sem):
    return pltpu.make_async_copy(src_hbm.at[pl.ds(src_row, 1), :], dst_ref.at[pl.ds(dst_row, 1), :], sem)


def _gather_kernel(idx_ref, h_hbm, xs_hbm, sem):
    e = pl.program_id(0)
    capp = idx_ref.shape[1]

    def start(j, carry):
        _row_copy(h_hbm, idx_ref[e, j], xs_hbm.at[e], j, sem).start()
        return carry

    def wait(j, carry):
        _row_copy(h_hbm, 0, xs_hbm.at[e], j, sem).wait()
        return carry

    lax.fori_loop(0, capp, start, 0)
    lax.fori_loop(0, capp, wait, 0)


def _gather_rows(idx, h):
    e, capp = idx.shape
    n, d = h.shape
    return pl.pallas_call(
        _gather_kernel,
        grid_spec=pltpu.PrefetchScalarGridSpec(
            num_scalar_prefetch=1,
            grid=(e,),
            in_specs=[pl.BlockSpec(memory_space=pl.ANY)],
            out_specs=pl.BlockSpec(memory_space=pl.ANY),
            scratch_shapes=[pltpu.SemaphoreType.DMA],
        ),
        out_shape=jax.ShapeDtypeStruct((e, capp, d), F32),
        compiler_params=_params(("arbitrary",)),
        name="moe_gather",
    )(idx, h)


def _ffn_up_kernel(xs_ref, wg_ref, wu_ref, h_ref):
    xs = xs_ref[0].astype(BF16)
    a = jnp.dot(xs, wg_ref[0].astype(BF16), preferred_element_type=F32)
    u = jnp.dot(xs, wu_ref[0].astype(BF16), preferred_element_type=F32)
    h_ref[0] = (a * jax.nn.sigmoid(a) * u).astype(BF16)


def _ffn_up(xs, w_gate, w_up, layer):
    e, capp, d = xs.shape
    f = w_gate.shape[3]
    tf = 512
    return pl.pallas_call(
        _ffn_up_kernel,
        grid=(e, f // tf),
        in_specs=[
            pl.BlockSpec((1, capp, d), lambda i, j: (i, 0, 0)),
            pl.BlockSpec((None, 1, d, tf), lambda i, j: (layer, i, 0, j)),
            pl.BlockSpec((None, 1, d, tf), lambda i, j: (layer, i, 0, j)),
        ],
        out_specs=pl.BlockSpec((1, capp, tf), lambda i, j: (i, 0, j)),
        out_shape=jax.ShapeDtypeStruct((e, capp, f), BF16),
        compiler_params=_params(("arbitrary", "arbitrary")),
        name="moe_ffn_up",
    )(xs, w_gate, w_up)


def _ffn_down_kernel(h_ref, wd_ref, gcol_ref, y_ref):
    y = jnp.dot(h_ref[0], wd_ref[0].astype(BF16), preferred_element_type=F32)
    gate = gcol_ref[0]
    for c in range(y.shape[1] // LANES):
        cols = slice(c * LANES, (c + 1) * LANES)
        y_ref[0, :, cols] = y[:, cols] * gate


def _ffn_down(h, w_down, gcol, layer):
    e, capp, f = h.shape
    d = w_down.shape[3]
    td = 512
    return pl.pallas_call(
        _ffn_down_kernel,
        grid=(e, d // td),
        in_specs=[
            pl.BlockSpec((1, capp, f), lambda i, j: (i, 0, 0)),
            pl.BlockSpec((None, 1, f, td), lambda i, j: (layer, i, 0, j)),
            pl.BlockSpec((1, capp, LANES), lambda i, j: (i, 0, 0)),
        ],
        out_specs=pl.BlockSpec((1, capp, td), lambda i, j: (i, 0, j)),
        out_shape=jax.ShapeDtypeStruct((e, capp, d), F32),
        compiler_params=_params(("arbitrary", "arbitrary")),
        name="moe_ffn_down",
    )(h, w_down, gcol)


COMBINE_TOKENS = 64


def _combine_kernel(ts_ref, te_ref, idx_ref, rnk_ref, cnt_ref, y_hbm, o_ref, z_scr, sems):
    i = pl.program_id(0)
    n_steps = pl.num_programs(0)
    n_exp, capp = idx_ref.shape
    tt = o_ref.shape[0]

    def start_tile(tile, buf):
        for e in range(n_exp):
            def start(j, carry, e=e):
                dst = z_scr.at[buf, rnk_ref[e, j]]
                _row_copy(y_hbm, e * capp + j, dst, idx_ref[e, j] - tile * tt, sems.at[buf]).start()
                return carry
            lax.fori_loop(ts_ref[e, tile], te_ref[e, tile], start, 0)

    def wait_tile(tile, buf):
        rows = ts_ref[0, tile] * 0
        for e in range(n_exp):
            rows = rows + te_ref[e, tile] - ts_ref[e, tile]

        def wait(r, carry):
            _row_copy(y_hbm, 0, z_scr.at[buf, 0], 0, sems.at[buf]).wait()
            return carry

        lax.fori_loop(0, rows, wait, 0)

    @pl.when(i == 0)
    def _():
        z_scr[...] = jnp.zeros(z_scr.shape, F32)
        start_tile(0, 0)

    @pl.when(i + 1 < n_steps)
    def _():
        start_tile(i + 1, (i + 1) % 2)

    buf = i % 2
    wait_tile(i, buf)

    cnt_row = cnt_ref[0]
    r = lax.broadcasted_iota(I32, (tt, tt), 0)
    c = lax.broadcasted_iota(I32, (tt, tt), 1)
    cnt_col = jnp.sum(jnp.where(r == c, jnp.broadcast_to(cnt_row, (tt, tt)), 0.0), axis=1, keepdims=True)
    k_max = jnp.max(cnt_row).astype(I32)
    o_ref[...] = jnp.zeros(o_ref.shape, F32)

    def add_lane(k, carry):
        o_ref[...] += jnp.where(k.astype(F32) < cnt_col, z_scr[buf, k], 0.0)
        return carry

    lax.fori_loop(0, k_max, add_lane, 0)


def _combine(tile_start, tile_end, idx, rnk, cnt, y, n):
    e, capp, d = y.shape
    tt = COMBINE_TOKENS
    return pl.pallas_call(
        _combine_kernel,
        grid_spec=pltpu.PrefetchScalarGridSpec(
            num_scalar_prefetch=4,
            grid=(n // tt,),
            in_specs=[pl.BlockSpec((1, 1, tt), lambda i, *_: (i, 0, 0)), pl.BlockSpec(memory_space=pl.ANY)],
            out_specs=pl.BlockSpec((tt, d), lambda i, *_: (i, 0)),
            scratch_shapes=[pltpu.VMEM((2, e, tt, d), F32), pltpu.SemaphoreType.DMA((2,))],
        ),
        out_shape=jax.ShapeDtypeStruct((n, d), F32),
        compiler_params=_params(("arbitrary",)),
        name="moe_combine",
    )(tile_start, tile_end, idx, rnk, cnt[0].reshape(n // tt, 1, tt), y.reshape(e * capp, d))


def _tile_bounds(pos, tile, cap):
    start = pos[:, ::tile]
    end = jnp.concatenate([start[:, 1:], jnp.full((pos.shape[0], 1), cap, I32)], axis=1)
    return start, end


def _expert_choice_ffn(x, g, shift, scale, w_router, w_gate, w_up, w_down, layer):
    n, d = x.shape
    cap = CAPACITY_FACTOR * n // N_EXPERTS
    capp = max(cap, ROUTE_WIN)
    h, aff_t = _router(x, g, shift, scale, w_router.T.astype(BF16))
    sel, pos, gsel, rank, cnt = _select(aff_t, cap)
    as_tiles = lambda a: a.reshape(N_EXPERTS, n // ROUTE_TILE, ROUTE_TILE)
    idx, rnk, gcol = _slot_meta(*_tile_bounds(pos, ROUTE_TILE, cap), as_tiles(sel), as_tiles(pos), as_tiles(gsel),
                                as_tiles(rank), capp)
    idx = idx.reshape(N_EXPERTS, capp)
    xs = _gather_rows(idx, h)
    hidden = _ffn_up(xs, w_gate, w_up, layer)
    y = _ffn_down(hidden, w_down, gcol, layer)
    return _combine(*_tile_bounds(pos, COMBINE_TOKENS, cap), idx, rnk.reshape(N_EXPERTS, capp), cnt, y, n)


def _resid_kernel(x_ref, y_ref, gate_ref, o_ref):
    o_ref[...] = x_ref[...] + gate_ref[...] * y_ref[...]


def _residual(x, y, gate):
    n, d = x.shape
    tm = min(512, n)
    return pl.pallas_call(
        _resid_kernel,
        grid=(n // tm,),
        in_specs=[pl.BlockSpec((tm, d), lambda i: (i, 0)), pl.BlockSpec((tm, d), lambda i: (i, 0)),
                  pl.BlockSpec((1, d), lambda i: (0, 0))],
        out_specs=pl.BlockSpec((tm, d), lambda i: (i, 0)),
        out_shape=jax.ShapeDtypeStruct((n, d), F32),
        compiler_params=_params(("arbitrary",)),
        name="residual",
    )(x, y, gate)


def _final_kernel(x_ref, y_ref, gate_ref, g_ref, o_ref):
    x = x_ref[...] + gate_ref[...] * y_ref[...]
    o_ref[...] = x * lax.rsqrt(jnp.mean(x * x, axis=-1, keepdims=True) + EPS) * g_ref[...]


def _final_norm(x, y, gate, g_final):
    n, d = x.shape
    tm = min(512, n)
    row = lambda i: (0, 0)
    return pl.pallas_call(
        _final_kernel,
        grid=(n // tm,),
        in_specs=[pl.BlockSpec((tm, d), lambda i: (i, 0)), pl.BlockSpec((tm, d), lambda i: (i, 0)),
                  pl.BlockSpec((1, d), row), pl.BlockSpec((1, d), row)],
        out_specs=pl.BlockSpec((tm, d), lambda i: (i, 0)),
        out_shape=jax.ShapeDtypeStruct((n, d), F32),
        compiler_params=_params(("arbitrary",)),
        name="final_norm",
    )(x, y, gate, g_final)


FFT_K1_BLK = 16


def _fourier_in_kernel(x_ref, y_ref, gate_ref, g_ref, sh_ref, sc_ref, cs_ref, perm_ref, x1_ref, a_ref, b_ref):
    x1 = x_ref[...] + gate_ref[...] * y_ref[...]
    x1_ref[...] = x1
    h = _norm_mod(x1, g_ref[...], sh_ref[...], sc_ref[...]).astype(BF16)
    hp = jnp.dot(perm_ref[...], h, preferred_element_type=F32).astype(BF16)
    n2, t1_blk = a_ref.shape[0], a_ref.shape[1]
    for g in range(C_GROUPS):
        cols = slice(g * C_GROUP_W, (g + 1) * C_GROUP_W)
        ab = jnp.dot(hp[:, cols], cs_ref[...], preferred_element_type=F32)
        a_ref[:, :, cols] = ab[:, :C_GROUP_W].astype(BF16).reshape(n2, t1_blk, C_GROUP_W)
        b_ref[:, :, cols] = ab[:, C_GROUP_W:].astype(BF16).reshape(n2, t1_blk, C_GROUP_W)


def _fourier_in(x, moe, gate, g, shift, scale, cs):
    n, d = x.shape
    tm = 512
    n1 = n // FFT_N2
    t1_blk = tm // FFT_N2
    row = lambda i: (0, 0)
    tile = pl.BlockSpec((tm, d), lambda i: (i, 0))
    ab_blk = pl.BlockSpec((FFT_N2, t1_blk, d), lambda i: (0, i, 0))
    ab_shape = jax.ShapeDtypeStruct((FFT_N2, n1, d), BF16)
    perm = np.zeros((tm, tm), np.float32)
    src = np.arange(tm)
    perm[(src % FFT_N2) * t1_blk + src // FFT_N2, src] = 1.0
    return pl.pallas_call(
        _fourier_in_kernel,
        grid=(n // tm,),
        in_specs=[tile, tile, pl.BlockSpec((1, d), row), pl.BlockSpec((1, d), row), pl.BlockSpec((1, d), row),
                  pl.BlockSpec((1, d), row), pl.BlockSpec((C_GROUP_W, 2 * C_GROUP_W), row),
                  pl.BlockSpec((tm, tm), row)],
        out_specs=[tile, ab_blk, ab_blk],
        out_shape=[jax.ShapeDtypeStruct((n, d), F32), ab_shape, ab_shape],
        compiler_params=_params(("arbitrary",)),
        name="fourier_channel_dft",
    )(x, moe, gate, g, shift, scale, cs, jnp.asarray(perm, BF16))


def _fourier_stage1_kernel(a_ref, b_ref, ma_ref, mb_ref, ct_ref, st_ref, zr_ref, zi_ref):
    n1 = a_ref.shape[1]
    z = (jnp.dot(ma_ref[...], a_ref[0], preferred_element_type=F32)
         + jnp.dot(mb_ref[...], b_ref[0], preferred_element_type=F32))
    ct = ct_ref[0]
    st = st_ref[0]
    for c in range(a_ref.shape[2] // LANES):
        cols = slice(c * LANES, (c + 1) * LANES)
        zr = z[:n1, cols]
        zi = z[n1:, cols]
        zr_ref[0, :, cols] = (zr * ct + zi * st).astype(BF16)
        zi_ref[0, :, cols] = (zi * ct - zr * st).astype(BF16)


def _fourier_stage1(a3, b3, ma, mb, ct, st):
    n2, n1, d = a3.shape
    blk = pl.BlockSpec((1, n1, d), lambda j: (j, 0, 0))
    mat = pl.BlockSpec((2 * n1, n1), lambda j: (0, 0))
    tw = pl.BlockSpec((1, n1, LANES), lambda j: (j, 0, 0))
    shp = jax.ShapeDtypeStruct((n2, n1, d), BF16)
    return pl.pallas_call(
        _fourier_stage1_kernel,
        grid=(n2,),
        in_specs=[blk, blk, mat, mat, tw, tw],
        out_specs=[blk, blk],
        out_shape=[shp, shp],
        compiler_params=_params(("arbitrary",)),
        name="fourier_stage1",
    )(a3, b3, ma, mb, ct, st)


def _fourier_out_kernel(zr_ref, zi_ref, bc_ref, bs_ref, wo_ref, x_ref, gm_ref, o_ref):
    rows = zr_ref.shape[0] * zr_ref.shape[1]
    d = zr_ref.shape[2]
    zr = zr_ref[...].reshape(rows, d)
    zi = zi_ref[...].reshape(rows, d)
    f = (jnp.dot(bc_ref[...], zr, preferred_element_type=F32)
         + jnp.dot(bs_ref[...], zi, preferred_element_type=F32))
    y = jnp.dot(f.astype(BF16), wo_ref[...], preferred_element_type=F32)
    o_ref[...] = x_ref[...] + gm_ref[...] * y.reshape(o_ref.shape)


def _fourier_out(zr3, zi3, bd_c, bd_s, w_o_bf16, x, gm):
    n2, n1, d = zr3.shape
    n = n1 * n2
    rows = n2 * FFT_K1_BLK
    x3 = x.reshape(n2, n1, d)
    blk = pl.BlockSpec((n2, FFT_K1_BLK, d), lambda i: (0, i, 0))
    const = lambda i: (0, 0)
    out = pl.pallas_call(
        _fourier_out_kernel,
        grid=(n1 // FFT_K1_BLK,),
        in_specs=[blk, blk, pl.BlockSpec((rows, rows), const), pl.BlockSpec((rows, rows), const),
                  pl.BlockSpec((d, d), const), blk, pl.BlockSpec((1, d), const)],
        out_specs=blk,
        out_shape=jax.ShapeDtypeStruct((n2, n1, d), F32),
        compiler_params=_params(("arbitrary",)),
        name="fourier_stage2_out",
    )(zr3, zi3, bd_c, bd_s, w_o_bf16, x3, gm)
    return out.reshape(n, d)


def _fourier_constants(n):
    n1, n2 = n // FFT_N2, FFT_N2
    two_pi = 2.0 * np.pi

    def angles(a, b, period):
        return two_pi * ((np.outer(a, b) % period).astype(np.float64) / period)

    kc = np.arange(C_GROUP_W)
    ang = angles(kc, kc, C_GROUP_W)
    cs = np.concatenate([np.cos(ang), np.sin(ang)], axis=1) / np.sqrt(C_GROUP_W)
    k1 = np.arange(n1)
    ang1 = angles(k1, k1, n1)
    c1, s1 = np.cos(ang1) / np.sqrt(n), np.sin(ang1) / np.sqrt(n)
    ma = np.concatenate([c1, -s1], axis=0)
    mb = np.concatenate([-s1, -c1], axis=0)
    t2 = np.arange(n2)
    angt = angles(t2, k1, n)
    ct = np.repeat(np.cos(angt)[:, :, None], LANES, axis=2)
    st = np.repeat(np.sin(angt)[:, :, None], LANES, axis=2)
    ang2 = angles(t2, t2, n2)
    k1_blk = FFT_K1_BLK
    bd_c = np.zeros((n2 * k1_blk, n2 * k1_blk))
    bd_s = np.zeros((n2 * k1_blk, n2 * k1_blk))
    for kl in range(k1_blk):
        bd_c[kl::k1_blk, kl::k1_blk] = np.cos(ang2)
        bd_s[kl::k1_blk, kl::k1_blk] = np.sin(ang2)
    bf = lambda a: jnp.asarray(a, F32).astype(BF16)
    return bf(cs), bf(ma), bf(mb), jnp.asarray(ct, F32), jnp.asarray(st, F32), bf(bd_c), bf(bd_s)


def _fourier_mix_layer(x, moe, gate, g, shift, scale, gm, w_o):
    cs, ma, mb, ct, st, bd_c, bd_s = _fourier_constants(x.shape[0])
    x1, a3, b3 = _fourier_in(x, moe, gate, g, shift, scale, cs)
    zr3, zi3 = _fourier_stage1(a3, b3, ma, mb, ct, st)
    return _fourier_out(zr3, zi3, bd_c, bd_s, w_o.astype(BF16), x1, gm)


def _even_layer_mix(x, ctx, mods, g_mix, w_in, w_out, lamv, g_subln, ln_g, ln_b, w_s, b_s, lam_init, need_ctx_out):
    sm, cm, gm = mods["lat"][0:3]
    smc, cmc, gmc = mods["ctx"][0:3]
    n = x.shape[0]
    w_in_b = w_in.astype(BF16)
    w_out_b = w_out.astype(BF16)
    ws_b = w_s.astype(BF16)
    bs_b = jnp.broadcast_to(b_s[:, :, None], (B_GROUPS, CHUNK, B_GROUP_W))
    cos_l, sin_l = _rope_tables(n, True)
    cos_c, sin_c = _rope_tables(ctx.shape[0], False)
    w_v_t = w_in[:, 2 * QK_COLS:2 * QK_COLS + A_WIDTH].T.astype(BF16)
    q_l, k_l, vt_l, uv_l = _in_projection(x, g_mix, sm, cm, w_in_b, w_v_t, cos_l, sin_l)
    q_c, k_c, vt_c, uv_c = _in_projection(ctx, g_mix, smc, cmc, w_in_b, w_v_t, cos_c, sin_c)
    k_all = jnp.concatenate([k_c, k_l], axis=0)
    vt_all = jnp.concatenate([vt_c, vt_l], axis=1)
    a_l = _diff_attention(q_l, k_all, vt_all, lamv, g_subln, lam_init)
    x = _finish_even(a_l, uv_l, x, w_out_b, ws_b, bs_b, ln_g, ln_b, gm)
    if need_ctx_out:
        a_c = _diff_attention(q_c, k_c, vt_c, lamv, g_subln, lam_init)
        ctx = _finish_even(a_c, uv_c, ctx, w_out_b, ws_b, bs_b, ln_g, ln_b, gmc)
    return x, ctx


def kernel(x, c, ctx, c_ctx, w_mod, b_mod, g_norm_mix, g_norm_ffn, w_in, w_out, lam_q1, lam_k1, lam_q2, lam_k2,
           g_subln, sgu_ln_g, sgu_ln_b, w_spatial, b_spatial, w_fourier_out, w_router, w_gate, w_up, w_down, g_final):
    assert x.shape[0] == 1 and DEPTH == 2
    d = D_MODEL
    x2 = x[0]
    ctx2 = ctx[0]
    c8 = jnp.zeros((8, d), F32).at[0].set(c[0]).at[1].set(c_ctx)
    mod_all = _modulation(c8, w_mod, b_mod)
    row = lambda v: v.reshape(1, -1)

    def mods_of(i):
        lat = [mod_all[i, 0:1, k * d:(k + 1) * d] for k in range(6)]
        cx = [mod_all[i, 1:2, k * d:(k + 1) * d] for k in range(6)]
        return {"lat": lat, "ctx": cx}

    m0 = mods_of(0)
    lam_init0 = 0.8 - 0.6 * math.exp(-0.3 * 0)
    lamv = jnp.zeros((8, LANES), F32)
    for r, v in enumerate((lam_q1[0], lam_k1[0], lam_q2[0], lam_k2[0])):
        lamv = lamv.at[r, :A_HEAD_DIM].set(v)
    x2, ctx2 = _even_layer_mix(x2, ctx2, m0, row(g_norm_mix[0]), w_in[0], w_out[0], lamv, row(g_subln[0]),
                               row(sgu_ln_g[0]), row(sgu_ln_b[0]), w_spatial[0], b_spatial[0], lam_init0, True)
    moe_x = _expert_choice_ffn(x2, row(g_norm_ffn[0]), m0["lat"][3], m0["lat"][4],
                               w_router[0], w_gate, w_up, w_down, 0)
    moe_c = _expert_choice_ffn(ctx2, row(g_norm_ffn[0]), m0["ctx"][3], m0["ctx"][4],
                               w_router[0], w_gate, w_up, w_down, 0)
    ctx2 = _residual(ctx2, moe_c, m0["ctx"][5])

    m1 = mods_of(1)
    x2 = _fourier_mix_layer(x2, moe_x, m0["lat"][5], row(g_norm_mix[1]), m1["lat"][0], m1["lat"][1],
                            m1["lat"][2], w_fourier_out[0])
    moe_x = _expert_choice_ffn(x2, row(g_norm_ffn[1]), m1["lat"][3], m1["lat"][4],
                               w_router[1], w_gate, w_up, w_down, 1)
    out = _final_norm(x2, moe_x, m1["lat"][5], row(g_final))
    del ctx2
    return out[None]
```

```python
import functools
import math

import numpy as np
import jax
import jax.numpy as jnp
from jax import lax
from jax.experimental import pallas as pl
from jax.experimental.pallas import tpu as pltpu

F32 = jnp.float32
BF16 = jnp.bfloat16
I32 = jnp.int32

D_MODEL = 2048
DEPTH = 2
GRID_W = 64
EPS = 1e-6

A_HEADS = 8
A_HEAD_DIM = 64
A_V_DIM = 2 * A_HEAD_DIM
A_WIDTH = A_HEADS * A_V_DIM
QK_COLS = A_HEADS * 2 * A_HEAD_DIM
ROPE_THETA = 10000.0

B_GROUPS = 8
B_GROUP_W = 128
B_WIDTH = B_GROUPS * B_GROUP_W
CHUNK = 128
IN_COLS = 2 * QK_COLS + A_WIDTH + 2 * B_WIDTH

C_GROUPS = 4
C_GROUP_W = D_MODEL // C_GROUPS

N_EXPERTS = 16
CAPACITY_FACTOR = 2
F_EXPERT = D_MODEL // 2

LANES = 128
MXU_DIM = 256
VMEM_LIMIT_BYTES = 56 * 1024 * 1024

FFT_N2 = 32
ROUTE_TILE = 256
ROUTE_WIN = 256


def _params(sem):
    return pltpu.CompilerParams(dimension_semantics=sem, vmem_limit_bytes=VMEM_LIMIT_BYTES)


def _nt_dot(a, b):
    return lax.dot_general(a, b, (((1,), (1,)), ((), ())), preferred_element_type=F32)


def _norm_mod(x, g, shift, scale):
    ms = jnp.mean(x * x, axis=-1, keepdims=True)
    y = x * lax.rsqrt(ms + EPS) * g
    return y * (1.0 + scale) + shift


def _mod_kernel(c_ref, w_ref, b_ref, o_ref):
    c = c_ref[...]
    s = c * jax.nn.sigmoid(c)
    o_ref[...] = jnp.dot(s.astype(BF16), w_ref[...].astype(BF16), preferred_element_type=F32) + b_ref[...]


def _modulation(c8, w_mod, b_mod):
    depth, d, n6 = w_mod.shape
    tn = 1024
    return pl.pallas_call(
        _mod_kernel,
        grid=(depth, n6 // tn),
        in_specs=[
            pl.BlockSpec((8, d), lambda l, j: (0, 0)),
            pl.BlockSpec((None, d, tn), lambda l, j: (l, 0, j)),
            pl.BlockSpec((None, 1, tn), lambda l, j: (l, 0, j)),
        ],
        out_specs=pl.BlockSpec((None, 8, tn), lambda l, j: (l, 0, j)),
        out_shape=jax.ShapeDtypeStruct((depth, 8, n6), F32),
        compiler_params=_params(("arbitrary", "arbitrary")),
        name="modulation",
    )(c8, w_mod, b_mod.reshape(depth, 1, n6))


PROJ_TN = 512
_Q_TILES = QK_COLS // PROJ_TN
_V_TILES = A_WIDTH // PROJ_TN
_UV_TILES = 2 * B_WIDTH // PROJ_TN


def _rope_chunk(z, cos, sin_signed, first_half):
    partner = jnp.where(first_half, pltpu.roll(z, LANES - 16, 1), pltpu.roll(z, 16, 1))
    return z * cos + partner * sin_signed


def _proj_kernel(x_ref, g_ref, sh_ref, sc_ref, w_ref, wvt_ref, cos_ref, sin_ref,
                 q_ref, k_ref, vt_ref, uv_ref, h_scr):
    j = pl.program_id(1)

    @pl.when(j == 0)
    def _():
        h_scr[...] = _norm_mod(x_ref[...], g_ref[...], sh_ref[...], sc_ref[...]).astype(BF16)

    def project():
        return jnp.dot(h_scr[...], w_ref[...], preferred_element_type=F32)

    def roped(scale):
        z = project()
        cos = cos_ref[...]
        sin = sin_ref[...]
        lane = lax.broadcasted_iota(I32, cos.shape, 1)
        first_half = (lane % 32) < 16
        parts = []
        for c in range(PROJ_TN // LANES):
            zc = z[:, c * LANES:(c + 1) * LANES]
            parts.append(_rope_chunk(zc, cos, sin, first_half) * scale)
        return jnp.concatenate(parts, axis=1)

    @pl.when(j < _Q_TILES)
    def _():
        q_ref[...] = roped(A_HEAD_DIM ** -0.5 * math.log2(math.e)).astype(BF16)

    @pl.when((j >= _Q_TILES) & (j < 2 * _Q_TILES))
    def _():
        k_ref[...] = roped(1.0).astype(BF16)

    @pl.when((j >= 2 * _Q_TILES) & (j < 2 * _Q_TILES + _V_TILES))
    def _():
        vt_ref[...] = _nt_dot(wvt_ref[...], h_scr[...]).astype(BF16)

    @pl.when(j >= 2 * _Q_TILES + _V_TILES)
    def _():
        uv_ref[...] = jax.nn.gelu(project())


def _in_projection(x, g, shift, scale, w_in_bf16, w_v_t_bf16, cos_t, sin_t):
    n, d = x.shape
    tm = min(1024, n)
    nq = _Q_TILES
    nj = IN_COLS // PROJ_TN
    row = lambda i, j: (0, 0)
    v_tile = lambda j: jnp.clip(j - 2 * nq, 0, _V_TILES - 1)
    return pl.pallas_call(
        _proj_kernel,
        grid=(n // tm, nj),
        in_specs=[
            pl.BlockSpec((tm, d), lambda i, j: (i, 0)),
            pl.BlockSpec((1, d), row), pl.BlockSpec((1, d), row), pl.BlockSpec((1, d), row),
            pl.BlockSpec((d, PROJ_TN), lambda i, j: (0, j)),
            pl.BlockSpec((PROJ_TN, d), lambda i, j: (v_tile(j), 0)),
            pl.BlockSpec((tm, LANES), lambda i, j: (i, 0)),
            pl.BlockSpec((tm, LANES), lambda i, j: (i, 0)),
        ],
        out_specs=[
            pl.BlockSpec((tm, PROJ_TN), lambda i, j: (i, jnp.clip(j, 0, nq - 1))),
            pl.BlockSpec((tm, PROJ_TN), lambda i, j: (i, jnp.clip(j - nq, 0, nq - 1))),
            pl.BlockSpec((PROJ_TN, tm), lambda i, j: (v_tile(j), i)),
            pl.BlockSpec((tm, PROJ_TN), lambda i, j: (i, jnp.clip(j - 2 * nq - _V_TILES, 0, _UV_TILES - 1))),
        ],
        out_shape=[
            jax.ShapeDtypeStruct((n, QK_COLS), BF16),
            jax.ShapeDtypeStruct((n, QK_COLS), BF16),
            jax.ShapeDtypeStruct((A_WIDTH, n), BF16),
            jax.ShapeDtypeStruct((n, 2 * B_WIDTH), F32),
        ],
        scratch_shapes=[pltpu.VMEM((tm, d), BF16)],
        compiler_params=_params(("arbitrary", "arbitrary")),
        name="in_projection",
    )(x, g, shift, scale, w_in_bf16, w_v_t_bf16, cos_t, sin_t)


def _rope_tables(n, rotate):
    if not rotate:
        return jnp.ones((n, LANES), F32), jnp.zeros((n, LANES), F32)
    t = np.arange(n)
    row = (t // GRID_W).astype(np.float32)
    col = (t % GRID_W).astype(np.float32)
    dim = A_HEAD_DIM // 2
    inv = (np.float32(ROPE_THETA) ** (-np.arange(0, dim, 2, dtype=np.float32) / np.float32(dim))).astype(np.float32)
    ang_r = row[:, None] * inv[None, :]
    ang_c = col[:, None] * inv[None, :]
    ang64 = np.concatenate([ang_r, ang_r, ang_c, ang_c], axis=1)
    sign64 = np.concatenate([-np.ones(16), np.ones(16), -np.ones(16), np.ones(16)]).astype(np.float32)
    ang = np.tile(ang64, (1, LANES // 64))
    sign = np.tile(sign64, LANES // 64)
    return jnp.asarray(np.cos(ang), F32), jnp.asarray(np.sin(ang) * sign[None, :], F32)


def _attn_kernel(q_ref, k_ref, vt_ref, lamv_ref, gs_ref, o_ref, m_scr, acc_scr, sa_scr, sb_scr, *, tk, lam_init):
    tq = q_ref.shape[0]
    n_kv = k_ref.shape[0] // tk
    qt = q_ref[...].astype(F32).T
    row = lax.broadcasted_iota(I32, qt.shape, 0)
    zero = jnp.zeros_like(qt)
    qst = jnp.concatenate([jnp.where(row < A_HEAD_DIM, qt, zero),
                           jnp.where(row >= A_HEAD_DIM, qt, zero)], axis=1).astype(BF16)
    m_scr[...] = jnp.full(m_scr.shape, -jnp.inf, F32)
    acc_scr[...] = jnp.zeros(acc_scr.shape, F32)

    def scores(j):
        off = pl.multiple_of(j * tk, tk)
        return jnp.dot(k_ref[pl.ds(off, tk), :], qst, preferred_element_type=F32)

    ones_rows = jnp.ones((16, tk), BF16)

    def consume(s, j):
        off = pl.multiple_of(j * tk, tk)
        vtb = jnp.concatenate([vt_ref[:, pl.ds(off, tk)], ones_rows], axis=0)
        m_old = m_scr[...]
        m_new = jnp.maximum(m_old, jnp.max(s, axis=0, keepdims=True))
        alpha = jnp.exp2(m_old - m_new)
        p = jnp.exp2(s - m_new).astype(BF16)
        acc_scr[...] = alpha * acc_scr[...] + jnp.dot(vtb, p, preferred_element_type=F32)
        m_scr[...] = m_new

    sa_scr[...] = scores(0)

    def pair(i, carry):
        j = 2 * i
        sb_scr[...] = scores(j + 1)
        consume(sa_scr[...], j)
        sa_scr[...] = scores(j + 2)
        consume(sb_scr[...], j + 1)
        return carry

    if n_kv % 2 == 1:
        lax.fori_loop(0, (n_kv - 1) // 2, pair, 0)
        consume(sa_scr[...], n_kv - 1)
    else:
        lax.fori_loop(0, n_kv // 2 - 1, pair, 0)
        sb_scr[...] = scores(n_kv - 1)
        consume(sa_scr[...], n_kv - 2)
        consume(sb_scr[...], n_kv - 1)

    lv = lamv_ref[...]
    lam = (jnp.exp(jnp.sum(lv[0:1] * lv[1:2], axis=-1, keepdims=True))
           - jnp.exp(jnp.sum(lv[2:3] * lv[3:4], axis=-1, keepdims=True)) + lam_init)
    ot = acc_scr[:A_V_DIM, :] / acc_scr[A_V_DIM:A_V_DIM + 1, :]
    o = (ot[:, :tq] - lam * ot[:, tq:]).T
    a = o * lax.rsqrt(jnp.mean(o * o, axis=-1, keepdims=True) + EPS) * gs_ref[...]
    o_ref[...] = (a * (1.0 - lam_init)).astype(BF16)


def _pick_tile(n, candidates):
    for c in candidates:
        if n % c == 0:
            return c
    raise ValueError(f"no tile for {n}")


def _diff_attention(q, k_all, vt_all, lamv, g_subln, lam_init):
    n = q.shape[0]
    nk = k_all.shape[0]
    tq = 256
    tk = _pick_tile(nk, (1408, 768, 512, 256))
    return pl.pallas_call(
        functools.partial(_attn_kernel, tk=tk, lam_init=lam_init),
        grid=(A_HEADS, n // tq),
        in_specs=[
            pl.BlockSpec((tq, A_V_DIM), lambda h, i: (i, h)),
            pl.BlockSpec((nk, A_V_DIM), lambda h, i: (0, h)),
            pl.BlockSpec((A_V_DIM, nk), lambda h, i: (h, 0)),
            pl.BlockSpec((8, LANES), lambda h, i: (0, 0)),
            pl.BlockSpec((1, A_V_DIM), lambda h, i: (0, 0)),
        ],
        out_specs=pl.BlockSpec((tq, A_V_DIM), lambda h, i: (i, h)),
        out_shape=jax.ShapeDtypeStruct((n, A_WIDTH), BF16),
        scratch_shapes=[pltpu.VMEM((1, 2 * tq), F32),
                        pltpu.VMEM((A_V_DIM + 16, 2 * tq), F32),
                        pltpu.VMEM((tk, 2 * tq), F32), pltpu.VMEM((tk, 2 * tq), F32)],
        compiler_params=_params(("arbitrary", "arbitrary")),
        name="diff_attention",
    )(q, k_all, vt_all, lamv, g_subln)


def _finish_kernel(a_ref, uv_ref, x_ref, wout_ref, ws_ref, bs_ref, lng_ref, lnb_ref, gm_ref, o_ref, cat_scr):
    tm = a_ref.shape[0]
    cat_scr[:, :A_WIDTH] = a_ref[...]
    for c in range(tm // CHUNK):
        rows = slice(c * CHUNK, (c + 1) * CHUNK)
        for g in range(B_GROUPS):
            cols = slice(g * B_GROUP_W, (g + 1) * B_GROUP_W)
            u = uv_ref[rows, g * B_GROUP_W:(g + 1) * B_GROUP_W]
            v = uv_ref[rows, B_WIDTH + g * B_GROUP_W:B_WIDTH + (g + 1) * B_GROUP_W]
            mu = jnp.mean(v, axis=-1, keepdims=True)
            var = jnp.mean(jnp.square(v - mu), axis=-1, keepdims=True)
            vn = (v - mu) * lax.rsqrt(var + EPS) * lng_ref[:, cols] + lnb_ref[:, cols]
            mixed = jnp.dot(ws_ref[g], vn.astype(BF16), preferred_element_type=F32) + bs_ref[g]
            cat_scr[rows, A_WIDTH + g * B_GROUP_W:A_WIDTH + (g + 1) * B_GROUP_W] = (u * mixed).astype(BF16)
    y = jnp.dot(cat_scr[...], wout_ref[...], preferred_element_type=F32)
    o_ref[...] = x_ref[...] + gm_ref[...] * y


def _finish_even(a, uv, x, w_out_bf16, ws_bf16, bs_b, ln_g, ln_b, gm):
    n, d = x.shape
    tm = min(512, n)
    row = lambda i: (0, 0)
    return pl.pallas_call(
        _finish_kernel,
        grid=(n // tm,),
        in_specs=[
            pl.BlockSpec((tm, A_WIDTH), lambda i: (i, 0)),
            pl.BlockSpec((tm, 2 * B_WIDTH), lambda i: (i, 0)),
            pl.BlockSpec((tm, d), lambda i: (i, 0)),
            pl.BlockSpec((A_WIDTH + B_WIDTH, d), row),
            pl.BlockSpec((B_GROUPS, CHUNK, CHUNK), lambda i: (0, 0, 0)),
            pl.BlockSpec((B_GROUPS, CHUNK, B_GROUP_W), lambda i: (0, 0, 0)),
            pl.BlockSpec((1, B_WIDTH), row), pl.BlockSpec((1, B_WIDTH), row),
            pl.BlockSpec((1, d), row),
        ],
        out_specs=pl.BlockSpec((tm, d), lambda i: (i, 0)),
        out_shape=jax.ShapeDtypeStruct((n, d), F32),
        scratch_shapes=[pltpu.VMEM((tm, A_WIDTH + B_WIDTH), BF16)],
        compiler_params=_params(("arbitrary",)),
        name="finish_even",
    )(a, uv, x, w_out_bf16, ws_bf16, bs_b, ln_g, ln_b, gm)


def _router_kernel(x_ref, g_ref, sh_ref, sc_ref, wrt_ref, ht_ref, aff_ref):
    h = _norm_mod(x_ref[...], g_ref[...], sh_ref[...], sc_ref[...])
    logits = _nt_dot(wrt_ref[...], h.astype(BF16))
    m = jnp.max(logits, axis=0, keepdims=True)
    e = jnp.exp(logits - m)
    aff_ref[...] = e / jnp.sum(e, axis=0, keepdims=True)
    ht_ref[...] = h.T.astype(BF16)


def _router(x, g, shift, scale, w_router_t_bf16):
    n, d = x.shape
    tm = min(512, n)
    row = lambda i: (0, 0)
    return pl.pallas_call(
        _router_kernel,
        grid=(n // tm,),
        in_specs=[
            pl.BlockSpec((tm, d), lambda i: (i, 0)),
            pl.BlockSpec((1, d), row), pl.BlockSpec((1, d), row), pl.BlockSpec((1, d), row),
            pl.BlockSpec((N_EXPERTS, d), row),
        ],
        out_specs=[pl.BlockSpec((d, tm), lambda i: (0, i)), pl.BlockSpec((N_EXPERTS, tm), lambda i: (0, i))],
        out_shape=[jax.ShapeDtypeStruct((d, n), BF16), jax.ShapeDtypeStruct((N_EXPERTS, n), F32)],
        compiler_params=_params(("arbitrary",)),
        name="router",
    )(x, g, shift, scale, w_router_t_bf16)


def _select_kernel(aff_ref, sel_ref, pos_ref, g_ref, first_ref, *, cap, capp):
    aff = aff_ref[...]
    n = aff.shape[1]
    idx = lax.broadcasted_iota(I32, aff.shape, 1)
    capf = float(cap)

    def count(mask):
        return jnp.sum(jnp.where(mask, 1.0, 0.0), axis=1, keepdims=True)

    def as_float(bits):
        return pltpu.bitcast(bits, F32)

    def thr_body(i, thr):
        cand = thr | jnp.left_shift(jnp.int32(1), 30 - i)
        return jnp.where(count(aff >= as_float(cand)) >= capf, cand, thr)

    thr = lax.fori_loop(0, 31, thr_body, jnp.zeros((aff.shape[0], 1), I32))
    gt = aff >= as_float(thr + 1)
    eq = (aff >= as_float(thr)) & jnp.logical_not(gt)
    need = capf - count(gt)
    nbits = int(n).bit_length()

    def tie_body(i, lim):
        cand = lim | jnp.left_shift(jnp.int32(1), nbits - 1 - i)
        return jnp.where(count(eq & (idx < cand)) <= need, cand, lim)

    lim = lax.fori_loop(0, nbits, tie_body, jnp.zeros((aff.shape[0], 1), I32))
    sel = jnp.where(gt | (eq & (idx < lim)), 1.0, 0.0)
    sel_ref[...] = sel
    g_ref[...] = aff * sel

    ri = lax.broadcasted_iota(I32, (LANES, LANES), 0)
    ci = lax.broadcasted_iota(I32, (LANES, LANES), 1)
    upper = jnp.where(ri < ci, 1.0, 0.0).astype(BF16)
    carry = jnp.zeros((aff.shape[0], 1), F32)
    for c in range(n // LANES):
        m = sel[:, c * LANES:(c + 1) * LANES]
        within = jnp.dot(m.astype(BF16), upper, preferred_element_type=F32)
        pos_ref[:, c * LANES:(c + 1) * LANES] = (within + carry).astype(I32)
        carry = carry + jnp.sum(m, axis=1, keepdims=True)

    incl = pos_ref[...].astype(F32) + sel
    lane = lax.broadcasted_iota(I32, first_ref.shape, 1)
    first = jnp.zeros(first_ref.shape, F32)
    for w in range(capp // ROUTE_WIN + 1):
        first = jnp.where(lane == w, count(incl <= float(w * ROUTE_WIN)), first)
    first_ref[...] = first.astype(I32)


def _select(aff_t, cap, capp):
    e, n = aff_t.shape
    full = lambda: (0, 0)
    return pl.pallas_call(
        functools.partial(_select_kernel, cap=cap, capp=capp),
        grid=(),
        in_specs=[pl.BlockSpec((e, n), full)],
        out_specs=[pl.BlockSpec((e, n), full)] * 3 + [pl.BlockSpec((e, LANES), full)],
        out_shape=[jax.ShapeDtypeStruct((e, n), F32), jax.ShapeDtypeStruct((e, n), I32),
                   jax.ShapeDtypeStruct((e, n), F32), jax.ShapeDtypeStruct((e, LANES), I32)],
        compiler_params=pltpu.CompilerParams(vmem_limit_bytes=VMEM_LIMIT_BYTES),
        name="expert_select",
    )(aff_t)


def _one_hot_window(pos_row, sel_row, lo, base, win=ROUTE_WIN):
    rel = jnp.where((sel_row > 0.0) & (pos_row >= lo), pos_row - base, -1)
    r = lax.broadcasted_iota(I32, (win, pos_row.shape[1]), 0)
    return jnp.where(r == rel, 1.0, 0.0).astype(BF16)


def _window_plan(p0, p1, win=ROUTE_WIN):
    start = (p0 // LANES) * LANES
    n_chunks = jnp.where(p1 > p0, (p1 - start + win - 1) // win, 0)
    return start, n_chunks


def _dispatch_kernel(first_ref, ht_hbm, sel_ref, pos_ref, g_ref, xs_ref, gslot_ref, ht_scr, acc_scr, gs_scr, sem,
                     *, k_chunk):
    e = pl.program_id(0)
    w = pl.program_id(1)
    n = ht_scr.shape[1]

    @pl.when((e == 0) & (w == 0))
    def _():
        load = pltpu.make_async_copy(ht_hbm, ht_scr, sem)
        load.start()
        load.wait()

    t0 = first_ref[e, w]
    t1 = first_ref[e, w + 1]
    start = (t0 // ROUTE_TILE) * ROUTE_TILE
    n_chunks = jnp.where(t1 > t0, (t1 - start + k_chunk - 1) // k_chunk, 0)
    slot0 = w * ROUTE_WIN

    def chunk(k):
        lo_tok = start + k * k_chunk
        base = pl.multiple_of(jnp.minimum(lo_tok, n - k_chunk), ROUTE_TILE)
        toks = pl.ds(base, k_chunk)
        tok = base + lax.broadcasted_iota(I32, (1, k_chunk), 1)
        pos_row = pos_ref[0, :, toks]
        keep = (sel_ref[0, :, toks] > 0.0) & (tok >= lo_tok)
        rel = jnp.where(keep, pos_row - slot0, -1)
        r = lax.broadcasted_iota(I32, (ROUTE_WIN, k_chunk), 0)
        onehot = jnp.where(r == rel, 1.0, 0.0).astype(BF16)
        g_row = g_ref[0, :, toks]
        g_hi = g_row.astype(BF16)
        r1 = g_row - g_hi.astype(F32)
        g_mid = r1.astype(BF16)
        g_lo = (r1 - g_mid.astype(F32)).astype(BF16)
        g8 = jnp.concatenate([g_hi, g_mid, g_lo, jnp.zeros((5, k_chunk), BF16)], axis=0)
        return _nt_dot(ht_scr[:, toks], onehot), _nt_dot(g8, onehot)

    acc_scr[...], gs_scr[...] = chunk(0)

    def extra(k, carry):
        x, gs = chunk(k)
        acc_scr[...] += x
        gs_scr[...] += gs
        return carry

    lax.fori_loop(1, n_chunks, extra, 0)
    xs_ref[0] = acc_scr[...].T.astype(BF16)
    gs = gs_scr[...]
    gslot_ref[0] = gs[0:1] + gs[1:2] + gs[2:3]


def _dispatch(first_tok, h_t, sel3, pos3, g3, capp):
    d, n = h_t.shape
    k_chunk = min(10 * ROUTE_TILE, n)
    row3 = pl.BlockSpec((1, 1, n), lambda e, w, ft: (e, 0, 0))
    return pl.pallas_call(
        functools.partial(_dispatch_kernel, k_chunk=k_chunk),
        grid_spec=pltpu.PrefetchScalarGridSpec(
            num_scalar_prefetch=1,
            grid=(N_EXPERTS, capp // ROUTE_WIN),
            in_specs=[pl.BlockSpec(memory_space=pl.ANY), row3, row3, row3],
            out_specs=[
                pl.BlockSpec((1, ROUTE_WIN, d), lambda e, w, ft: (e, w, 0)),
                pl.BlockSpec((1, 1, ROUTE_WIN), lambda e, w, ft: (e, 0, w)),
            ],
            scratch_shapes=[pltpu.VMEM((d, n), BF16), pltpu.VMEM((d, ROUTE_WIN), F32),
                            pltpu.VMEM((8, ROUTE_WIN), F32), pltpu.SemaphoreType.DMA],
        ),
        out_shape=[jax.ShapeDtypeStruct((N_EXPERTS, capp, d), BF16),
                   jax.ShapeDtypeStruct((N_EXPERTS, 1, capp), F32)],
        compiler_params=_params(("arbitrary", "arbitrary")),
        name="moe_dispatch",
    )(first_tok, h_t, sel3, pos3, g3)


def _ffn_up_kernel(xs_ref, wg_ref, wu_ref, h_ref):
    xs = xs_ref[0].astype(BF16)
    a = jnp.dot(xs, wg_ref[0].astype(BF16), preferred_element_type=F32)
    u = jnp.dot(xs, wu_ref[0].astype(BF16), preferred_element_type=F32)
    h_ref[0] = (a * jax.nn.sigmoid(a) * u).astype(BF16)


def _ffn_up(xs, w_gate, w_up, layer):
    e, capp, d = xs.shape
    f = w_gate.shape[3]
    tf = 512
    return pl.pallas_call(
        _ffn_up_kernel,
        grid=(e, f // tf),
        in_specs=[
            pl.BlockSpec((1, capp, d), lambda i, j: (i, 0, 0)),
            pl.BlockSpec((None, 1, d, tf), lambda i, j: (layer, i, 0, j)),
            pl.BlockSpec((None, 1, d, tf), lambda i, j: (layer, i, 0, j)),
        ],
        out_specs=pl.BlockSpec((1, capp, tf), lambda i, j: (i, 0, j)),
        out_shape=jax.ShapeDtypeStruct((e, capp, f), BF16),
        compiler_params=_params(("arbitrary", "arbitrary")),
        name="moe_ffn_up",
    )(xs, w_gate, w_up)


def _ffn_down_kernel(h_ref, wd_ref, gslot_ref, y_ref):
    y = jnp.dot(h_ref[0], wd_ref[0].astype(BF16), preferred_element_type=F32)
    y_ref[0] = (y.T * gslot_ref[0]).astype(BF16)


def _ffn_down(h, w_down, gslot, layer):
    e, capp, f = h.shape
    d = w_down.shape[3]
    td = 512
    return pl.pallas_call(
        _ffn_down_kernel,
        grid=(e, d // td),
        in_specs=[
            pl.BlockSpec((1, capp, f), lambda i, j: (i, 0, 0)),
            pl.BlockSpec((None, 1, f, td), lambda i, j: (layer, i, 0, j)),
            pl.BlockSpec((1, 1, capp), lambda i, j: (i, 0, 0)),
        ],
        out_specs=pl.BlockSpec((1, td, capp), lambda i, j: (i, j, 0)),
        out_shape=jax.ShapeDtypeStruct((e, d, capp), BF16),
        compiler_params=_params(("arbitrary", "arbitrary")),
        name="moe_ffn_down",
    )(h, w_down, gslot)


COMBINE_TD = 512
COMBINE_WIN = ROUTE_WIN


def _combine_kernel(ps_ref, pe_ref, y_ref, sel_ref, pos_ref, o_ref, acc_scr):
    t = pl.program_id(1)
    n_exp, _, capp = y_ref.shape

    def window(e, k):
        start, _ = _window_plan(ps_ref[e, t], pe_ref[e, t], COMBINE_WIN)
        lo = start + k * COMBINE_WIN
        base = pl.multiple_of(jnp.minimum(lo, capp - COMBINE_WIN), LANES)
        onehot = _one_hot_window(pos_ref[e, pl.ds(t, 1), :], sel_ref[e, pl.ds(t, 1), :], lo, base, COMBINE_WIN)
        return [y_ref[e, :, pl.ds(base, COMBINE_WIN)]], [onehot]

    lhs, rhs = [], []
    for e in range(n_exp):
        rows, hots = window(e, 0)
        lhs += rows
        rhs += hots
    acc_scr[...] = jnp.dot(jnp.concatenate(lhs, axis=1), jnp.concatenate(rhs, axis=0), preferred_element_type=F32)

    for e in range(n_exp):
        n_chunks = _window_plan(ps_ref[e, t], pe_ref[e, t], COMBINE_WIN)[1]

        def extra(k, carry, e=e):
            rows, hots = window(e, k)
            acc_scr[...] += jnp.dot(jnp.concatenate(rows, axis=1), jnp.concatenate(hots, axis=0),
                                    preferred_element_type=F32)
            return carry

        lax.fori_loop(1, n_chunks, extra, 0)

    o_ref[...] = acc_scr[...].T


def _combine(tile_start, tile_end, y_t, sel_t, pos_t, n):
    e, d, capp = y_t.shape
    n_tiles = n // ROUTE_TILE
    td = COMBINE_TD
    y_blk = pl.BlockSpec((e, td, capp), lambda j, t, ps, pe: (0, j, 0))
    full = pl.BlockSpec((e, n_tiles, ROUTE_TILE), lambda j, t, ps, pe: (0, 0, 0))
    return pl.pallas_call(
        _combine_kernel,
        grid_spec=pltpu.PrefetchScalarGridSpec(
            num_scalar_prefetch=2,
            grid=(d // td, n_tiles),
            in_specs=[y_blk, full, full],
            out_specs=pl.BlockSpec((ROUTE_TILE, td), lambda j, t, ps, pe: (t, j)),
            scratch_shapes=[pltpu.VMEM((td, ROUTE_TILE), F32)],
        ),
        out_shape=jax.ShapeDtypeStruct((n, d), F32),
        compiler_params=_params(("arbitrary", "arbitrary")),
        name="moe_combine",
    )(tile_start, tile_end, y_t, sel_t, pos_t)


def _tile_bounds(pos, tile, cap):
    start = pos[:, ::tile]
    end = jnp.concatenate([start[:, 1:], jnp.full((pos.shape[0], 1), cap, I32)], axis=1)
    return start, end


def _expert_choice_ffn(x, g, shift, scale, w_router, w_gate, w_up, w_down, layer):
    n, d = x.shape
    cap = CAPACITY_FACTOR * n // N_EXPERTS
    capp = max(cap, ROUTE_WIN)
    h_t, aff_t = _router(x, g, shift, scale, w_router.T.astype(BF16))
    sel, pos, gsel, first_tok = _select(aff_t, cap, capp)
    as3 = lambda a: a.reshape(N_EXPERTS, 1, n)
    xs, gslot = _dispatch(first_tok, h_t, as3(sel), as3(pos), as3(gsel), capp)
    hidden = _ffn_up(xs, w_gate, w_up, layer)
    y_t = _ffn_down(hidden, w_down, gslot, layer)
    as_tiles = lambda a: a.reshape(N_EXPERTS, n // ROUTE_TILE, ROUTE_TILE)
    return _combine(*_tile_bounds(pos, ROUTE_TILE, cap), y_t, as_tiles(sel), as_tiles(pos), n)


def _resid_kernel(x_ref, y_ref, gate_ref, o_ref):
    o_ref[...] = x_ref[...] + gate_ref[...] * y_ref[...]


def _residual(x, y, gate):
    n, d = x.shape
    tm = min(512, n)
    return pl.pallas_call(
        _resid_kernel,
        grid=(n // tm,),
        in_specs=[pl.BlockSpec((tm, d), lambda i: (i, 0)), pl.BlockSpec((tm, d), lambda i: (i, 0)),
                  pl.BlockSpec((1, d), lambda i: (0, 0))],
        out_specs=pl.BlockSpec((tm, d), lambda i: (i, 0)),
        out_shape=jax.ShapeDtypeStruct((n, d), F32),
        compiler_params=_params(("arbitrary",)),
        name="residual",
    )(x, y, gate)


def _final_kernel(x_ref, y_ref, gate_ref, g_ref, o_ref):
    x = x_ref[...] + gate_ref[...] * y_ref[...]
    o_ref[...] = x * lax.rsqrt(jnp.mean(x * x, axis=-1, keepdims=True) + EPS) * g_ref[...]


def _final_norm(x, y, gate, g_final):
    n, d = x.shape
    tm = min(512, n)
    row = lambda i: (0, 0)
    return pl.pallas_call(
        _final_kernel,
        grid=(n // tm,),
        in_specs=[pl.BlockSpec((tm, d), lambda i: (i, 0)), pl.BlockSpec((tm, d), lambda i: (i, 0)),
                  pl.BlockSpec((1, d), row), pl.BlockSpec((1, d), row)],
        out_specs=pl.BlockSpec((tm, d), lambda i: (i, 0)),
        out_shape=jax.ShapeDtypeStruct((n, d), F32),
        compiler_params=_params(("arbitrary",)),
        name="final_norm",
    )(x, y, gate, g_final)


FFT_K1_BLK = 16


def _fourier_in_kernel(x_ref, y_ref, gate_ref, g_ref, sh_ref, sc_ref, cs_ref, perm_ref, x1_ref, a_ref, b_ref):
    x1 = x_ref[...] + gate_ref[...] * y_ref[...]
    x1_ref[...] = x1
    h = _norm_mod(x1, g_ref[...], sh_ref[...], sc_ref[...]).astype(BF16)
    hp = jnp.dot(perm_ref[...], h, preferred_element_type=F32).astype(BF16)
    n2, t1_blk = a_ref.shape[0], a_ref.shape[1]
    for g in range(C_GROUPS):
        cols = slice(g * C_GROUP_W, (g + 1) * C_GROUP_W)
        ab = jnp.dot(hp[:, cols], cs_ref[...], preferred_element_type=F32)
        a_ref[:, :, cols] = ab[:, :C_GROUP_W].astype(BF16).reshape(n2, t1_blk, C_GROUP_W)
        b_ref[:, :, cols] = ab[:, C_GROUP_W:].astype(BF16).reshape(n2, t1_blk, C_GROUP_W)


def _fourier_in(x, moe, gate, g, shift, scale, cs):
    n, d = x.shape
    tm = 512
    n1 = n // FFT_N2
    t1_blk = tm // FFT_N2
    row = lambda i: (0, 0)
    tile = pl.BlockSpec((tm, d), lambda i: (i, 0))
    ab_blk = pl.BlockSpec((FFT_N2, t1_blk, d), lambda i: (0, i, 0))
    ab_shape = jax.ShapeDtypeStruct((FFT_N2, n1, d), BF16)
    perm = np.zeros((tm, tm), np.float32)
    src = np.arange(tm)
    perm[(src % FFT_N2) * t1_blk + src // FFT_N2, src] = 1.0
    return pl.pallas_call(
        _fourier_in_kernel,
        grid=(n // tm,),
        in_specs=[tile, tile, pl.BlockSpec((1, d), row), pl.BlockSpec((1, d), row), pl.BlockSpec((1, d), row),
                  pl.BlockSpec((1, d), row), pl.BlockSpec((C_GROUP_W, 2 * C_GROUP_W), row),
                  pl.BlockSpec((tm, tm), row)],
        out_specs=[tile, ab_blk, ab_blk],
        out_shape=[jax.ShapeDtypeStruct((n, d), F32), ab_shape, ab_shape],
        compiler_params=_params(("arbitrary",)),
        name="fourier_channel_dft",
    )(x, moe, gate, g, shift, scale, cs, jnp.asarray(perm, BF16))


def _fourier_stage1_kernel(a_ref, b_ref, ma_ref, mb_ref, ct_ref, st_ref, zr_ref, zi_ref):
    n1 = a_ref.shape[1]
    z = (jnp.dot(ma_ref[...], a_ref[0], preferred_element_type=F32)
         + jnp.dot(mb_ref[...], b_ref[0], preferred_element_type=F32))
    ct = ct_ref[0]
    st = st_ref[0]
    for c in range(a_ref.shape[2] // LANES):
        cols = slice(c * LANES, (c + 1) * LANES)
        zr = z[:n1, cols]
        zi = z[n1:, cols]
        zr_ref[0, :, cols] = (zr * ct + zi * st).astype(BF16)
        zi_ref[0, :, cols] = (zi * ct - zr * st).astype(BF16)


def _fourier_stage1(a3, b3, ma, mb, ct, st):
    n2, n1, d = a3.shape
    blk = pl.BlockSpec((1, n1, d), lambda j: (j, 0, 0))
    mat = pl.BlockSpec((2 * n1, n1), lambda j: (0, 0))
    tw = pl.BlockSpec((1, n1, LANES), lambda j: (j, 0, 0))
    shp = jax.ShapeDtypeStruct((n2, n1, d), BF16)
    return pl.pallas_call(
        _fourier_stage1_kernel,
        grid=(n2,),
        in_specs=[blk, blk, mat, mat, tw, tw],
        out_specs=[blk, blk],
        out_shape=[shp, shp],
        compiler_params=_params(("arbitrary",)),
        name="fourier_stage1",
    )(a3, b3, ma, mb, ct, st)


def _fourier_out_kernel(zr_ref, zi_ref, bc_ref, bs_ref, wo_ref, x_ref, gm_ref, o_ref):
    rows = zr_ref.shape[0] * zr_ref.shape[1]
    d = zr_ref.shape[2]
    zr = zr_ref[...].reshape(rows, d)
    zi = zi_ref[...].reshape(rows, d)
    f = (jnp.dot(bc_ref[...], zr, preferred_element_type=F32)
         + jnp.dot(bs_ref[...], zi, preferred_element_type=F32))
    y = jnp.dot(f.astype(BF16), wo_ref[...], preferred_element_type=F32)
    o_ref[...] = x_ref[...] + gm_ref[...] * y.reshape(o_ref.shape)


def _fourier_out(zr3, zi3, bd_c, bd_s, w_o_bf16, x, gm):
    n2, n1, d = zr3.shape
    n = n1 * n2
    rows = n2 * FFT_K1_BLK
    x3 = x.reshape(n2, n1, d)
    blk = pl.BlockSpec((n2, FFT_K1_BLK, d), lambda i: (0, i, 0))
    const = lambda i: (0, 0)
    out = pl.pallas_call(
        _fourier_out_kernel,
        grid=(n1 // FFT_K1_BLK,),
        in_specs=[blk, blk, pl.BlockSpec((rows, rows), const), pl.BlockSpec((rows, rows), const),
                  pl.BlockSpec((d, d), const), blk, pl.BlockSpec((1, d), const)],
        out_specs=blk,
        out_shape=jax.ShapeDtypeStruct((n2, n1, d), F32),
        compiler_params=_params(("arbitrary",)),
        name="fourier_stage2_out",
    )(zr3, zi3, bd_c, bd_s, w_o_bf16, x3, gm)
    return out.reshape(n, d)


def _fourier_constants(n):
    n1, n2 = n // FFT_N2, FFT_N2
    two_pi = 2.0 * np.pi

    def angles(a, b, period):
        return two_pi * ((np.outer(a, b) % period).astype(np.float64) / period)

    kc = np.arange(C_GROUP_W)
    ang = angles(kc, kc, C_GROUP_W)
    cs = np.concatenate([np.cos(ang), np.sin(ang)], axis=1) / np.sqrt(C_GROUP_W)
    k1 = np.arange(n1)
    ang1 = angles(k1, k1, n1)
    c1, s1 = np.cos(ang1) / np.sqrt(n), np.sin(ang1) / np.sqrt(n)
    ma = np.concatenate([c1, -s1], axis=0)
    mb = np.concatenate([-s1, -c1], axis=0)
    t2 = np.arange(n2)
    angt = angles(t2, k1, n)
    ct = np.repeat(np.cos(angt)[:, :, None], LANES, axis=2)
    st = np.repeat(np.sin(angt)[:, :, None], LANES, axis=2)
    ang2 = angles(t2, t2, n2)
    k1_blk = FFT_K1_BLK
    bd_c = np.zeros((n2 * k1_blk, n2 * k1_blk))
    bd_s = np.zeros((n2 * k1_blk, n2 * k1_blk))
    for kl in range(k1_blk):
        bd_c[kl::k1_blk, kl::k1_blk] = np.cos(ang2)
        bd_s[kl::k1_blk, kl::k1_blk] = np.sin(ang2)
    bf = lambda a: jnp.asarray(a, F32).astype(BF16)
    return bf(cs), bf(ma), bf(mb), jnp.asarray(ct, F32), jnp.asarray(st, F32), bf(bd_c), bf(bd_s)


def _fourier_mix_layer(x, moe, gate, g, shift, scale, gm, w_o):
    cs, ma, mb, ct, st, bd_c, bd_s = _fourier_constants(x.shape[0])
    x1, a3, b3 = _fourier_in(x, moe, gate, g, shift, scale, cs)
    zr3, zi3 = _fourier_stage1(a3, b3, ma, mb, ct, st)
    return _fourier_out(zr3, zi3, bd_c, bd_s, w_o.astype(BF16), x1, gm)


def _even_layer_mix(x, ctx, mods, g_mix, w_in, w_out, lamv, g_subln, ln_g, ln_b, w_s, b_s, lam_init, need_ctx_out):
    sm, cm, gm = mods["lat"][0:3]
    smc, cmc, gmc = mods["ctx"][0:3]
    n = x.shape[0]
    w_in_b = w_in.astype(BF16)
    w_out_b = w_out.astype(BF16)
    ws_b = w_s.astype(BF16)
    bs_b = jnp.broadcast_to(b_s[:, :, None], (B_GROUPS, CHUNK, B_GROUP_W))
    cos_l, sin_l = _rope_tables(n, True)
    cos_c, sin_c = _rope_tables(ctx.shape[0], False)
    w_v_t = w_in[:, 2 * QK_COLS:2 * QK_COLS + A_WIDTH].T.astype(BF16)
    q_l, k_l, vt_l, uv_l = _in_projection(x, g_mix, sm, cm, w_in_b, w_v_t, cos_l, sin_l)
    q_c, k_c, vt_c, uv_c = _in_projection(ctx, g_mix, smc, cmc, w_in_b, w_v_t, cos_c, sin_c)
    k_all = jnp.concatenate([k_c, k_l], axis=0)
    vt_all = jnp.concatenate([vt_c, vt_l], axis=1)
    a_l = _diff_attention(q_l, k_all, vt_all, lamv, g_subln, lam_init)
    x = _finish_even(a_l, uv_l, x, w_out_b, ws_b, bs_b, ln_g, ln_b, gm)
    if need_ctx_out:
        a_c = _diff_attention(q_c, k_c, vt_c, lamv, g_subln, lam_init)
        ctx = _finish_even(a_c, uv_c, ctx, w_out_b, ws_b, bs_b, ln_g, ln_b, gmc)
    return x, ctx


def kernel(x, c, ctx, c_ctx, w_mod, b_mod, g_norm_mix, g_norm_ffn, w_in, w_out, lam_q1, lam_k1, lam_q2, lam_k2,
           g_subln, sgu_ln_g, sgu_ln_b, w_spatial, b_spatial, w_fourier_out, w_router, w_gate, w_up, w_down, g_final):
    assert x.shape[0] == 1 and DEPTH == 2
    d = D_MODEL
    x2 = x[0]
    ctx2 = ctx[0]
    c8 = jnp.zeros((8, d), F32).at[0].set(c[0]).at[1].set(c_ctx)
    mod_all = _modulation(c8, w_mod, b_mod)
    row = lambda v: v.reshape(1, -1)

    def mods_of(i):
        lat = [mod_all[i, 0:1, k * d:(k + 1) * d] for k in range(6)]
        cx = [mod_all[i, 1:2, k * d:(k + 1) * d] for k in range(6)]
        return {"lat": lat, "ctx": cx}

    m0 = mods_of(0)
    lam_init0 = 0.8 - 0.6 * math.exp(-0.3 * 0)
    lamv = jnp.zeros((8, LANES), F32)
    for r, v in enumerate((lam_q1[0], lam_k1[0], lam_q2[0], lam_k2[0])):
        lamv = lamv.at[r, :A_HEAD_DIM].set(v)
    x2, ctx2 = _even_layer_mix(x2, ctx2, m0, row(g_norm_mix[0]), w_in[0], w_out[0], lamv, row(g_subln[0]),
                               row(sgu_ln_g[0]), row(sgu_ln_b[0]), w_spatial[0], b_spatial[0], lam_init0, True)
    moe_x = _expert_choice_ffn(x2, row(g_norm_ffn[0]), m0["lat"][3], m0["lat"][4],
                               w_router[0], w_gate, w_up, w_down, 0)
    moe_c = _expert_choice_ffn(ctx2, row(g_norm_ffn[0]), m0["ctx"][3], m0["ctx"][4],
                               w_router[0], w_gate, w_up, w_down, 0)
    ctx2 = _residual(ctx2, moe_c, m0["ctx"][5])

    m1 = mods_of(1)
    x2 = _fourier_mix_layer(x2, moe_x, m0["lat"][5], row(g_norm_mix[1]), m1["lat"][0], m1["lat"][1],
                            m1["lat"][2], w_fourier_out[0])
    moe_x = _expert_choice_ffn(x2, row(g_norm_ffn[1]), m1["lat"][3], m1["lat"][4],
                               w_router[1], w_gate, w_up, w_down, 1)
    out = _final_norm(x2, moe_x, m1["lat"][5], row(g_final))
    del ctx2
    return out[None]
```

```python
import functools
import math

import numpy as np
import jax
import jax.numpy as jnp
from jax import lax
from jax.experimental import pallas as pl
from jax.experimental.pallas import tpu as pltpu

F32 = jnp.float32
BF16 = jnp.bfloat16
I32 = jnp.int32

D_MODEL = 2048
DEPTH = 2
GRID_W = 64
EPS = 1e-6

A_HEADS = 8
A_HEAD_DIM = 64
A_V_DIM = 2 * A_HEAD_DIM
A_WIDTH = A_HEADS * A_V_DIM
QK_COLS = A_HEADS * 2 * A_HEAD_DIM
ROPE_THETA = 10000.0

B_GROUPS = 8
B_GROUP_W = 128
B_WIDTH = B_GROUPS * B_GROUP_W
CHUNK = 128
IN_COLS = 2 * QK_COLS + A_WIDTH + 2 * B_WIDTH

C_GROUPS = 4
C_GROUP_W = D_MODEL // C_GROUPS

N_EXPERTS = 16
CAPACITY_FACTOR = 2
F_EXPERT = D_MODEL // 2

LANES = 128
MXU_DIM = 256
VMEM_LIMIT_BYTES = 56 * 1024 * 1024

FFT_N2 = 32
ROUTE_TILE = 256
ROUTE_WIN = 256


def _params(sem):
    return pltpu.CompilerParams(dimension_semantics=sem, vmem_limit_bytes=VMEM_LIMIT_BYTES)


def _nt_dot(a, b):
    return lax.dot_general(a, b, (((1,), (1,)), ((), ())), preferred_element_type=F32)


def _norm_mod(x, g, shift, scale):
    ms = jnp.mean(x * x, axis=-1, keepdims=True)
    y = x * lax.rsqrt(ms + EPS) * g
    return y * (1.0 + scale) + shift


def _mod_kernel(c_ref, w_ref, b_ref, o_ref):
    c = c_ref[...]
    s = c * jax.nn.sigmoid(c)
    o_ref[...] = jnp.dot(s.astype(BF16), w_ref[...].astype(BF16), preferred_element_type=F32) + b_ref[...]


def _modulation(c8, w_mod, b_mod):
    depth, d, n6 = w_mod.shape
    tn = 1024
    return pl.pallas_call(
        _mod_kernel,
        grid=(depth, n6 // tn),
        in_specs=[
            pl.BlockSpec((8, d), lambda l, j: (0, 0)),
            pl.BlockSpec((None, d, tn), lambda l, j: (l, 0, j)),
            pl.BlockSpec((None, 1, tn), lambda l, j: (l, 0, j)),
        ],
        out_specs=pl.BlockSpec((None, 8, tn), lambda l, j: (l, 0, j)),
        out_shape=jax.ShapeDtypeStruct((depth, 8, n6), F32),
        compiler_params=_params(("arbitrary", "arbitrary")),
        name="modulation",
    )(c8, w_mod, b_mod.reshape(depth, 1, n6))


PROJ_TN = 512
PROJ_ROW_PARTS = 4
_Q_TILES = QK_COLS // PROJ_TN
_V_TILES = A_WIDTH // PROJ_TN
_UV_TILES = 2 * B_WIDTH // PROJ_TN


def _rope_chunk(z, cos, sin_signed, first_half):
    partner = jnp.where(first_half, pltpu.roll(z, LANES - 16, 1), pltpu.roll(z, 16, 1))
    return z * cos + partner * sin_signed


def _proj_kernel(x_ref, g_ref, sh_ref, sc_ref, w_ref, wvt_ref, cos_ref, sin_ref,
                 q_ref, k_ref, vt_ref, uv_ref, h_scr):
    j = pl.program_id(1)

    tm = h_scr.shape[0]

    @pl.when(j == 0)
    def _():
        for r in range(tm // CHUNK):
            rows = slice(r * CHUNK, (r + 1) * CHUNK)
            h_scr[rows, :] = _norm_mod(x_ref[rows, :], g_ref[...], sh_ref[...], sc_ref[...]).astype(BF16)

    parts = [slice(r * (tm // PROJ_ROW_PARTS), (r + 1) * (tm // PROJ_ROW_PARTS)) for r in range(PROJ_ROW_PARTS)]

    def project(rows):
        return jnp.dot(h_scr[rows, :], w_ref[...], preferred_element_type=F32)

    def roped(rows, scale):
        z = project(rows)
        cos = cos_ref[rows, :]
        sin = sin_ref[rows, :]
        lane = lax.broadcasted_iota(I32, cos.shape, 1)
        first_half = (lane % 32) < 16
        chunks = []
        for c in range(PROJ_TN // LANES):
            zc = z[:, c * LANES:(c + 1) * LANES]
            chunks.append(_rope_chunk(zc, cos, sin, first_half) * scale)
        return jnp.concatenate(chunks, axis=1)

    @pl.when(j < _Q_TILES)
    def _():
        for rows in parts:
            q_ref[rows, :] = roped(rows, A_HEAD_DIM ** -0.5 * math.log2(math.e)).astype(BF16)

    @pl.when((j >= _Q_TILES) & (j < 2 * _Q_TILES))
    def _():
        for rows in parts:
            k_ref[rows, :] = roped(rows, 1.0).astype(BF16)

    @pl.when((j >= 2 * _Q_TILES) & (j < 2 * _Q_TILES + _V_TILES))
    def _():
        vt_ref[...] = _nt_dot(wvt_ref[...], h_scr[...]).astype(BF16)

    @pl.when(j >= 2 * _Q_TILES + _V_TILES)
    def _():
        for rows in parts:
            uv_ref[rows, :] = jax.nn.gelu(project(rows))


def _in_projection(x, g, shift, scale, w_in_bf16, w_v_t_bf16, cos_t, sin_t):
    n, d = x.shape
    tm = min(1024, n)
    nq = _Q_TILES
    nj = IN_COLS // PROJ_TN
    row = lambda i, j: (0, 0)
    v_tile = lambda j: jnp.clip(j - 2 * nq, 0, _V_TILES - 1)
    return pl.pallas_call(
        _proj_kernel,
        grid=(n // tm, nj),
        in_specs=[
            pl.BlockSpec((tm, d), lambda i, j: (i, 0)),
            pl.BlockSpec((1, d), row), pl.BlockSpec((1, d), row), pl.BlockSpec((1, d), row),
            pl.BlockSpec((d, PROJ_TN), lambda i, j: (0, j)),
            pl.BlockSpec((PROJ_TN, d), lambda i, j: (v_tile(j), 0)),
            pl.BlockSpec((tm, LANES), lambda i, j: (i, 0)),
            pl.BlockSpec((tm, LANES), lambda i, j: (i, 0)),
        ],
        out_specs=[
            pl.BlockSpec((tm, PROJ_TN), lambda i, j: (i, jnp.clip(j, 0, nq - 1))),
            pl.BlockSpec((tm, PROJ_TN), lambda i, j: (i, jnp.clip(j - nq, 0, nq - 1))),
            pl.BlockSpec((PROJ_TN, tm), lambda i, j: (v_tile(j), i)),
            pl.BlockSpec((tm, PROJ_TN), lambda i, j: (i, jnp.clip(j - 2 * nq - _V_TILES, 0, _UV_TILES - 1))),
        ],
        out_shape=[
            jax.ShapeDtypeStruct((n, QK_COLS), BF16),
            jax.ShapeDtypeStruct((n, QK_COLS), BF16),
            jax.ShapeDtypeStruct((A_WIDTH, n), BF16),
            jax.ShapeDtypeStruct((n, 2 * B_WIDTH), F32),
        ],
        scratch_shapes=[pltpu.VMEM((tm, d), BF16)],
        compiler_params=_params(("arbitrary", "arbitrary")),
        name="in_projection",
    )(x, g, shift, scale, w_in_bf16, w_v_t_bf16, cos_t, sin_t)


def _rope_tables(n, rotate):
    if not rotate:
        return jnp.ones((n, LANES), F32), jnp.zeros((n, LANES), F32)
    t = np.arange(n)
    row = (t // GRID_W).astype(np.float32)
    col = (t % GRID_W).astype(np.float32)
    dim = A_HEAD_DIM // 2
    inv = (np.float32(ROPE_THETA) ** (-np.arange(0, dim, 2, dtype=np.float32) / np.float32(dim))).astype(np.float32)
    ang_r = row[:, None] * inv[None, :]
    ang_c = col[:, None] * inv[None, :]
    ang64 = np.concatenate([ang_r, ang_r, ang_c, ang_c], axis=1)
    sign64 = np.concatenate([-np.ones(16), np.ones(16), -np.ones(16), np.ones(16)]).astype(np.float32)
    ang = np.tile(ang64, (1, LANES // 64))
    sign = np.tile(sign64, LANES // 64)
    return jnp.asarray(np.cos(ang), F32), jnp.asarray(np.sin(ang) * sign[None, :], F32)


def _attn_kernel(q_ref, k_ref, vt_ref, lamv_ref, gs_ref, o_ref, m_scr, acc_scr, qst_scr, sa_scr, sb_scr,
                 *, tq, tk, lam_init):
    n_blocks = q_ref.shape[0] // tq
    n_kv = k_ref.shape[0] // tk

    def query_operand(i):
        qt = q_ref[pl.ds(pl.multiple_of(i * tq, tq), tq), :].astype(F32).T
        row = lax.broadcasted_iota(I32, qt.shape, 0)
        zero = jnp.zeros_like(qt)
        return jnp.concatenate([jnp.where(row < A_HEAD_DIM, qt, zero),
                                jnp.where(row >= A_HEAD_DIM, qt, zero)], axis=1).astype(BF16)

    def scores(qst, j):
        off = pl.multiple_of(j * tk, tk)
        return jnp.dot(k_ref[pl.ds(off, tk), :], qst, preferred_element_type=F32)

    ones_rows = jnp.ones((16, tk), BF16)

    def consume(s, j):
        off = pl.multiple_of(j * tk, tk)
        vtb = jnp.concatenate([vt_ref[:, pl.ds(off, tk)], ones_rows], axis=0)
        m_old = m_scr[...]
        m_new = jnp.maximum(m_old, jnp.max(s, axis=0, keepdims=True))
        alpha = jnp.exp2(m_old - m_new)
        p = jnp.exp2(s - m_new).astype(BF16)
        acc_scr[...] = alpha * acc_scr[...] + jnp.dot(vtb, p, preferred_element_type=F32)
        m_scr[...] = m_new

    lv = lamv_ref[...]
    lam = (jnp.exp(jnp.sum(lv[0:1] * lv[1:2], axis=-1, keepdims=True))
           - jnp.exp(jnp.sum(lv[2:3] * lv[3:4], axis=-1, keepdims=True)) + lam_init)

    cross_block = n_kv % 2 == 0
    qst_scr[...] = query_operand(0)
    if cross_block:
        sa_scr[...] = scores(qst_scr[...], 0)

    def block(i, carry):
        qst = qst_scr[...]
        m_scr[...] = jnp.full(m_scr.shape, -jnp.inf, F32)
        acc_scr[...] = jnp.zeros(acc_scr.shape, F32)
        if not cross_block:
            sa_scr[...] = scores(qst, 0)

        def pair(p, c):
            j = 2 * p
            sb_scr[...] = scores(qst, j + 1)
            consume(sa_scr[...], j)
            sa_scr[...] = scores(qst, j + 2)
            consume(sb_scr[...], j + 1)
            return c

        qst_next = query_operand(jnp.minimum(i + 1, n_blocks - 1))
        if cross_block:
            lax.fori_loop(0, n_kv // 2 - 1, pair, 0)
            sb_scr[...] = scores(qst, n_kv - 1)
            consume(sa_scr[...], n_kv - 2)
            sa_scr[...] = scores(qst_next, 0)
            consume(sb_scr[...], n_kv - 1)
        else:
            lax.fori_loop(0, (n_kv - 1) // 2, pair, 0)
            consume(sa_scr[...], n_kv - 1)
        qst_scr[...] = qst_next

        ot = acc_scr[:A_V_DIM, :] / acc_scr[A_V_DIM:A_V_DIM + 1, :]
        o = (ot[:, :tq] - lam * ot[:, tq:]).T
        a = o * lax.rsqrt(jnp.mean(o * o, axis=-1, keepdims=True) + EPS) * gs_ref[...]
        o_ref[pl.ds(pl.multiple_of(i * tq, tq), tq), :] = (a * (1.0 - lam_init)).astype(BF16)
        return carry

    lax.fori_loop(0, n_blocks, block, 0)


def _pick_tile(n, candidates):
    for c in candidates:
        if n % c == 0:
            return c
    raise ValueError(f"no tile for {n}")


def _diff_attention(q, k_all, vt_all, lamv, g_subln, lam_init):
    n = q.shape[0]
    nk = k_all.shape[0]
    tq = 256
    tk = _pick_tile(nk, (1408, 768, 512, 256))
    score_buf = pltpu.VMEM((tk, 2 * tq), F32)
    return pl.pallas_call(
        functools.partial(_attn_kernel, tq=tq, tk=tk, lam_init=lam_init),
        grid=(A_HEADS,),
        in_specs=[
            pl.BlockSpec((n, A_V_DIM), lambda h: (0, h)),
            pl.BlockSpec((nk, A_V_DIM), lambda h: (0, h)),
            pl.BlockSpec((A_V_DIM, nk), lambda h: (h, 0)),
            pl.BlockSpec((8, LANES), lambda h: (0, 0)),
            pl.BlockSpec((1, A_V_DIM), lambda h: (0, 0)),
        ],
        out_specs=pl.BlockSpec((n, A_V_DIM), lambda h: (0, h)),
        out_shape=jax.ShapeDtypeStruct((n, A_WIDTH), BF16),
        scratch_shapes=[pltpu.VMEM((1, 2 * tq), F32),
                        pltpu.VMEM((A_V_DIM + 16, 2 * tq), F32),
                        pltpu.VMEM((A_V_DIM, 2 * tq), BF16),
                        score_buf, score_buf],
        compiler_params=_params(("arbitrary",)),
        name="diff_attention",
    )(q, k_all, vt_all, lamv, g_subln)


def _finish_kernel(a_ref, uv_ref, x_ref, wout_ref, ws_ref, bs_ref, lng_ref, lnb_ref, gm_ref, o_ref, cat_scr):
    tm = a_ref.shape[0]
    cat_scr[:, :A_WIDTH] = a_ref[...]
    for c in range(tm // CHUNK):
        rows = slice(c * CHUNK, (c + 1) * CHUNK)
        for g in range(B_GROUPS):
            cols = slice(g * B_GROUP_W, (g + 1) * B_GROUP_W)
            u = uv_ref[rows, g * B_GROUP_W:(g + 1) * B_GROUP_W]
            v = uv_ref[rows, B_WIDTH + g * B_GROUP_W:B_WIDTH + (g + 1) * B_GROUP_W]
            mu = jnp.mean(v, axis=-1, keepdims=True)
            var = jnp.mean(jnp.square(v - mu), axis=-1, keepdims=True)
            vn = (v - mu) * lax.rsqrt(var + EPS) * lng_ref[:, cols] + lnb_ref[:, cols]
            mixed = jnp.dot(ws_ref[g], vn.astype(BF16), preferred_element_type=F32) + bs_ref[g]
            cat_scr[rows, A_WIDTH + g * B_GROUP_W:A_WIDTH + (g + 1) * B_GROUP_W] = (u * mixed).astype(BF16)
    y = jnp.dot(cat_scr[...], wout_ref[...], preferred_element_type=F32)
    o_ref[...] = x_ref[...] + gm_ref[...] * y


def _finish_even(a, uv, x, w_out_bf16, ws_bf16, bs_b, ln_g, ln_b, gm):
    n, d = x.shape
    tm = min(512, n)
    row = lambda i: (0, 0)
    return pl.pallas_call(
        _finish_kernel,
        grid=(n // tm,),
        in_specs=[
            pl.BlockSpec((tm, A_WIDTH), lambda i: (i, 0)),
            pl.BlockSpec((tm, 2 * B_WIDTH), lambda i: (i, 0)),
            pl.BlockSpec((tm, d), lambda i: (i, 0)),
            pl.BlockSpec((A_WIDTH + B_WIDTH, d), row),
            pl.BlockSpec((B_GROUPS, CHUNK, CHUNK), lambda i: (0, 0, 0)),
            pl.BlockSpec((B_GROUPS, CHUNK, B_GROUP_W), lambda i: (0, 0, 0)),
            pl.BlockSpec((1, B_WIDTH), row), pl.BlockSpec((1, B_WIDTH), row),
            pl.BlockSpec((1, d), row),
        ],
        out_specs=pl.BlockSpec((tm, d), lambda i: (i, 0)),
        out_shape=jax.ShapeDtypeStruct((n, d), F32),
        scratch_shapes=[pltpu.VMEM((tm, A_WIDTH + B_WIDTH), BF16)],
        compiler_params=_params(("arbitrary",)),
        name="finish_even",
    )(a, uv, x, w_out_bf16, ws_bf16, bs_b, ln_g, ln_b, gm)


def _router_kernel(x_ref, g_ref, sh_ref, sc_ref, wrt_ref, ht_ref, aff_ref):
    h = _norm_mod(x_ref[...], g_ref[...], sh_ref[...], sc_ref[...])
    logits = _nt_dot(wrt_ref[...], h.astype(BF16))
    m = jnp.max(logits, axis=0, keepdims=True)
    e = jnp.exp(logits - m)
    aff_ref[...] = e / jnp.sum(e, axis=0, keepdims=True)
    ht_ref[...] = h.T.astype(BF16)


def _router(x, g, shift, scale, w_router_t_bf16):
    n, d = x.shape
    tm = min(512, n)
    row = lambda i: (0, 0)
    return pl.pallas_call(
        _router_kernel,
        grid=(n // tm,),
        in_specs=[
            pl.BlockSpec((tm, d), lambda i: (i, 0)),
            pl.BlockSpec((1, d), row), pl.BlockSpec((1, d), row), pl.BlockSpec((1, d), row),
            pl.BlockSpec((N_EXPERTS, d), row),
        ],
        out_specs=[pl.BlockSpec((d, tm), lambda i: (0, i)), pl.BlockSpec((N_EXPERTS, tm), lambda i: (0, i))],
        out_shape=[jax.ShapeDtypeStruct((d, n), BF16), jax.ShapeDtypeStruct((N_EXPERTS, n), F32)],
        compiler_params=_params(("arbitrary",)),
        name="router",
    )(x, g, shift, scale, w_router_t_bf16)


def _select_kernel(aff_ref, sel_ref, pos_ref, g_ref, first_ref, *, cap, capp):
    aff = aff_ref[...]
    n = aff.shape[1]
    idx = lax.broadcasted_iota(I32, aff.shape, 1)
    capf = float(cap)

    def count(mask):
        return jnp.sum(jnp.where(mask, 1.0, 0.0), axis=1, keepdims=True)

    def as_float(bits):
        return pltpu.bitcast(bits, F32)

    def thr_body(i, thr):
        cand = thr | jnp.left_shift(jnp.int32(1), 30 - i)
        return jnp.where(count(aff >= as_float(cand)) >= capf, cand, thr)

    thr = lax.fori_loop(0, 31, thr_body, jnp.zeros((aff.shape[0], 1), I32))
    gt = aff >= as_float(thr + 1)
    eq = (aff >= as_float(thr)) & jnp.logical_not(gt)
    need = capf - count(gt)
    nbits = int(n).bit_length()

    def tie_body(i, lim):
        cand = lim | jnp.left_shift(jnp.int32(1), nbits - 1 - i)
        return jnp.where(count(eq & (idx < cand)) <= need, cand, lim)

    lim = lax.fori_loop(0, nbits, tie_body, jnp.zeros((aff.shape[0], 1), I32))
    sel = jnp.where(gt | (eq & (idx < lim)), 1.0, 0.0)
    sel_ref[...] = sel
    g_ref[...] = aff * sel

    ri = lax.broadcasted_iota(I32, (LANES, LANES), 0)
    ci = lax.broadcasted_iota(I32, (LANES, LANES), 1)
    upper = jnp.where(ri < ci, 1.0, 0.0).astype(BF16)
    carry = jnp.zeros((aff.shape[0], 1), F32)
    for c in range(n // LANES):
        m = sel[:, c * LANES:(c + 1) * LANES]
        within = jnp.dot(m.astype(BF16), upper, preferred_element_type=F32)
        pos_ref[:, c * LANES:(c + 1) * LANES] = (within + carry).astype(I32)
        carry = carry + jnp.sum(m, axis=1, keepdims=True)

    incl = pos_ref[...].astype(F32) + sel
    lane = lax.broadcasted_iota(I32, first_ref.shape, 1)
    first = jnp.zeros(first_ref.shape, F32)
    for w in range(capp // ROUTE_WIN + 1):
        first = jnp.where(lane == w, count(incl <= float(w * ROUTE_WIN)), first)
    first_ref[...] = first.astype(I32)


def _select(aff_t, cap, capp):
    e, n = aff_t.shape
    full = lambda: (0, 0)
    return pl.pallas_call(
        functools.partial(_select_kernel, cap=cap, capp=capp),
        grid=(),
        in_specs=[pl.BlockSpec((e, n), full)],
        out_specs=[pl.BlockSpec((e, n), full)] * 3 + [pl.BlockSpec((e, LANES), full)],
        out_shape=[jax.ShapeDtypeStruct((e, n), F32), jax.ShapeDtypeStruct((e, n), I32),
                   jax.ShapeDtypeStruct((e, n), F32), jax.ShapeDtypeStruct((e, LANES), I32)],
        compiler_params=pltpu.CompilerParams(vmem_limit_bytes=VMEM_LIMIT_BYTES),
        name="expert_select",
    )(aff_t)


def _one_hot_window(pos_row, sel_row, lo, base, win=ROUTE_WIN):
    rel = jnp.where((sel_row > 0.0) & (pos_row >= lo), pos_row - base, -1)
    r = lax.broadcasted_iota(I32, (win, pos_row.shape[1]), 0)
    return jnp.where(r == rel, 1.0, 0.0).astype(BF16)


def _window_plan(p0, p1, win=ROUTE_WIN):
    start = (p0 // LANES) * LANES
    n_chunks = jnp.where(p1 > p0, (p1 - start + win - 1) // win, 0)
    return start, n_chunks


def _dispatch_kernel(first_ref, ht_hbm, sel_ref, pos_ref, g_ref, xs_ref, gslot_ref, ht_scr, acc_scr, gs_scr, sem,
                     *, k_chunk):
    e = pl.program_id(0)
    w = pl.program_id(1)
    n = ht_scr.shape[1]

    @pl.when((e == 0) & (w == 0))
    def _():
        load = pltpu.make_async_copy(ht_hbm, ht_scr, sem)
        load.start()
        load.wait()

    t0 = first_ref[e, w]
    t1 = first_ref[e, w + 1]
    start = (t0 // ROUTE_TILE) * ROUTE_TILE
    n_chunks = jnp.where(t1 > t0, (t1 - start + k_chunk - 1) // k_chunk, 0)
    slot0 = w * ROUTE_WIN

    def chunk(k):
        lo_tok = start + k * k_chunk
        base = pl.multiple_of(jnp.minimum(lo_tok, n - k_chunk), ROUTE_TILE)
        toks = pl.ds(base, k_chunk)
        tok = base + lax.broadcasted_iota(I32, (1, k_chunk), 1)
        pos_row = pos_ref[0, :, toks]
        keep = (sel_ref[0, :, toks] > 0.0) & (tok >= lo_tok)
        rel = jnp.where(keep, pos_row - slot0, -1)
        r = lax.broadcasted_iota(I32, (ROUTE_WIN, k_chunk), 0)
        onehot = jnp.where(r == rel, 1.0, 0.0).astype(BF16)
        g_row = g_ref[0, :, toks]
        g_hi = g_row.astype(BF16)
        r1 = g_row - g_hi.astype(F32)
        g_mid = r1.astype(BF16)
        g_lo = (r1 - g_mid.astype(F32)).astype(BF16)
        g8 = jnp.concatenate([g_hi, g_mid, g_lo, jnp.zeros((5, k_chunk), BF16)], axis=0)
        return _nt_dot(ht_scr[:, toks], onehot), _nt_dot(g8, onehot)

    acc_scr[...], gs_scr[...] = chunk(0)

    def extra(k, carry):
        x, gs = chunk(k)
        acc_scr[...] += x
        gs_scr[...] += gs
        return carry

    lax.fori_loop(1, n_chunks, extra, 0)
    xs_ref[0] = acc_scr[...].T.astype(BF16)
    gs = gs_scr[...]
    gslot_ref[0] = gs[0:1] + gs[1:2] + gs[2:3]


def _dispatch(first_tok, h_t, sel3, pos3, g3, capp):
    d, n = h_t.shape
    k_chunk = min(10 * ROUTE_TILE, n)
    row3 = pl.BlockSpec((1, 1, n), lambda e, w, ft: (e, 0, 0))
    return pl.pallas_call(
        functools.partial(_dispatch_kernel, k_chunk=k_chunk),
        grid_spec=pltpu.PrefetchScalarGridSpec(
            num_scalar_prefetch=1,
            grid=(N_EXPERTS, capp // ROUTE_WIN),
            in_specs=[pl.BlockSpec(memory_space=pl.ANY), row3, row3, row3],
            out_specs=[
                pl.BlockSpec((1, ROUTE_WIN, d), lambda e, w, ft: (e, w, 0)),
                pl.BlockSpec((1, 1, ROUTE_WIN), lambda e, w, ft: (e, 0, w)),
            ],
            scratch_shapes=[pltpu.VMEM((d, n), BF16), pltpu.VMEM((d, ROUTE_WIN), F32),
                            pltpu.VMEM((8, ROUTE_WIN), F32), pltpu.SemaphoreType.DMA],
        ),
        out_shape=[jax.ShapeDtypeStruct((N_EXPERTS, capp, d), BF16),
                   jax.ShapeDtypeStruct((N_EXPERTS, 1, capp), F32)],
        compiler_params=_params(("arbitrary", "arbitrary")),
        name="moe_dispatch",
    )(first_tok, h_t, sel3, pos3, g3)


def _ffn_up_kernel(xs_ref, wg_ref, wu_ref, h_ref):
    xs = xs_ref[0].astype(BF16)
    a = jnp.dot(xs, wg_ref[0].astype(BF16), preferred_element_type=F32)
    u = jnp.dot(xs, wu_ref[0].astype(BF16), preferred_element_type=F32)
    h_ref[0] = (a * jax.nn.sigmoid(a) * u).astype(BF16)


def _ffn_up(xs, w_gate, w_up, layer):
    e, capp, d = xs.shape
    f = w_gate.shape[3]
    tf = 512
    return pl.pallas_call(
        _ffn_up_kernel,
        grid=(e, f // tf),
        in_specs=[
            pl.BlockSpec((1, capp, d), lambda i, j: (i, 0, 0)),
            pl.BlockSpec((None, 1, d, tf), lambda i, j: (layer, i, 0, j)),
            pl.BlockSpec((None, 1, d, tf), lambda i, j: (layer, i, 0, j)),
        ],
        out_specs=pl.BlockSpec((1, capp, tf), lambda i, j: (i, 0, j)),
        out_shape=jax.ShapeDtypeStruct((e, capp, f), BF16),
        compiler_params=_params(("arbitrary", "arbitrary")),
        name="moe_ffn_up",
    )(xs, w_gate, w_up)


def _ffn_down_kernel(h_ref, wd_ref, gslot_ref, y_ref):
    y = jnp.dot(h_ref[0], wd_ref[0].astype(BF16), preferred_element_type=F32)
    y_ref[0] = (y.T * gslot_ref[0]).astype(BF16)


def _ffn_down(h, w_down, gslot, layer):
    e, capp, f = h.shape
    d = w_down.shape[3]
    td = 512
    return pl.pallas_call(
        _ffn_down_kernel,
        grid=(e, d // td),
        in_specs=[
            pl.BlockSpec((1, capp, f), lambda i, j: (i, 0, 0)),
            pl.BlockSpec((None, 1, f, td), lambda i, j: (layer, i, 0, j)),
            pl.BlockSpec((1, 1, capp), lambda i, j: (i, 0, 0)),
        ],
        out_specs=pl.BlockSpec((1, td, capp), lambda i, j: (i, j, 0)),
        out_shape=jax.ShapeDtypeStruct((e, d, capp), BF16),
        compiler_params=_params(("arbitrary", "arbitrary")),
        name="moe_ffn_down",
    )(h, w_down, gslot)


COMBINE_TD = 512
COMBINE_WIN = ROUTE_WIN


def _combine_kernel(ps_ref, pe_ref, y_ref, sel_ref, pos_ref, x_ref, gate_ref, o_ref, acc_scr):
    t = pl.program_id(1)
    n_exp, _, capp = y_ref.shape

    def window(e, k):
        start, _ = _window_plan(ps_ref[e, t], pe_ref[e, t], COMBINE_WIN)
        lo = start + k * COMBINE_WIN
        base = pl.multiple_of(jnp.minimum(lo, capp - COMBINE_WIN), LANES)
        onehot = _one_hot_window(pos_ref[e, pl.ds(t, 1), :], sel_ref[e, pl.ds(t, 1), :], lo, base, COMBINE_WIN)
        return [y_ref[e, :, pl.ds(base, COMBINE_WIN)]], [onehot]

    lhs, rhs = [], []
    for e in range(n_exp):
        rows, hots = window(e, 0)
        lhs += rows
        rhs += hots
    acc_scr[...] = jnp.dot(jnp.concatenate(lhs, axis=1), jnp.concatenate(rhs, axis=0), preferred_element_type=F32)

    for e in range(n_exp):
        n_chunks = _window_plan(ps_ref[e, t], pe_ref[e, t], COMBINE_WIN)[1]

        def extra(k, carry, e=e):
            rows, hots = window(e, k)
            acc_scr[...] += jnp.dot(jnp.concatenate(rows, axis=1), jnp.concatenate(hots, axis=0),
                                    preferred_element_type=F32)
            return carry

        lax.fori_loop(1, n_chunks, extra, 0)

    o_ref[...] = x_ref[...] + gate_ref[...] * acc_scr[...].T


def _combine(tile_start, tile_end, y_t, sel_t, pos_t, x, gate):
    e, d, capp = y_t.shape
    n = x.shape[0]
    n_tiles = n // ROUTE_TILE
    td = COMBINE_TD
    y_blk = pl.BlockSpec((e, td, capp), lambda j, t, ps, pe: (0, j, 0))
    full = pl.BlockSpec((e, n_tiles, ROUTE_TILE), lambda j, t, ps, pe: (0, 0, 0))
    tile = pl.BlockSpec((ROUTE_TILE, td), lambda j, t, ps, pe: (t, j))
    return pl.pallas_call(
        _combine_kernel,
        grid_spec=pltpu.PrefetchScalarGridSpec(
            num_scalar_prefetch=2,
            grid=(d // td, n_tiles),
            in_specs=[y_blk, full, full, tile, pl.BlockSpec((1, td), lambda j, t, ps, pe: (0, j))],
            out_specs=tile,
            scratch_shapes=[pltpu.VMEM((td, ROUTE_TILE), F32)],
        ),
        out_shape=jax.ShapeDtypeStruct((n, d), F32),
        compiler_params=_params(("arbitrary", "arbitrary")),
        name="moe_combine",
    )(tile_start, tile_end, y_t, sel_t, pos_t, x, gate)


def _tile_bounds(pos, tile, cap):
    start = pos[:, ::tile]
    end = jnp.concatenate([start[:, 1:], jnp.full((pos.shape[0], 1), cap, I32)], axis=1)
    return start, end


def _expert_choice_ffn(x, g, shift, scale, gate, w_router, w_gate, w_up, w_down, layer):
    n, d = x.shape
    cap = CAPACITY_FACTOR * n // N_EXPERTS
    capp = max(cap, ROUTE_WIN)
    h_t, aff_t = _router(x, g, shift, scale, w_router.T.astype(BF16))
    sel, pos, gsel, first_tok = _select(aff_t, cap, capp)
    as3 = lambda a: a.reshape(N_EXPERTS, 1, n)
    xs, gslot = _dispatch(first_tok, h_t, as3(sel), as3(pos), as3(gsel), capp)
    hidden = _ffn_up(xs, w_gate, w_up, layer)
    y_t = _ffn_down(hidden, w_down, gslot, layer)
    as_tiles = lambda a: a.reshape(N_EXPERTS, n // ROUTE_TILE, ROUTE_TILE)
    return _combine(*_tile_bounds(pos, ROUTE_TILE, cap), y_t, as_tiles(sel), as_tiles(pos), x, gate)


def _final_kernel(x_ref, g_ref, o_ref):
    x = x_ref[...]
    o_ref[...] = x * lax.rsqrt(jnp.mean(x * x, axis=-1, keepdims=True) + EPS) * g_ref[...]


def _final_norm(x, g_final):
    n, d = x.shape
    tm = min(512, n)
    return pl.pallas_call(
        _final_kernel,
        grid=(n // tm,),
        in_specs=[pl.BlockSpec((tm, d), lambda i: (i, 0)), pl.BlockSpec((1, d), lambda i: (0, 0))],
        out_specs=pl.BlockSpec((tm, d), lambda i: (i, 0)),
        out_shape=jax.ShapeDtypeStruct((n, d), F32),
        compiler_params=_params(("arbitrary",)),
        name="final_norm",
    )(x, g_final)


FFT_K1_BLK = 16


def _fourier_in_kernel(x_ref, g_ref, sh_ref, sc_ref, cs_ref, perm_ref, a_ref, b_ref):
    h = _norm_mod(x_ref[...], g_ref[...], sh_ref[...], sc_ref[...]).astype(BF16)
    hp = jnp.dot(perm_ref[...], h, preferred_element_type=F32).astype(BF16)
    n2, t1_blk = a_ref.shape[0], a_ref.shape[1]
    for g in range(C_GROUPS):
        cols = slice(g * C_GROUP_W, (g + 1) * C_GROUP_W)
        ab = jnp.dot(hp[:, cols], cs_ref[...], preferred_element_type=F32)
        a_ref[:, :, cols] = ab[:, :C_GROUP_W].astype(BF16).reshape(n2, t1_blk, C_GROUP_W)
        b_ref[:, :, cols] = ab[:, C_GROUP_W:].astype(BF16).reshape(n2, t1_blk, C_GROUP_W)


def _fourier_in(x, g, shift, scale, cs):
    n, d = x.shape
    tm = 512
    n1 = n // FFT_N2
    t1_blk = tm // FFT_N2
    row = lambda i: (0, 0)
    tile = pl.BlockSpec((tm, d), lambda i: (i, 0))
    ab_blk = pl.BlockSpec((FFT_N2, t1_blk, d), lambda i: (0, i, 0))
    ab_shape = jax.ShapeDtypeStruct((FFT_N2, n1, d), BF16)
    perm = np.zeros((tm, tm), np.float32)
    src = np.arange(tm)
    perm[(src % FFT_N2) * t1_blk + src // FFT_N2, src] = 1.0
    return pl.pallas_call(
        _fourier_in_kernel,
        grid=(n // tm,),
        in_specs=[tile, pl.BlockSpec((1, d), row), pl.BlockSpec((1, d), row), pl.BlockSpec((1, d), row),
                  pl.BlockSpec((C_GROUP_W, 2 * C_GROUP_W), row), pl.BlockSpec((tm, tm), row)],
        out_specs=[ab_blk, ab_blk],
        out_shape=[ab_shape, ab_shape],
        compiler_params=_params(("arbitrary",)),
        name="fourier_channel_dft",
    )(x, g, shift, scale, cs, jnp.asarray(perm, BF16))


def _fourier_stage1_kernel(a_ref, b_ref, ma_ref, mb_ref, ct_ref, st_ref, zr_ref, zi_ref):
    n1 = a_ref.shape[1]
    z = (jnp.dot(ma_ref[...], a_ref[0], preferred_element_type=F32)
         + jnp.dot(mb_ref[...], b_ref[0], preferred_element_type=F32))
    ct = ct_ref[0]
    st = st_ref[0]
    for c in range(a_ref.shape[2] // LANES):
        cols = slice(c * LANES, (c + 1) * LANES)
        zr = z[:n1, cols]
        zi = z[n1:, cols]
        zr_ref[0, :, cols] = (zr * ct + zi * st).astype(BF16)
        zi_ref[0, :, cols] = (zi * ct - zr * st).astype(BF16)


def _fourier_stage1(a3, b3, ma, mb, ct, st):
    n2, n1, d = a3.shape
    blk = pl.BlockSpec((1, n1, d), lambda j: (j, 0, 0))
    mat = pl.BlockSpec((2 * n1, n1), lambda j: (0, 0))
    tw = pl.BlockSpec((1, n1, LANES), lambda j: (j, 0, 0))
    shp = jax.ShapeDtypeStruct((n2, n1, d), BF16)
    return pl.pallas_call(
        _fourier_stage1_kernel,
        grid=(n2,),
        in_specs=[blk, blk, mat, mat, tw, tw],
        out_specs=[blk, blk],
        out_shape=[shp, shp],
        compiler_params=_params(("arbitrary",)),
        name="fourier_stage1",
    )(a3, b3, ma, mb, ct, st)


def _fourier_out_kernel(zr_ref, zi_ref, bc_ref, bs_ref, wo_ref, x_ref, gm_ref, o_ref):
    rows = zr_ref.shape[0] * zr_ref.shape[1]
    d = zr_ref.shape[2]
    zr = zr_ref[...].reshape(rows, d)
    zi = zi_ref[...].reshape(rows, d)
    f = (jnp.dot(bc_ref[...], zr, preferred_element_type=F32)
         + jnp.dot(bs_ref[...], zi, preferred_element_type=F32))
    y = jnp.dot(f.astype(BF16), wo_ref[...], preferred_element_type=F32)
    o_ref[...] = x_ref[...] + gm_ref[...] * y.reshape(o_ref.shape)


def _fourier_out(zr3, zi3, bd_c, bd_s, w_o_bf16, x, gm):
    n2, n1, d = zr3.shape
    n = n1 * n2
    rows = n2 * FFT_K1_BLK
    x3 = x.reshape(n2, n1, d)
    blk = pl.BlockSpec((n2, FFT_K1_BLK, d), lambda i: (0, i, 0))
    const = lambda i: (0, 0)
    out = pl.pallas_call(
        _fourier_out_kernel,
        grid=(n1 // FFT_K1_BLK,),
        in_specs=[blk, blk, pl.BlockSpec((rows, rows), const), pl.BlockSpec((rows, rows), const),
                  pl.BlockSpec((d, d), const), blk, pl.BlockSpec((1, d), const)],
        out_specs=blk,
        out_shape=jax.ShapeDtypeStruct((n2, n1, d), F32),
        compiler_params=_params(("arbitrary",)),
        name="fourier_stage2_out",
    )(zr3, zi3, bd_c, bd_s, w_o_bf16, x3, gm)
    return out.reshape(n, d)


def _fourier_constants(n):
    n1, n2 = n // FFT_N2, FFT_N2
    two_pi = 2.0 * np.pi

    def angles(a, b, period):
        return two_pi * ((np.outer(a, b) % period).astype(np.float64) / period)

    kc = np.arange(C_GROUP_W)
    ang = angles(kc, kc, C_GROUP_W)
    cs = np.concatenate([np.cos(ang), np.sin(ang)], axis=1) / np.sqrt(C_GROUP_W)
    k1 = np.arange(n1)
    ang1 = angles(k1, k1, n1)
    c1, s1 = np.cos(ang1) / np.sqrt(n), np.sin(ang1) / np.sqrt(n)
    ma = np.concatenate([c1, -s1], axis=0)
    mb = np.concatenate([-s1, -c1], axis=0)
    t2 = np.arange(n2)
    angt = angles(t2, k1, n)
    ct = np.repeat(np.cos(angt)[:, :, None], LANES, axis=2)
    st = np.repeat(np.sin(angt)[:, :, None], LANES, axis=2)
    ang2 = angles(t2, t2, n2)
    k1_blk = FFT_K1_BLK
    bd_c = np.zeros((n2 * k1_blk, n2 * k1_blk))
    bd_s = np.zeros((n2 * k1_blk, n2 * k1_blk))
    for kl in range(k1_blk):
        bd_c[kl::k1_blk, kl::k1_blk] = np.cos(ang2)
        bd_s[kl::k1_blk, kl::k1_blk] = np.sin(ang2)
    bf = lambda a: jnp.asarray(a, F32).astype(BF16)
    return bf(cs), bf(ma), bf(mb), jnp.asarray(ct, F32), jnp.asarray(st, F32), bf(bd_c), bf(bd_s)


def _fourier_mix_layer(x, g, shift, scale, gm, w_o):
    cs, ma, mb, ct, st, bd_c, bd_s = _fourier_constants(x.shape[0])
    a3, b3 = _fourier_in(x, g, shift, scale, cs)
    zr3, zi3 = _fourier_stage1(a3, b3, ma, mb, ct, st)
    return _fourier_out(zr3, zi3, bd_c, bd_s, w_o.astype(BF16), x, gm)


def _even_layer_mix(x, ctx, mods, g_mix, w_in, w_out, lamv, g_subln, ln_g, ln_b, w_s, b_s, lam_init, need_ctx_out):
    sm, cm, gm = mods["lat"][0:3]
    smc, cmc, gmc = mods["ctx"][0:3]
    n = x.shape[0]
    w_in_b = w_in.astype(BF16)
    w_out_b = w_out.astype(BF16)
    ws_b = w_s.astype(BF16)
    bs_b = jnp.broadcast_to(b_s[:, :, None], (B_GROUPS, CHUNK, B_GROUP_W))
    cos_l, sin_l = _rope_tables(n, True)
    cos_c, sin_c = _rope_tables(ctx.shape[0], False)
    w_v_t = w_in[:, 2 * QK_COLS:2 * QK_COLS + A_WIDTH].T.astype(BF16)
    q_l, k_l, vt_l, uv_l = _in_projection(x, g_mix, sm, cm, w_in_b, w_v_t, cos_l, sin_l)
    q_c, k_c, vt_c, uv_c = _in_projection(ctx, g_mix, smc, cmc, w_in_b, w_v_t, cos_c, sin_c)
    k_all = jnp.concatenate([k_c, k_l], axis=0)
    vt_all = jnp.concatenate([vt_c, vt_l], axis=1)
    a_l = _diff_attention(q_l, k_all, vt_all, lamv, g_subln, lam_init)
    x = _finish_even(a_l, uv_l, x, w_out_b, ws_b, bs_b, ln_g, ln_b, gm)
    if need_ctx_out:
        a_c = _diff_attention(q_c, k_c, vt_c, lamv, g_subln, lam_init)
        ctx = _finish_even(a_c, uv_c, ctx, w_out_b, ws_b, bs_b, ln_g, ln_b, gmc)
    return x, ctx


def kernel(x, c, ctx, c_ctx, w_mod, b_mod, g_norm_mix, g_norm_ffn, w_in, w_out, lam_q1, lam_k1, lam_q2, lam_k2,
           g_subln, sgu_ln_g, sgu_ln_b, w_spatial, b_spatial, w_fourier_out, w_router, w_gate, w_up, w_down, g_final):
    assert x.shape[0] == 1 and DEPTH == 2
    d = D_MODEL
    x2 = x[0]
    ctx2 = ctx[0]
    c8 = jnp.zeros((8, d), F32).at[0].set(c[0]).at[1].set(c_ctx)
    mod_all = _modulation(c8, w_mod, b_mod)
    row = lambda v: v.reshape(1, -1)

    def mods_of(i):
        lat = [mod_all[i, 0:1, k * d:(k + 1) * d] for k in range(6)]
        cx = [mod_all[i, 1:2, k * d:(k + 1) * d] for k in range(6)]
        return {"lat": lat, "ctx": cx}

    m0 = mods_of(0)
    lam_init0 = 0.8 - 0.6 * math.exp(-0.3 * 0)
    lamv = jnp.zeros((8, LANES), F32)
    for r, v in enumerate((lam_q1[0], lam_k1[0], lam_q2[0], lam_k2[0])):
        lamv = lamv.at[r, :A_HEAD_DIM].set(v)
    x2, ctx2 = _even_layer_mix(x2, ctx2, m0, row(g_norm_mix[0]), w_in[0], w_out[0], lamv, row(g_subln[0]),
                               row(sgu_ln_g[0]), row(sgu_ln_b[0]), w_spatial[0], b_spatial[0], lam_init0, True)
    x2 = _expert_choice_ffn(x2, row(g_norm_ffn[0]), m0["lat"][3], m0["lat"][4], m0["lat"][5],
                            w_router[0], w_gate, w_up, w_down, 0)
    ctx2 = _expert_choice_ffn(ctx2, row(g_norm_ffn[0]), m0["ctx"][3], m0["ctx"][4], m0["ctx"][5],
                              w_router[0], w_gate, w_up, w_down, 0)

    m1 = mods_of(1)
    x2 = _fourier_mix_layer(x2, row(g_norm_mix[1]), m1["lat"][0], m1["lat"][1], m1["lat"][2], w_fourier_out[0])
    x2 = _expert_choice_ffn(x2, row(g_norm_ffn[1]), m1["lat"][3], m1["lat"][4], m1["lat"][5],
                            w_router[1], w_gate, w_up, w_down, 1)
    out = _final_norm(x2, row(g_final))
    del ctx2
    return out[None]
```

```python
import functools
import math

import numpy as np
import jax
import jax.numpy as jnp
from jax import lax
from jax.experimental import pallas as pl
from jax.experimental.pallas import tpu as pltpu

F32 = jnp.float32
BF16 = jnp.bfloat16
I32 = jnp.int32

D_MODEL = 2048
DEPTH = 2
GRID_W = 64
EPS = 1e-6

A_HEADS = 8
A_HEAD_DIM = 64
A_V_DIM = 2 * A_HEAD_DIM
A_WIDTH = A_HEADS * A_V_DIM
QK_COLS = A_HEADS * 2 * A_HEAD_DIM
ROPE_THETA = 10000.0

B_GROUPS = 8
B_GROUP_W = 128
B_WIDTH = B_GROUPS * B_GROUP_W
CHUNK = 128
IN_COLS = 2 * QK_COLS + A_WIDTH + 2 * B_WIDTH

C_GROUPS = 4
C_GROUP_W = D_MODEL // C_GROUPS

N_EXPERTS = 16
CAPACITY_FACTOR = 2
F_EXPERT = D_MODEL // 2

LANES = 128
MXU_DIM = 256
VMEM_LIMIT_BYTES = 56 * 1024 * 1024

FFT_N2 = 32
ROUTE_TILE = 256
ROUTE_WIN = 256


def _params(sem):
    return pltpu.CompilerParams(dimension_semantics=sem, vmem_limit_bytes=VMEM_LIMIT_BYTES)


def _nt_dot(a, b):
    return lax.dot_general(a, b, (((1,), (1,)), ((), ())), preferred_element_type=F32)


def _norm_mod(x, g, shift, scale):
    ms = jnp.mean(x * x, axis=-1, keepdims=True)
    y = x * lax.rsqrt(ms + EPS) * g
    return y * (1.0 + scale) + shift


def _mod_kernel(c_ref, w_ref, b_ref, o_ref):
    c = c_ref[...]
    s = c * jax.nn.sigmoid(c)
    o_ref[...] = jnp.dot(s.astype(BF16), w_ref[...].astype(BF16), preferred_element_type=F32) + b_ref[...]


def _modulation(c8, w_mod, b_mod):
    depth, d, n6 = w_mod.shape
    tn = 1024
    return pl.pallas_call(
        _mod_kernel,
        grid=(depth, n6 // tn),
        in_specs=[
            pl.BlockSpec((8, d), lambda l, j: (0, 0)),
            pl.BlockSpec((None, d, tn), lambda l, j: (l, 0, j)),
            pl.BlockSpec((None, 1, tn), lambda l, j: (l, 0, j)),
        ],
        out_specs=pl.BlockSpec((None, 8, tn), lambda l, j: (l, 0, j)),
        out_shape=jax.ShapeDtypeStruct((depth, 8, n6), F32),
        compiler_params=_params(("arbitrary", "arbitrary")),
        name="modulation",
    )(c8, w_mod, b_mod.reshape(depth, 1, n6))


PROJ_TN = 512
PROJ_ROW_PARTS = 4
_Q_TILES = QK_COLS // PROJ_TN
_V_TILES = A_WIDTH // PROJ_TN
_UV_TILES = 2 * B_WIDTH // PROJ_TN


def _rope_chunk(z, cos, sin_signed, first_half):
    partner = jnp.where(first_half, pltpu.roll(z, LANES - 16, 1), pltpu.roll(z, 16, 1))
    return z * cos + partner * sin_signed


def _proj_kernel(x_ref, g_ref, sh_ref, sc_ref, w_ref, wvt_ref, cos_ref, sin_ref,
                 q_ref, k_ref, vt_ref, uv_ref, h_scr):
    j = pl.program_id(1)

    tm = h_scr.shape[0]

    @pl.when(j == 0)
    def _():
        for r in range(tm // CHUNK):
            rows = slice(r * CHUNK, (r + 1) * CHUNK)
            h_scr[rows, :] = _norm_mod(x_ref[rows, :], g_ref[...], sh_ref[...], sc_ref[...]).astype(BF16)

    parts = [slice(r * (tm // PROJ_ROW_PARTS), (r + 1) * (tm // PROJ_ROW_PARTS)) for r in range(PROJ_ROW_PARTS)]

    def project(rows):
        return jnp.dot(h_scr[rows, :], w_ref[...], preferred_element_type=F32)

    def roped(rows, scale):
        z = project(rows)
        cos = cos_ref[rows, :]
        sin = sin_ref[rows, :]
        lane = lax.broadcasted_iota(I32, cos.shape, 1)
        first_half = (lane % 32) < 16
        chunks = []
        for c in range(PROJ_TN // LANES):
            zc = z[:, c * LANES:(c + 1) * LANES]
            chunks.append(_rope_chunk(zc, cos, sin, first_half) * scale)
        return jnp.concatenate(chunks, axis=1)

    @pl.when(j < _Q_TILES)
    def _():
        for rows in parts:
            q_ref[rows, :] = roped(rows, A_HEAD_DIM ** -0.5 * math.log2(math.e)).astype(BF16)

    @pl.when((j >= _Q_TILES) & (j < 2 * _Q_TILES))
    def _():
        for rows in parts:
            k_ref[rows, :] = roped(rows, 1.0).astype(BF16)

    @pl.when((j >= 2 * _Q_TILES) & (j < 2 * _Q_TILES + _V_TILES))
    def _():
        vt_ref[...] = _nt_dot(wvt_ref[...], h_scr[...]).astype(BF16)

    @pl.when(j >= 2 * _Q_TILES + _V_TILES)
    def _():
        for rows in parts:
            uv_ref[rows, :] = jax.nn.gelu(project(rows))


def _in_projection(x, g, shift, scale, w_in_bf16, w_v_t_bf16, cos_t, sin_t):
    n, d = x.shape
    tm = min(1024, n)
    nq = _Q_TILES
    nj = IN_COLS // PROJ_TN
    row = lambda i, j: (0, 0)
    v_tile = lambda j: jnp.clip(j - 2 * nq, 0, _V_TILES - 1)
    return pl.pallas_call(
        _proj_kernel,
        grid=(n // tm, nj),
        in_specs=[
            pl.BlockSpec((tm, d), lambda i, j: (i, 0)),
            pl.BlockSpec((1, d), row), pl.BlockSpec((1, d), row), pl.BlockSpec((1, d), row),
            pl.BlockSpec((d, PROJ_TN), lambda i, j: (0, j)),
            pl.BlockSpec((PROJ_TN, d), lambda i, j: (v_tile(j), 0)),
            pl.BlockSpec((tm, LANES), lambda i, j: (i, 0)),
            pl.BlockSpec((tm, LANES), lambda i, j: (i, 0)),
        ],
        out_specs=[
            pl.BlockSpec((tm, PROJ_TN), lambda i, j: (i, jnp.clip(j, 0, nq - 1))),
            pl.BlockSpec((tm, PROJ_TN), lambda i, j: (i, jnp.clip(j - nq, 0, nq - 1))),
            pl.BlockSpec((PROJ_TN, tm), lambda i, j: (v_tile(j), i)),
            pl.BlockSpec((tm, PROJ_TN), lambda i, j: (i, jnp.clip(j - 2 * nq - _V_TILES, 0, _UV_TILES - 1))),
        ],
        out_shape=[
            jax.ShapeDtypeStruct((n, QK_COLS), BF16),
            jax.ShapeDtypeStruct((n, QK_COLS), BF16),
            jax.ShapeDtypeStruct((A_WIDTH, n), BF16),
            jax.ShapeDtypeStruct((n, 2 * B_WIDTH), F32),
        ],
        scratch_shapes=[pltpu.VMEM((tm, d), BF16)],
        compiler_params=_params(("arbitrary", "arbitrary")),
        name="in_projection",
    )(x, g, shift, scale, w_in_bf16, w_v_t_bf16, cos_t, sin_t)


def _rope_tables(n, rotate):
    if not rotate:
        return jnp.ones((n, LANES), F32), jnp.zeros((n, LANES), F32)
    t = np.arange(n)
    row = (t // GRID_W).astype(np.float32)
    col = (t % GRID_W).astype(np.float32)
    dim = A_HEAD_DIM // 2
    inv = (np.float32(ROPE_THETA) ** (-np.arange(0, dim, 2, dtype=np.float32) / np.float32(dim))).astype(np.float32)
    ang_r = row[:, None] * inv[None, :]
    ang_c = col[:, None] * inv[None, :]
    ang64 = np.concatenate([ang_r, ang_r, ang_c, ang_c], axis=1)
    sign64 = np.concatenate([-np.ones(16), np.ones(16), -np.ones(16), np.ones(16)]).astype(np.float32)
    ang = np.tile(ang64, (1, LANES // 64))
    sign = np.tile(sign64, LANES // 64)
    return jnp.asarray(np.cos(ang), F32), jnp.asarray(np.sin(ang) * sign[None, :], F32)


def _attn_kernel(*refs, n_parts, tq, tk, lam_init):
    q_ref = refs[0]
    k_parts = refs[1:1 + n_parts]
    vt_parts = refs[1 + n_parts:1 + 2 * n_parts]
    lamv_ref, gs_ref, o_ref, m_scr, acc_scr, qst_scr, sa_scr, sb_scr, k_ref, vt_ref = refs[1 + 2 * n_parts:]
    n_blocks = q_ref.shape[0] // tq
    n_kv = k_ref.shape[0] // tk

    off = 0
    for k_part, vt_part in zip(k_parts, vt_parts):
        rows = k_part.shape[0]
        k_ref[off:off + rows, :] = k_part[...]
        vt_ref[:, off:off + rows] = vt_part[...]
        off += rows

    def query_operand(i):
        qt = q_ref[pl.ds(pl.multiple_of(i * tq, tq), tq), :].astype(F32).T
        row = lax.broadcasted_iota(I32, qt.shape, 0)
        zero = jnp.zeros_like(qt)
        return jnp.concatenate([jnp.where(row < A_HEAD_DIM, qt, zero),
                                jnp.where(row >= A_HEAD_DIM, qt, zero)], axis=1).astype(BF16)

    def scores(qst, j):
        off = pl.multiple_of(j * tk, tk)
        return jnp.dot(k_ref[pl.ds(off, tk), :], qst, preferred_element_type=F32)

    ones_rows = jnp.ones((16, tk), BF16)

    def consume(s, j):
        off = pl.multiple_of(j * tk, tk)
        vtb = jnp.concatenate([vt_ref[:, pl.ds(off, tk)], ones_rows], axis=0)
        m_old = m_scr[...]
        m_new = jnp.maximum(m_old, jnp.max(s, axis=0, keepdims=True))
        alpha = jnp.exp2(m_old - m_new)
        p = jnp.exp2(s - m_new).astype(BF16)
        acc_scr[...] = alpha * acc_scr[...] + jnp.dot(vtb, p, preferred_element_type=F32)
        m_scr[...] = m_new

    lv = lamv_ref[...]
    lam = (jnp.exp(jnp.sum(lv[0:1] * lv[1:2], axis=-1, keepdims=True))
           - jnp.exp(jnp.sum(lv[2:3] * lv[3:4], axis=-1, keepdims=True)) + lam_init)

    cross_block = n_kv % 2 == 0
    qst_scr[...] = query_operand(0)
    if cross_block:
        sa_scr[...] = scores(qst_scr[...], 0)

    def block(i, carry):
        qst = qst_scr[...]
        m_scr[...] = jnp.full(m_scr.shape, -jnp.inf, F32)
        acc_scr[...] = jnp.zeros(acc_scr.shape, F32)
        if not cross_block:
            sa_scr[...] = scores(qst, 0)

        def pair(p, c):
            j = 2 * p
            sb_scr[...] = scores(qst, j + 1)
            consume(sa_scr[...], j)
            sa_scr[...] = scores(qst, j + 2)
            consume(sb_scr[...], j + 1)
            return c

        qst_next = query_operand(jnp.minimum(i + 1, n_blocks - 1))
        if cross_block:
            lax.fori_loop(0, n_kv // 2 - 1, pair, 0)
            sb_scr[...] = scores(qst, n_kv - 1)
            consume(sa_scr[...], n_kv - 2)
            sa_scr[...] = scores(qst_next, 0)
            consume(sb_scr[...], n_kv - 1)
        else:
            lax.fori_loop(0, (n_kv - 1) // 2, pair, 0)
            consume(sa_scr[...], n_kv - 1)
        qst_scr[...] = qst_next

        ot = acc_scr[:A_V_DIM, :] / acc_scr[A_V_DIM:A_V_DIM + 1, :]
        o = (ot[:, :tq] - lam * ot[:, tq:]).T
        a = o * lax.rsqrt(jnp.mean(o * o, axis=-1, keepdims=True) + EPS) * gs_ref[...]
        o_ref[pl.ds(pl.multiple_of(i * tq, tq), tq), :] = (a * (1.0 - lam_init)).astype(BF16)
        return carry

    lax.fori_loop(0, n_blocks, block, 0)


def _pick_tile(n, candidates):
    for c in candidates:
        if n % c == 0:
            return c
    raise ValueError(f"no tile for {n}")


def _diff_attention(q, k_parts, vt_parts, lamv, g_subln, lam_init):
    n = q.shape[0]
    nk = sum(k.shape[0] for k in k_parts)
    tq = min(512, n)
    tk = _pick_tile(nk, (1408, 768, 512, 256))
    score_buf = pltpu.VMEM((tk, 2 * tq), F32)
    return pl.pallas_call(
        functools.partial(_attn_kernel, n_parts=len(k_parts), tq=tq, tk=tk, lam_init=lam_init),
        grid=(A_HEADS,),
        in_specs=[pl.BlockSpec((n, A_V_DIM), lambda h: (0, h))]
        + [pl.BlockSpec((k.shape[0], A_V_DIM), lambda h: (0, h)) for k in k_parts]
        + [pl.BlockSpec((A_V_DIM, vt.shape[1]), lambda h: (h, 0)) for vt in vt_parts]
        + [pl.BlockSpec((8, LANES), lambda h: (0, 0)), pl.BlockSpec((1, A_V_DIM), lambda h: (0, 0))],
        out_specs=pl.BlockSpec((n, A_V_DIM), lambda h: (0, h)),
        out_shape=jax.ShapeDtypeStruct((n, A_WIDTH), BF16),
        scratch_shapes=[pltpu.VMEM((1, 2 * tq), F32),
                        pltpu.VMEM((A_V_DIM + 16, 2 * tq), F32),
                        pltpu.VMEM((A_V_DIM, 2 * tq), BF16),
                        score_buf, score_buf,
                        pltpu.VMEM((nk, A_V_DIM), BF16), pltpu.VMEM((A_V_DIM, nk), BF16)],
        compiler_params=_params(("arbitrary",)),
        name="diff_attention",
    )(q, *k_parts, *vt_parts, lamv, g_subln)


def _finish_kernel(a_ref, uv_ref, x_ref, wout_ref, ws_ref, bs_ref, lng_ref, lnb_ref, gm_ref, o_ref, cat_scr):
    tm = a_ref.shape[0]
    cat_scr[:, :A_WIDTH] = a_ref[...]
    for c in range(tm // CHUNK):
        rows = slice(c * CHUNK, (c + 1) * CHUNK)
        for g in range(B_GROUPS):
            cols = slice(g * B_GROUP_W, (g + 1) * B_GROUP_W)
            u = uv_ref[rows, g * B_GROUP_W:(g + 1) * B_GROUP_W]
            v = uv_ref[rows, B_WIDTH + g * B_GROUP_W:B_WIDTH + (g + 1) * B_GROUP_W]
            mu = jnp.mean(v, axis=-1, keepdims=True)
            var = jnp.mean(jnp.square(v - mu), axis=-1, keepdims=True)
            vn = (v - mu) * lax.rsqrt(var + EPS) * lng_ref[:, cols] + lnb_ref[:, cols]
            mixed = jnp.dot(ws_ref[g], vn.astype(BF16), preferred_element_type=F32) + bs_ref[g]
            cat_scr[rows, A_WIDTH + g * B_GROUP_W:A_WIDTH + (g + 1) * B_GROUP_W] = (u * mixed).astype(BF16)
    y = jnp.dot(cat_scr[...], wout_ref[...], preferred_element_type=F32)
    o_ref[...] = x_ref[...] + gm_ref[...] * y


def _finish_even(a, uv, x, w_out_bf16, ws_bf16, bs_b, ln_g, ln_b, gm):
    n, d = x.shape
    tm = min(512, n)
    row = lambda i: (0, 0)
    return pl.pallas_call(
        _finish_kernel,
        grid=(n // tm,),
        in_specs=[
            pl.BlockSpec((tm, A_WIDTH), lambda i: (i, 0)),
            pl.BlockSpec((tm, 2 * B_WIDTH), lambda i: (i, 0)),
            pl.BlockSpec((tm, d), lambda i: (i, 0)),
            pl.BlockSpec((A_WIDTH + B_WIDTH, d), row),
            pl.BlockSpec((B_GROUPS, CHUNK, CHUNK), lambda i: (0, 0, 0)),
            pl.BlockSpec((B_GROUPS, CHUNK, B_GROUP_W), lambda i: (0, 0, 0)),
            pl.BlockSpec((1, B_WIDTH), row), pl.BlockSpec((1, B_WIDTH), row),
            pl.BlockSpec((1, d), row),
        ],
        out_specs=pl.BlockSpec((tm, d), lambda i: (i, 0)),
        out_shape=jax.ShapeDtypeStruct((n, d), F32),
        scratch_shapes=[pltpu.VMEM((tm, A_WIDTH + B_WIDTH), BF16)],
        compiler_params=_params(("arbitrary",)),
        name="finish_even",
    )(a, uv, x, w_out_bf16, ws_bf16, bs_b, ln_g, ln_b, gm)


def _router_kernel(x_ref, g_ref, sh_ref, sc_ref, wrt_ref, ht_ref, aff_ref):
    h = _norm_mod(x_ref[...], g_ref[...], sh_ref[...], sc_ref[...])
    logits = _nt_dot(wrt_ref[...], h.astype(BF16))
    m = jnp.max(logits, axis=0, keepdims=True)
    e = jnp.exp(logits - m)
    aff_ref[...] = e / jnp.sum(e, axis=0, keepdims=True)
    ht_ref[...] = h.T.astype(BF16)


def _router(x, g, shift, scale, w_router_t_bf16):
    n, d = x.shape
    tm = min(512, n)
    row = lambda i: (0, 0)
    return pl.pallas_call(
        _router_kernel,
        grid=(n // tm,),
        in_specs=[
            pl.BlockSpec((tm, d), lambda i: (i, 0)),
            pl.BlockSpec((1, d), row), pl.BlockSpec((1, d), row), pl.BlockSpec((1, d), row),
            pl.BlockSpec((N_EXPERTS, d), row),
        ],
        out_specs=[pl.BlockSpec((d, tm), lambda i: (0, i)), pl.BlockSpec((N_EXPERTS, tm), lambda i: (0, i))],
        out_shape=[jax.ShapeDtypeStruct((d, n), BF16), jax.ShapeDtypeStruct((N_EXPERTS, n), F32)],
        compiler_params=_params(("arbitrary",)),
        name="router",
    )(x, g, shift, scale, w_router_t_bf16)


def _select_kernel(aff_ref, sel_ref, pos_ref, g_ref, first_ref, *, cap, capp):
    aff = aff_ref[...]
    n = aff.shape[1]
    idx = lax.broadcasted_iota(I32, aff.shape, 1)
    capf = float(cap)

    def count(mask):
        return jnp.sum(jnp.where(mask, 1.0, 0.0), axis=1, keepdims=True)

    def as_float(bits):
        return pltpu.bitcast(bits, F32)

    def thr_body(i, thr):
        cand = thr | jnp.left_shift(jnp.int32(1), 30 - i)
        return jnp.where(count(aff >= as_float(cand)) >= capf, cand, thr)

    thr = lax.fori_loop(0, 31, thr_body, jnp.zeros((aff.shape[0], 1), I32))
    gt = aff >= as_float(thr + 1)
    eq = (aff >= as_float(thr)) & jnp.logical_not(gt)
    need = capf - count(gt)
    nbits = int(n).bit_length()

    def tie_body(i, lim):
        cand = lim | jnp.left_shift(jnp.int32(1), nbits - 1 - i)
        return jnp.where(count(eq & (idx < cand)) <= need, cand, lim)

    lim = lax.fori_loop(0, nbits, tie_body, jnp.zeros((aff.shape[0], 1), I32))
    sel = jnp.where(gt | (eq & (idx < lim)), 1.0, 0.0)
    sel_ref[...] = sel
    g_ref[...] = aff * sel

    ri = lax.broadcasted_iota(I32, (LANES, LANES), 0)
    ci = lax.broadcasted_iota(I32, (LANES, LANES), 1)
    upper = jnp.where(ri < ci, 1.0, 0.0).astype(BF16)
    carry = jnp.zeros((aff.shape[0], 1), F32)
    for c in range(n // LANES):
        m = sel[:, c * LANES:(c + 1) * LANES]
        within = jnp.dot(m.astype(BF16), upper, preferred_element_type=F32)
        pos_ref[:, c * LANES:(c + 1) * LANES] = (within + carry).astype(I32)
        carry = carry + jnp.sum(m, axis=1, keepdims=True)

    incl = pos_ref[...].astype(F32) + sel
    lane = lax.broadcasted_iota(I32, first_ref.shape, 1)
    first = jnp.zeros(first_ref.shape, F32)
    for w in range(capp // ROUTE_WIN + 1):
        first = jnp.where(lane == w, count(incl <= float(w * ROUTE_WIN)), first)
    first_ref[...] = first.astype(I32)


def _select(aff_t, cap, capp):
    e, n = aff_t.shape
    full = lambda: (0, 0)
    return pl.pallas_call(
        functools.partial(_select_kernel, cap=cap, capp=capp),
        grid=(),
        in_specs=[pl.BlockSpec((e, n), full)],
        out_specs=[pl.BlockSpec((e, n), full)] * 3 + [pl.BlockSpec((e, LANES), full)],
        out_shape=[jax.ShapeDtypeStruct((e, n), F32), jax.ShapeDtypeStruct((e, n), I32),
                   jax.ShapeDtypeStruct((e, n), F32), jax.ShapeDtypeStruct((e, LANES), I32)],
        compiler_params=pltpu.CompilerParams(vmem_limit_bytes=VMEM_LIMIT_BYTES),
        name="expert_select",
    )(aff_t)


def _one_hot_rows(rel, win):
    r = lax.broadcasted_iota(I32, (win, rel.shape[1]), 0)
    return jnp.where(r == rel, 1.0, 0.0).astype(BF16)


def _one_hot_window(pos_row, sel_row, lo, base, win=ROUTE_WIN):
    rel = jnp.where((sel_row > 0.0) & (pos_row >= lo), pos_row - base, -1)
    return _one_hot_rows(rel, win)


def _window_plan(p0, p1, win=ROUTE_WIN):
    start = (p0 // LANES) * LANES
    n_chunks = jnp.where(p1 > p0, (p1 - start + win - 1) // win, 0)
    return start, n_chunks


def _dispatch_kernel(first_ref, ht_hbm, sel_ref, pos_ref, g_ref, xs_ref, gslot_ref, ht_scr, acc_scr, gs_scr, sem,
                     *, k_chunk):
    e = pl.program_id(0)
    w = pl.program_id(1)
    n = ht_scr.shape[1]

    @pl.when((e == 0) & (w == 0))
    def _():
        load = pltpu.make_async_copy(ht_hbm, ht_scr, sem)
        load.start()
        load.wait()

    t0 = first_ref[e, w]
    t1 = first_ref[e, w + 1]
    start = (t0 // ROUTE_TILE) * ROUTE_TILE
    n_chunks = jnp.where(t1 > t0, (t1 - start + k_chunk - 1) // k_chunk, 0)
    slot0 = w * ROUTE_WIN

    def chunk(k):
        lo_tok = start + k * k_chunk
        base = pl.multiple_of(jnp.minimum(lo_tok, n - k_chunk), ROUTE_TILE)
        toks = pl.ds(base, k_chunk)
        tok = base + lax.broadcasted_iota(I32, (1, k_chunk), 1)
        pos_row = pos_ref[0, :, toks]
        keep = (sel_ref[0, :, toks] > 0.0) & (tok >= lo_tok)
        rel = jnp.where(keep, pos_row - slot0, -1)
        onehot = _one_hot_rows(rel, ROUTE_WIN)
        g_row = g_ref[0, :, toks]
        g_hi = g_row.astype(BF16)
        r1 = g_row - g_hi.astype(F32)
        g_mid = r1.astype(BF16)
        g_lo = (r1 - g_mid.astype(F32)).astype(BF16)
        g8 = jnp.concatenate([g_hi, g_mid, g_lo, jnp.zeros((5, k_chunk), BF16)], axis=0)
        return _nt_dot(ht_scr[:, toks], onehot), _nt_dot(g8, onehot)

    acc_scr[...], gs_scr[...] = chunk(0)

    def extra(k, carry):
        x, gs = chunk(k)
        acc_scr[...] += x
        gs_scr[...] += gs
        return carry

    lax.fori_loop(1, n_chunks, extra, 0)
    xs_ref[0] = acc_scr[...].T.astype(BF16)
    gs = gs_scr[...]
    gslot_ref[0] = gs[0:1] + gs[1:2] + gs[2:3]


def _dispatch(first_tok, h_t, sel3, pos3, g3, capp):
    d, n = h_t.shape
    k_chunk = min(10 * ROUTE_TILE, n)
    row3 = pl.BlockSpec((1, 1, n), lambda e, w, ft: (e, 0, 0))
    return pl.pallas_call(
        functools.partial(_dispatch_kernel, k_chunk=k_chunk),
        grid_spec=pltpu.PrefetchScalarGridSpec(
            num_scalar_prefetch=1,
            grid=(N_EXPERTS, capp // ROUTE_WIN),
            in_specs=[pl.BlockSpec(memory_space=pl.ANY), row3, row3, row3],
            out_specs=[
                pl.BlockSpec((1, ROUTE_WIN, d), lambda e, w, ft: (e, w, 0)),
                pl.BlockSpec((1, 1, ROUTE_WIN), lambda e, w, ft: (e, 0, w)),
            ],
            scratch_shapes=[pltpu.VMEM((d, n), BF16), pltpu.VMEM((d, ROUTE_WIN), F32),
                            pltpu.VMEM((8, ROUTE_WIN), F32), pltpu.SemaphoreType.DMA],
        ),
        out_shape=[jax.ShapeDtypeStruct((N_EXPERTS, capp, d), BF16),
                   jax.ShapeDtypeStruct((N_EXPERTS, 1, capp), F32)],
        compiler_params=_params(("arbitrary", "arbitrary")),
        name="moe_dispatch",
    )(first_tok, h_t, sel3, pos3, g3)


def _ffn_up_kernel(xs_ref, wg_ref, wu_ref, h_ref):
    xs = xs_ref[0].astype(BF16)
    a = jnp.dot(xs, wg_ref[0].astype(BF16), preferred_element_type=F32)
    u = jnp.dot(xs, wu_ref[0].astype(BF16), preferred_element_type=F32)
    h_ref[0] = (a * jax.nn.sigmoid(a) * u).astype(BF16)


def _ffn_up(xs, w_gate, w_up, layer):
    e, capp, d = xs.shape
    f = w_gate.shape[3]
    tf = 512
    return pl.pallas_call(
        _ffn_up_kernel,
        grid=(e, f // tf),
        in_specs=[
            pl.BlockSpec((1, capp, d), lambda i, j: (i, 0, 0)),
            pl.BlockSpec((None, 1, d, tf), lambda i, j: (layer, i, 0, j)),
            pl.BlockSpec((None, 1, d, tf), lambda i, j: (layer, i, 0, j)),
        ],
        out_specs=pl.BlockSpec((1, capp, tf), lambda i, j: (i, 0, j)),
        out_shape=jax.ShapeDtypeStruct((e, capp, f), BF16),
        compiler_params=_params(("arbitrary", "arbitrary")),
        name="moe_ffn_up",
    )(xs, w_gate, w_up)


def _ffn_down_kernel(h_ref, wd_ref, gslot_ref, y_ref):
    y = jnp.dot(h_ref[0], wd_ref[0].astype(BF16), preferred_element_type=F32)
    y_ref[0] = (y.T * gslot_ref[0]).astype(BF16)


def _ffn_down(h, w_down, gslot, layer):
    e, capp, f = h.shape
    d = w_down.shape[3]
    td = 1024
    return pl.pallas_call(
        _ffn_down_kernel,
        grid=(e, d // td),
        in_specs=[
            pl.BlockSpec((1, capp, f), lambda i, j: (i, 0, 0)),
            pl.BlockSpec((None, 1, f, td), lambda i, j: (layer, i, 0, j)),
            pl.BlockSpec((1, 1, capp), lambda i, j: (i, 0, 0)),
        ],
        out_specs=pl.BlockSpec((1, td, capp), lambda i, j: (i, j, 0)),
        out_shape=jax.ShapeDtypeStruct((e, d, capp), BF16),
        compiler_params=_params(("arbitrary", "arbitrary")),
        name="moe_ffn_down",
    )(h, w_down, gslot)


COMBINE_TD = 512
COMBINE_WIN = ROUTE_WIN


def _combine_kernel(ps_ref, pe_ref, y_ref, sel_ref, pos_ref, x_ref, gate_ref, o_ref, acc_scr):
    t = pl.program_id(1)
    n_exp, _, capp = y_ref.shape

    def window(e, k):
        start, _ = _window_plan(ps_ref[e, t], pe_ref[e, t], COMBINE_WIN)
        lo = start + k * COMBINE_WIN
        base = pl.multiple_of(jnp.minimum(lo, capp - COMBINE_WIN), LANES)
        onehot = _one_hot_window(pos_ref[e, pl.ds(t, 1), :], sel_ref[e, pl.ds(t, 1), :], lo, base, COMBINE_WIN)
        return [y_ref[e, :, pl.ds(base, COMBINE_WIN)]], [onehot]

    lhs, rhs = [], []
    for e in range(n_exp):
        rows, hots = window(e, 0)
        lhs += rows
        rhs += hots
    acc_scr[...] = jnp.dot(jnp.concatenate(lhs, axis=1), jnp.concatenate(rhs, axis=0), preferred_element_type=F32)

    for e in range(n_exp):
        n_chunks = _window_plan(ps_ref[e, t], pe_ref[e, t], COMBINE_WIN)[1]

        def extra(k, carry, e=e):
            rows, hots = window(e, k)
            acc_scr[...] += jnp.dot(jnp.concatenate(rows, axis=1), jnp.concatenate(hots, axis=0),
                                    preferred_element_type=F32)
            return carry

        lax.fori_loop(1, n_chunks, extra, 0)

    o_ref[...] = x_ref[...] + gate_ref[...] * acc_scr[...].T


def _combine(tile_start, tile_end, y_t, sel_t, pos_t, x, gate):
    e, d, capp = y_t.shape
    n = x.shape[0]
    n_tiles = n // ROUTE_TILE
    td = COMBINE_TD
    y_blk = pl.BlockSpec((e, td, capp), lambda j, t, ps, pe: (0, j, 0))
    full = pl.BlockSpec((e, n_tiles, ROUTE_TILE), lambda j, t, ps, pe: (0, 0, 0))
    tile = pl.BlockSpec((ROUTE_TILE, td), lambda j, t, ps, pe: (t, j))
    return pl.pallas_call(
        _combine_kernel,
        grid_spec=pltpu.PrefetchScalarGridSpec(
            num_scalar_prefetch=2,
            grid=(d // td, n_tiles),
            in_specs=[y_blk, full, full, tile, pl.BlockSpec((1, td), lambda j, t, ps, pe: (0, j))],
            out_specs=tile,
            scratch_shapes=[pltpu.VMEM((td, ROUTE_TILE), F32)],
        ),
        out_shape=jax.ShapeDtypeStruct((n, d), F32),
        compiler_params=_params(("arbitrary", "arbitrary")),
        name="moe_combine",
    )(tile_start, tile_end, y_t, sel_t, pos_t, x, gate)


def _tile_bounds(pos, tile, cap):
    start = pos[:, ::tile]
    end = jnp.concatenate([start[:, 1:], jnp.full((pos.shape[0], 1), cap, I32)], axis=1)
    return start, end


def _expert_choice_ffn(x, g, shift, scale, gate, w_router, w_gate, w_up, w_down, layer):
    n, d = x.shape
    cap = CAPACITY_FACTOR * n // N_EXPERTS
    capp = max(cap, ROUTE_WIN)
    h_t, aff_t = _router(x, g, shift, scale, w_router.T.astype(BF16))
    sel, pos, gsel, first_tok = _select(aff_t, cap, capp)
    as3 = lambda a: a.reshape(N_EXPERTS, 1, n)
    xs, gslot = _dispatch(first_tok, h_t, as3(sel), as3(pos), as3(gsel), capp)
    hidden = _ffn_up(xs, w_gate, w_up, layer)
    y_t = _ffn_down(hidden, w_down, gslot, layer)
    as_tiles = lambda a: a.reshape(N_EXPERTS, n // ROUTE_TILE, ROUTE_TILE)
    return _combine(*_tile_bounds(pos, ROUTE_TILE, cap), y_t, as_tiles(sel), as_tiles(pos), x, gate)


def _final_kernel(x_ref, g_ref, o_ref):
    x = x_ref[...]
    o_ref[...] = x * lax.rsqrt(jnp.mean(x * x, axis=-1, keepdims=True) + EPS) * g_ref[...]


def _final_norm(x, g_final):
    n, d = x.shape
    tm = min(512, n)
    return pl.pallas_call(
        _final_kernel,
        grid=(n // tm,),
        in_specs=[pl.BlockSpec((tm, d), lambda i: (i, 0)), pl.BlockSpec((1, d), lambda i: (0, 0))],
        out_specs=pl.BlockSpec((tm, d), lambda i: (i, 0)),
        out_shape=jax.ShapeDtypeStruct((n, d), F32),
        compiler_params=_params(("arbitrary",)),
        name="final_norm",
    )(x, g_final)


FFT_K1_BLK = 16


def _fourier_in_kernel(x_ref, g_ref, sh_ref, sc_ref, cs_ref, perm_ref, a_ref, b_ref):
    h = _norm_mod(x_ref[...], g_ref[...], sh_ref[...], sc_ref[...]).astype(BF16)
    hp = jnp.dot(perm_ref[...], h, preferred_element_type=F32).astype(BF16)
    n2, t1_blk = a_ref.shape[0], a_ref.shape[1]
    for g in range(C_GROUPS):
        cols = slice(g * C_GROUP_W, (g + 1) * C_GROUP_W)
        ab = jnp.dot(hp[:, cols], cs_ref[...], preferred_element_type=F32)
        a_ref[:, :, cols] = ab[:, :C_GROUP_W].astype(BF16).reshape(n2, t1_blk, C_GROUP_W)
        b_ref[:, :, cols] = ab[:, C_GROUP_W:].astype(BF16).reshape(n2, t1_blk, C_GROUP_W)


def _fourier_in(x, g, shift, scale, cs):
    n, d = x.shape
    tm = 512
    n1 = n // FFT_N2
    t1_blk = tm // FFT_N2
    row = lambda i: (0, 0)
    tile = pl.BlockSpec((tm, d), lambda i: (i, 0))
    ab_blk = pl.BlockSpec((FFT_N2, t1_blk, d), lambda i: (0, i, 0))
    ab_shape = jax.ShapeDtypeStruct((FFT_N2, n1, d), BF16)
    perm = np.zeros((tm, tm), np.float32)
    src = np.arange(tm)
    perm[(src % FFT_N2) * t1_blk + src // FFT_N2, src] = 1.0
    return pl.pallas_call(
        _fourier_in_kernel,
        grid=(n // tm,),
        in_specs=[tile, pl.BlockSpec((1, d), row), pl.BlockSpec((1, d), row), pl.BlockSpec((1, d), row),
                  pl.BlockSpec((C_GROUP_W, 2 * C_GROUP_W), row), pl.BlockSpec((tm, tm), row)],
        out_specs=[ab_blk, ab_blk],
        out_shape=[ab_shape, ab_shape],
        compiler_params=_params(("arbitrary",)),
        name="fourier_channel_dft",
    )(x, g, shift, scale, cs, jnp.asarray(perm, BF16))


def _fourier_stage1_kernel(a_ref, b_ref, ma_ref, mb_ref, ct_ref, st_ref, zr_ref, zi_ref):
    n1 = a_ref.shape[1]
    z = (jnp.dot(ma_ref[...], a_ref[0], preferred_element_type=F32)
         + jnp.dot(mb_ref[...], b_ref[0], preferred_element_type=F32))
    ct = ct_ref[0]
    st = st_ref[0]
    for c in range(a_ref.shape[2] // LANES):
        cols = slice(c * LANES, (c + 1) * LANES)
        zr = z[:n1, cols]
        zi = z[n1:, cols]
        zr_ref[0, :, cols] = (zr * ct + zi * st).astype(BF16)
        zi_ref[0, :, cols] = (zi * ct - zr * st).astype(BF16)


def _fourier_stage1(a3, b3, ma, mb, ct, st):
    n2, n1, d = a3.shape
    blk = pl.BlockSpec((1, n1, d), lambda j: (j, 0, 0))
    mat = pl.BlockSpec((2 * n1, n1), lambda j: (0, 0))
    tw = pl.BlockSpec((1, n1, LANES), lambda j: (j, 0, 0))
    shp = jax.ShapeDtypeStruct((n2, n1, d), BF16)
    return pl.pallas_call(
        _fourier_stage1_kernel,
        grid=(n2,),
        in_specs=[blk, blk, mat, mat, tw, tw],
        out_specs=[blk, blk],
        out_shape=[shp, shp],
        compiler_params=_params(("arbitrary",)),
        name="fourier_stage1",
    )(a3, b3, ma, mb, ct, st)


def _fourier_out_kernel(zr_ref, zi_ref, bc_ref, bs_ref, wo_ref, x_ref, gm_ref, o_ref):
    rows = zr_ref.shape[0] * zr_ref.shape[1]
    d = zr_ref.shape[2]
    zr = zr_ref[...].reshape(rows, d)
    zi = zi_ref[...].reshape(rows, d)
    f = (jnp.dot(bc_ref[...], zr, preferred_element_type=F32)
         + jnp.dot(bs_ref[...], zi, preferred_element_type=F32))
    y = jnp.dot(f.astype(BF16), wo_ref[...], preferred_element_type=F32)
    o_ref[...] = x_ref[...] + gm_ref[...] * y.reshape(o_ref.shape)


def _fourier_out(zr3, zi3, bd_c, bd_s, w_o_bf16, x, gm):
    n2, n1, d = zr3.shape
    n = n1 * n2
    rows = n2 * FFT_K1_BLK
    x3 = x.reshape(n2, n1, d)
    blk = pl.BlockSpec((n2, FFT_K1_BLK, d), lambda i: (0, i, 0))
    const = lambda i: (0, 0)
    out = pl.pallas_call(
        _fourier_out_kernel,
        grid=(n1 // FFT_K1_BLK,),
        in_specs=[blk, blk, pl.BlockSpec((rows, rows), const), pl.BlockSpec((rows, rows), const),
                  pl.BlockSpec((d, d), const), blk, pl.BlockSpec((1, d), const)],
        out_specs=blk,
        out_shape=jax.ShapeDtypeStruct((n2, n1, d), F32),
        compiler_params=_params(("arbitrary",)),
        name="fourier_stage2_out",
    )(zr3, zi3, bd_c, bd_s, w_o_bf16, x3, gm)
    return out.reshape(n, d)


def _fourier_constants(n):
    n1, n2 = n // FFT_N2, FFT_N2
    two_pi = 2.0 * np.pi

    def angles(a, b, period):
        return two_pi * ((np.outer(a, b) % period).astype(np.float64) / period)

    kc = np.arange(C_GROUP_W)
    ang = angles(kc, kc, C_GROUP_W)
    cs = np.concatenate([np.cos(ang), np.sin(ang)], axis=1) / np.sqrt(C_GROUP_W)
    k1 = np.arange(n1)
    ang1 = angles(k1, k1, n1)
    c1, s1 = np.cos(ang1) / np.sqrt(n), np.sin(ang1) / np.sqrt(n)
    ma = np.concatenate([c1, -s1], axis=0)
    mb = np.concatenate([-s1, -c1], axis=0)
    t2 = np.arange(n2)
    angt = angles(t2, k1, n)
    ct = np.repeat(np.cos(angt)[:, :, None], LANES, axis=2)
    st = np.repeat(np.sin(angt)[:, :, None], LANES, axis=2)
    ang2 = angles(t2, t2, n2)
    k1_blk = FFT_K1_BLK
    bd_c = np.zeros((n2 * k1_blk, n2 * k1_blk))
    bd_s = np.zeros((n2 * k1_blk, n2 * k1_blk))
    for kl in range(k1_blk):
        bd_c[kl::k1_blk, kl::k1_blk] = np.cos(ang2)
        bd_s[kl::k1_blk, kl::k1_blk] = np.sin(ang2)
    bf = lambda a: jnp.asarray(a, F32).astype(BF16)
    return bf(cs), bf(ma), bf(mb), jnp.asarray(ct, F32), jnp.asarray(st, F32), bf(bd_c), bf(bd_s)


def _fourier_mix_layer(x, g, shift, scale, gm, w_o):
    cs, ma, mb, ct, st, bd_c, bd_s = _fourier_constants(x.shape[0])
    a3, b3 = _fourier_in(x, g, shift, scale, cs)
    zr3, zi3 = _fourier_stage1(a3, b3, ma, mb, ct, st)
    return _fourier_out(zr3, zi3, bd_c, bd_s, w_o.astype(BF16), x, gm)


def _even_layer_mix(x, ctx, mods, g_mix, w_in, w_out, lamv, g_subln, ln_g, ln_b, w_s, b_s, lam_init, need_ctx_out):
    sm, cm, gm = mods["lat"][0:3]
    smc, cmc, gmc = mods["ctx"][0:3]
    n = x.shape[0]
    w_in_b = w_in.astype(BF16)
    w_out_b = w_out.astype(BF16)
    ws_b = w_s.astype(BF16)
    bs_b = jnp.broadcast_to(b_s[:, :, None], (B_GROUPS, CHUNK, B_GROUP_W))
    cos_l, sin_l = _rope_tables(n, True)
    cos_c, sin_c = _rope_tables(ctx.shape[0], False)
    w_v_t = w_in[:, 2 * QK_COLS:2 * QK_COLS + A_WIDTH].T.astype(BF16)
    q_l, k_l, vt_l, uv_l = _in_projection(x, g_mix, sm, cm, w_in_b, w_v_t, cos_l, sin_l)
    q_c, k_c, vt_c, uv_c = _in_projection(ctx, g_mix, smc, cmc, w_in_b, w_v_t, cos_c, sin_c)
    a_l = _diff_attention(q_l, [k_c, k_l], [vt_c, vt_l], lamv, g_subln, lam_init)
    x = _finish_even(a_l, uv_l, x, w_out_b, ws_b, bs_b, ln_g, ln_b, gm)
    if need_ctx_out:
        a_c = _diff_attention(q_c, [k_c], [vt_c], lamv, g_subln, lam_init)
        ctx = _finish_even(a_c, uv_c, ctx, w_out_b, ws_b, bs_b, ln_g, ln_b, gmc)
    return x, ctx


def kernel(x, c, ctx, c_ctx, w_mod, b_mod, g_norm_mix, g_norm_ffn, w_in, w_out, lam_q1, lam_k1, lam_q2, lam_k2,
           g_subln, sgu_ln_g, sgu_ln_b, w_spatial, b_spatial, w_fourier_out, w_router, w_gate, w_up, w_down, g_final):
    assert x.shape[0] == 1 and DEPTH == 2
    d = D_MODEL
    x2 = x[0]
    ctx2 = ctx[0]
    c8 = jnp.zeros((8, d), F32).at[0].set(c[0]).at[1].set(c_ctx)
    mod_all = _modulation(c8, w_mod, b_mod)
    row = lambda v: v.reshape(1, -1)

    def mods_of(i):
        lat = [mod_all[i, 0:1, k * d:(k + 1) * d] for k in range(6)]
        cx = [mod_all[i, 1:2, k * d:(k + 1) * d] for k in range(6)]
        return {"lat": lat, "ctx": cx}

    m0 = mods_of(0)
    lam_init0 = 0.8 - 0.6 * math.exp(-0.3 * 0)
    lamv = jnp.zeros((8, LANES), F32)
    for r, v in enumerate((lam_q1[0], lam_k1[0], lam_q2[0], lam_k2[0])):
        lamv = lamv.at[r, :A_HEAD_DIM].set(v)
    x2, ctx2 = _even_layer_mix(x2, ctx2, m0, row(g_norm_mix[0]), w_in[0], w_out[0], lamv, row(g_subln[0]),
                               row(sgu_ln_g[0]), row(sgu_ln_b[0]), w_spatial[0], b_spatial[0], lam_init0, True)
    x2 = _expert_choice_ffn(x2, row(g_norm_ffn[0]), m0["lat"][3], m0["lat"][4], m0["lat"][5],
                            w_router[0], w_gate, w_up, w_down, 0)
    ctx2 = _expert_choice_ffn(ctx2, row(g_norm_ffn[0]), m0["ctx"][3], m0["ctx"][4], m0["ctx"][5],
                              w_router[0], w_gate, w_up, w_down, 0)

    m1 = mods_of(1)
    x2 = _fourier_mix_layer(x2, row(g_norm_mix[1]), m1["lat"][0], m1["lat"][1], m1["lat"][2], w_fourier_out[0])
    x2 = _expert_choice_ffn(x2, row(g_norm_ffn[1]), m1["lat"][3], m1["lat"][4], m1["lat"][5],
                            w_router[1], w_gate, w_up, w_down, 1)
    out = _final_norm(x2, row(g_final))
    del ctx2
    return out[None]
```

```python
import functools
import math

import numpy as np
import jax
import jax.numpy as jnp
from jax import lax
from jax.experimental import pallas as pl
from jax.experimental.pallas import tpu as pltpu

F32 = jnp.float32
BF16 = jnp.bfloat16
I32 = jnp.int32

D_MODEL = 2048
DEPTH = 2
GRID_W = 64
EPS = 1e-6

A_HEADS = 8
A_HEAD_DIM = 64
A_V_DIM = 2 * A_HEAD_DIM
A_WIDTH = A_HEADS * A_V_DIM
QK_COLS = A_HEADS * 2 * A_HEAD_DIM
ROPE_THETA = 10000.0

B_GROUPS = 8
B_GROUP_W = 128
B_WIDTH = B_GROUPS * B_GROUP_W
CHUNK = 128
IN_COLS = 2 * QK_COLS + A_WIDTH + 2 * B_WIDTH

C_GROUPS = 4
C_GROUP_W = D_MODEL // C_GROUPS

N_EXPERTS = 16
CAPACITY_FACTOR = 2
F_EXPERT = D_MODEL // 2

LANES = 128
MXU_DIM = 256
VMEM_LIMIT_BYTES = 56 * 1024 * 1024

FFT_N2 = 32
ROUTE_TILE = 256
ROUTE_WIN = 256


def _params(sem):
    return pltpu.CompilerParams(dimension_semantics=sem, vmem_limit_bytes=VMEM_LIMIT_BYTES)


def _nt_dot(a, b):
    return lax.dot_general(a, b, (((1,), (1,)), ((), ())), preferred_element_type=F32)


def _norm_mod(x, g, shift, scale):
    ms = jnp.mean(x * x, axis=-1, keepdims=True)
    y = x * lax.rsqrt(ms + EPS) * g
    return y * (1.0 + scale) + shift


def _mod_kernel(c_ref, w_ref, b_ref, o_ref):
    c = c_ref[...]
    s = c * jax.nn.sigmoid(c)
    o_ref[...] = jnp.dot(s.astype(BF16), w_ref[...].astype(BF16), preferred_element_type=F32) + b_ref[...]


def _modulation(c8, w_mod, b_mod):
    depth, d, n6 = w_mod.shape
    tn = 1024
    return pl.pallas_call(
        _mod_kernel,
        grid=(depth, n6 // tn),
        in_specs=[
            pl.BlockSpec((8, d), lambda l, j: (0, 0)),
            pl.BlockSpec((None, d, tn), lambda l, j: (l, 0, j)),
            pl.BlockSpec((None, 1, tn), lambda l, j: (l, 0, j)),
        ],
        out_specs=pl.BlockSpec((None, 8, tn), lambda l, j: (l, 0, j)),
        out_shape=jax.ShapeDtypeStruct((depth, 8, n6), F32),
        compiler_params=_params(("arbitrary", "arbitrary")),
        name="modulation",
    )(c8, w_mod, b_mod.reshape(depth, 1, n6))


PROJ_TN = 512
PROJ_ROW_PARTS = 4
_Q_TILES = QK_COLS // PROJ_TN
_V_TILES = A_WIDTH // PROJ_TN
_UV_TILES = 2 * B_WIDTH // PROJ_TN


def _rope_chunk(z, cos, sin_signed, first_half):
    partner = jnp.where(first_half, pltpu.roll(z, LANES - 16, 1), pltpu.roll(z, 16, 1))
    return z * cos + partner * sin_signed


def _proj_kernel(x_ref, g_ref, sh_ref, sc_ref, w_ref, cos_ref, sin_ref,
                 q_ref, k_ref, vt_ref, uv_ref, h_scr):
    j = pl.program_id(1)

    tm = h_scr.shape[0]

    @pl.when(j == 0)
    def _():
        for r in range(tm // CHUNK):
            rows = slice(r * CHUNK, (r + 1) * CHUNK)
            h_scr[rows, :] = _norm_mod(x_ref[rows, :], g_ref[...], sh_ref[...], sc_ref[...]).astype(BF16)

    n_parts = min(PROJ_ROW_PARTS, tm // LANES)
    parts = [slice(r * (tm // n_parts), (r + 1) * (tm // n_parts)) for r in range(n_parts)]

    def project(rows):
        return jnp.dot(h_scr[rows, :], w_ref[...], preferred_element_type=F32)

    def roped(rows, scale):
        z = project(rows)
        cos = cos_ref[rows, :]
        sin = sin_ref[rows, :]
        lane = lax.broadcasted_iota(I32, cos.shape, 1)
        first_half = (lane % 32) < 16
        chunks = []
        for c in range(PROJ_TN // LANES):
            zc = z[:, c * LANES:(c + 1) * LANES]
            chunks.append(_rope_chunk(zc, cos, sin, first_half) * scale)
        return jnp.concatenate(chunks, axis=1)

    @pl.when(j < _Q_TILES)
    def _():
        for rows in parts:
            q_ref[rows, :] = roped(rows, A_HEAD_DIM ** -0.5 * math.log2(math.e)).astype(BF16)

    @pl.when((j >= _Q_TILES) & (j < 2 * _Q_TILES))
    def _():
        for rows in parts:
            k_ref[rows, :] = roped(rows, 1.0).astype(BF16)

    @pl.when((j >= 2 * _Q_TILES) & (j < 2 * _Q_TILES + _V_TILES))
    def _():
        for rows in parts:
            vt_ref[:, rows] = project(rows).T.astype(BF16)

    @pl.when(j >= 2 * _Q_TILES + _V_TILES)
    def _():
        for rows in parts:
            uv_ref[rows, :] = jax.nn.gelu(project(rows))


def _in_projection(x, g, shift, scale, w_in_bf16, cos_t, sin_t):
    n, d = x.shape
    tm = min(1024, n)
    nq = _Q_TILES
    nj = IN_COLS // PROJ_TN
    row = lambda i, j: (0, 0)
    v_tile = lambda j: jnp.clip(j - 2 * nq, 0, _V_TILES - 1)
    return pl.pallas_call(
        _proj_kernel,
        grid=(n // tm, nj),
        in_specs=[
            pl.BlockSpec((tm, d), lambda i, j: (i, 0)),
            pl.BlockSpec((1, d), row), pl.BlockSpec((1, d), row), pl.BlockSpec((1, d), row),
            pl.BlockSpec((d, PROJ_TN), lambda i, j: (0, j)),
            pl.BlockSpec((tm, LANES), lambda i, j: (i, 0)),
            pl.BlockSpec((tm, LANES), lambda i, j: (i, 0)),
        ],
        out_specs=[
            pl.BlockSpec((tm, PROJ_TN), lambda i, j: (i, jnp.clip(j, 0, nq - 1))),
            pl.BlockSpec((tm, PROJ_TN), lambda i, j: (i, jnp.clip(j - nq, 0, nq - 1))),
            pl.BlockSpec((PROJ_TN, tm), lambda i, j: (v_tile(j), i)),
            pl.BlockSpec((tm, PROJ_TN), lambda i, j: (i, jnp.clip(j - 2 * nq - _V_TILES, 0, _UV_TILES - 1))),
        ],
        out_shape=[
            jax.ShapeDtypeStruct((n, QK_COLS), BF16),
            jax.ShapeDtypeStruct((n, QK_COLS), BF16),
            jax.ShapeDtypeStruct((A_WIDTH, n), BF16),
            jax.ShapeDtypeStruct((n, 2 * B_WIDTH), F32),
        ],
        scratch_shapes=[pltpu.VMEM((tm, d), BF16)],
        compiler_params=_params(("arbitrary", "arbitrary")),
        name="in_projection",
    )(x, g, shift, scale, w_in_bf16, cos_t, sin_t)


def _rope_tables(n, rotate):
    if not rotate:
        return jnp.ones((n, LANES), F32), jnp.zeros((n, LANES), F32)
    t = np.arange(n)
    row = (t // GRID_W).astype(np.float32)
    col = (t % GRID_W).astype(np.float32)
    dim = A_HEAD_DIM // 2
    inv = (np.float32(ROPE_THETA) ** (-np.arange(0, dim, 2, dtype=np.float32) / np.float32(dim))).astype(np.float32)
    ang_r = row[:, None] * inv[None, :]
    ang_c = col[:, None] * inv[None, :]
    ang64 = np.concatenate([ang_r, ang_r, ang_c, ang_c], axis=1)
    sign64 = np.concatenate([-np.ones(16), np.ones(16), -np.ones(16), np.ones(16)]).astype(np.float32)
    ang = np.tile(ang64, (1, LANES // 64))
    sign = np.tile(sign64, LANES // 64)
    return jnp.asarray(np.cos(ang), F32), jnp.asarray(np.sin(ang) * sign[None, :], F32)


def _attn_kernel(*refs, n_parts, tq, tk, lam_init):
    q_ref = refs[0]
    k_parts = refs[1:1 + n_parts]
    vt_parts = refs[1 + n_parts:1 + 2 * n_parts]
    lamv_ref, gs_ref, o_ref, m_scr, acc_scr, qst_scr, sa_scr, sb_scr, k_ref, vt_ref = refs[1 + 2 * n_parts:]
    n_blocks = q_ref.shape[0] // tq
    n_kv = k_ref.shape[0] // tk

    off = 0
    for k_part, vt_part in zip(k_parts, vt_parts):
        rows = k_part.shape[0]
        k_ref[off:off + rows, :] = k_part[...]
        vt_ref[:, off:off + rows] = vt_part[...]
        off += rows

    def query_operand(i):
        qt = q_ref[pl.ds(pl.multiple_of(i * tq, tq), tq), :].astype(F32).T
        row = lax.broadcasted_iota(I32, qt.shape, 0)
        zero = jnp.zeros_like(qt)
        return jnp.concatenate([jnp.where(row < A_HEAD_DIM, qt, zero),
                                jnp.where(row >= A_HEAD_DIM, qt, zero)], axis=1).astype(BF16)

    def scores(qst, j):
        off = pl.multiple_of(j * tk, tk)
        return jnp.dot(k_ref[pl.ds(off, tk), :], qst, preferred_element_type=F32)

    ones_rows = jnp.ones((16, tk), BF16)

    def consume(s, j):
        off = pl.multiple_of(j * tk, tk)
        vtb = jnp.concatenate([vt_ref[:, pl.ds(off, tk)], ones_rows], axis=0)
        m_old = m_scr[...]
        m_new = jnp.maximum(m_old, jnp.max(s, axis=0, keepdims=True))
        alpha = jnp.exp2(m_old - m_new)
        p = jnp.exp2(s - m_new).astype(BF16)
        acc_scr[...] = alpha * acc_scr[...] + jnp.dot(vtb, p, preferred_element_type=F32)
        m_scr[...] = m_new

    lv = lamv_ref[...]
    lam = (jnp.exp(jnp.sum(lv[0:1] * lv[1:2], axis=-1, keepdims=True))
           - jnp.exp(jnp.sum(lv[2:3] * lv[3:4], axis=-1, keepdims=True)) + lam_init)

    cross_block = n_kv % 2 == 0
    qst_scr[...] = query_operand(0)
    if cross_block:
        sa_scr[...] = scores(qst_scr[...], 0)

    def block(i, carry):
        qst = qst_scr[...]
        m_scr[...] = jnp.full(m_scr.shape, -jnp.inf, F32)
        acc_scr[...] = jnp.zeros(acc_scr.shape, F32)
        if not cross_block:
            sa_scr[...] = scores(qst, 0)

        def pair(p, c):
            j = 2 * p
            sb_scr[...] = scores(qst, j + 1)
            consume(sa_scr[...], j)
            sa_scr[...] = scores(qst, j + 2)
            consume(sb_scr[...], j + 1)
            return c

        qst_next = query_operand(jnp.minimum(i + 1, n_blocks - 1))
        if cross_block:
            lax.fori_loop(0, n_kv // 2 - 1, pair, 0)
            sb_scr[...] = scores(qst, n_kv - 1)
            consume(sa_scr[...], n_kv - 2)
            sa_scr[...] = scores(qst_next, 0)
            consume(sb_scr[...], n_kv - 1)
        else:
            lax.fori_loop(0, (n_kv - 1) // 2, pair, 0)
            consume(sa_scr[...], n_kv - 1)
        qst_scr[...] = qst_next

        ot = acc_scr[:A_V_DIM, :] / acc_scr[A_V_DIM:A_V_DIM + 1, :]
        o = (ot[:, :tq] - lam * ot[:, tq:]).T
        a = o * lax.rsqrt(jnp.mean(o * o, axis=-1, keepdims=True) + EPS) * gs_ref[...]
        o_ref[pl.ds(pl.multiple_of(i * tq, tq), tq), :] = (a * (1.0 - lam_init)).astype(BF16)
        return carry

    lax.fori_loop(0, n_blocks, block, 0)


def _pick_tile(n, candidates):
    for c in candidates:
        if n % c == 0:
            return c
    raise ValueError(f"no tile for {n}")


def _diff_attention(q, k_parts, vt_parts, lamv, g_subln, lam_init):
    n = q.shape[0]
    nk = sum(k.shape[0] for k in k_parts)
    tq = min(512, n)
    tk = _pick_tile(nk, (1408, 768, 512, 256))
    score_buf = pltpu.VMEM((tk, 2 * tq), F32)
    return pl.pallas_call(
        functools.partial(_attn_kernel, n_parts=len(k_parts), tq=tq, tk=tk, lam_init=lam_init),
        grid=(A_HEADS,),
        in_specs=[pl.BlockSpec((n, A_V_DIM), lambda h: (0, h))]
        + [pl.BlockSpec((k.shape[0], A_V_DIM), lambda h: (0, h)) for k in k_parts]
        + [pl.BlockSpec((A_V_DIM, vt.shape[1]), lambda h: (h, 0)) for vt in vt_parts]
        + [pl.BlockSpec((8, LANES), lambda h: (0, 0)), pl.BlockSpec((1, A_V_DIM), lambda h: (0, 0))],
        out_specs=pl.BlockSpec((n, A_V_DIM), lambda h: (0, h)),
        out_shape=jax.ShapeDtypeStruct((n, A_WIDTH), BF16),
        scratch_shapes=[pltpu.VMEM((1, 2 * tq), F32),
                        pltpu.VMEM((A_V_DIM + 16, 2 * tq), F32),
                        pltpu.VMEM((A_V_DIM, 2 * tq), BF16),
                        score_buf, score_buf,
                        pltpu.VMEM((nk, A_V_DIM), BF16), pltpu.VMEM((A_V_DIM, nk), BF16)],
        compiler_params=_params(("arbitrary",)),
        name="diff_attention",
    )(q, *k_parts, *vt_parts, lamv, g_subln)


def _finish_kernel(a_ref, uv_ref, x_ref, wout_ref, ws_ref, bs_ref, lng_ref, lnb_ref, gm_ref, o_ref, cat_scr):
    tm = a_ref.shape[0]
    cat_scr[:, :A_WIDTH] = a_ref[...]
    for c in range(tm // CHUNK):
        rows = slice(c * CHUNK, (c + 1) * CHUNK)
        for g in range(B_GROUPS):
            cols = slice(g * B_GROUP_W, (g + 1) * B_GROUP_W)
            u = uv_ref[rows, g * B_GROUP_W:(g + 1) * B_GROUP_W]
            v = uv_ref[rows, B_WIDTH + g * B_GROUP_W:B_WIDTH + (g + 1) * B_GROUP_W]
            mu = jnp.mean(v, axis=-1, keepdims=True)
            var = jnp.mean(jnp.square(v - mu), axis=-1, keepdims=True)
            vn = (v - mu) * lax.rsqrt(var + EPS) * lng_ref[:, cols] + lnb_ref[:, cols]
            mixed = jnp.dot(ws_ref[g], vn.astype(BF16), preferred_element_type=F32) + bs_ref[g]
            cat_scr[rows, A_WIDTH + g * B_GROUP_W:A_WIDTH + (g + 1) * B_GROUP_W] = (u * mixed).astype(BF16)
    y = jnp.dot(cat_scr[...], wout_ref[...], preferred_element_type=F32)
    o_ref[...] = x_ref[...] + gm_ref[...] * y


def _finish_even(a, uv, x, w_out_bf16, ws_bf16, bs_b, ln_g, ln_b, gm):
    n, d = x.shape
    tm = min(512, n)
    row = lambda i: (0, 0)
    return pl.pallas_call(
        _finish_kernel,
        grid=(n // tm,),
        in_specs=[
            pl.BlockSpec((tm, A_WIDTH), lambda i: (i, 0)),
            pl.BlockSpec((tm, 2 * B_WIDTH), lambda i: (i, 0)),
            pl.BlockSpec((tm, d), lambda i: (i, 0)),
            pl.BlockSpec((A_WIDTH + B_WIDTH, d), row),
            pl.BlockSpec((B_GROUPS, CHUNK, CHUNK), lambda i: (0, 0, 0)),
            pl.BlockSpec((B_GROUPS, CHUNK, B_GROUP_W), lambda i: (0, 0, 0)),
            pl.BlockSpec((1, B_WIDTH), row), pl.BlockSpec((1, B_WIDTH), row),
            pl.BlockSpec((1, d), row),
        ],
        out_specs=pl.BlockSpec((tm, d), lambda i: (i, 0)),
        out_shape=jax.ShapeDtypeStruct((n, d), F32),
        scratch_shapes=[pltpu.VMEM((tm, A_WIDTH + B_WIDTH), BF16)],
        compiler_params=_params(("arbitrary",)),
        name="finish_even",
    )(a, uv, x, w_out_bf16, ws_bf16, bs_b, ln_g, ln_b, gm)


def _router_kernel(x_ref, g_ref, sh_ref, sc_ref, wrt_ref, ht_ref, aff_ref):
    h = _norm_mod(x_ref[...], g_ref[...], sh_ref[...], sc_ref[...])
    logits = _nt_dot(wrt_ref[...], h.astype(BF16))
    m = jnp.max(logits, axis=0, keepdims=True)
    e = jnp.exp(logits - m)
    aff_ref[...] = e / jnp.sum(e, axis=0, keepdims=True)
    ht_ref[...] = h.T.astype(BF16)


def _router(x, g, shift, scale, w_router_t_bf16):
    n, d = x.shape
    tm = min(512, n)
    row = lambda i: (0, 0)
    return pl.pallas_call(
        _router_kernel,
        grid=(n // tm,),
        in_specs=[
            pl.BlockSpec((tm, d), lambda i: (i, 0)),
            pl.BlockSpec((1, d), row), pl.BlockSpec((1, d), row), pl.BlockSpec((1, d), row),
            pl.BlockSpec((N_EXPERTS, d), row),
        ],
        out_specs=[pl.BlockSpec((d, tm), lambda i: (0, i)), pl.BlockSpec((N_EXPERTS, tm), lambda i: (0, i))],
        out_shape=[jax.ShapeDtypeStruct((d, n), BF16), jax.ShapeDtypeStruct((N_EXPERTS, n), F32)],
        compiler_params=_params(("arbitrary",)),
        name="router",
    )(x, g, shift, scale, w_router_t_bf16)


def _select_kernel(aff_ref, sel_ref, pos_ref, g_ref, first_ref, *, cap, capp):
    aff = aff_ref[...]
    n = aff.shape[1]
    idx = lax.broadcasted_iota(I32, aff.shape, 1)
    capf = float(cap)

    def count(mask):
        return jnp.sum(jnp.where(mask, 1.0, 0.0), axis=1, keepdims=True)

    def as_float(bits):
        return pltpu.bitcast(bits, F32)

    def thr_body(i, thr):
        cand = thr | jnp.left_shift(jnp.int32(1), 30 - i)
        return jnp.where(count(aff >= as_float(cand)) >= capf, cand, thr)

    thr = lax.fori_loop(0, 31, thr_body, jnp.zeros((aff.shape[0], 1), I32))
    gt = aff >= as_float(thr + 1)
    eq = (aff >= as_float(thr)) & jnp.logical_not(gt)
    need = capf - count(gt)
    nbits = int(n).bit_length()

    def tie_body(i, lim):
        cand = lim | jnp.left_shift(jnp.int32(1), nbits - 1 - i)
        return jnp.where(count(eq & (idx < cand)) <= need, cand, lim)

    lim = lax.fori_loop(0, nbits, tie_body, jnp.zeros((aff.shape[0], 1), I32))
    sel = jnp.where(gt | (eq & (idx < lim)), 1.0, 0.0)
    sel_ref[...] = sel
    g_ref[...] = aff * sel

    ri = lax.broadcasted_iota(I32, (LANES, LANES), 0)
    ci = lax.broadcasted_iota(I32, (LANES, LANES), 1)
    upper = jnp.where(ri < ci, 1.0, 0.0).astype(BF16)
    carry = jnp.zeros((aff.shape[0], 1), F32)
    for c in range(n // LANES):
        m = sel[:, c * LANES:(c + 1) * LANES]
        within = jnp.dot(m.astype(BF16), upper, preferred_element_type=F32)
        pos_ref[:, c * LANES:(c + 1) * LANES] = (within + carry).astype(I32)
        carry = carry + jnp.sum(m, axis=1, keepdims=True)

    incl = pos_ref[...].astype(F32) + sel
    lane = lax.broadcasted_iota(I32, first_ref.shape, 1)
    first = jnp.zeros(first_ref.shape, F32)
    for w in range(capp // ROUTE_WIN + 1):
        first = jnp.where(lane == w, count(incl <= float(w * ROUTE_WIN)), first)
    first_ref[...] = first.astype(I32)


def _select(aff_t, cap, capp):
    e, n = aff_t.shape
    full = lambda: (0, 0)
    return pl.pallas_call(
        functools.partial(_select_kernel, cap=cap, capp=capp),
        grid=(),
        in_specs=[pl.BlockSpec((e, n), full)],
        out_specs=[pl.BlockSpec((e, n), full)] * 3 + [pl.BlockSpec((e, LANES), full)],
        out_shape=[jax.ShapeDtypeStruct((e, n), F32), jax.ShapeDtypeStruct((e, n), I32),
                   jax.ShapeDtypeStruct((e, n), F32), jax.ShapeDtypeStruct((e, LANES), I32)],
        compiler_params=pltpu.CompilerParams(vmem_limit_bytes=VMEM_LIMIT_BYTES),
        name="expert_select",
    )(aff_t)


def _one_hot_rows(rel, win):
    r = lax.broadcasted_iota(I32, (win, rel.shape[1]), 0)
    return jnp.where(r == rel, 1.0, 0.0).astype(BF16)


def _one_hot_window(pos_row, sel_row, lo, base, win=ROUTE_WIN):
    rel = jnp.where((sel_row > 0.0) & (pos_row >= lo), pos_row - base, -1)
    return _one_hot_rows(rel, win)


def _window_plan(p0, p1, win=ROUTE_WIN):
    start = (p0 // LANES) * LANES
    n_chunks = jnp.where(p1 > p0, (p1 - start + win - 1) // win, 0)
    return start, n_chunks


def _dispatch_kernel(first_ref, ht_hbm, sel_ref, pos_ref, g_ref, xs_ref, gslot_ref, ht_scr, acc_scr, gs_scr, sem,
                     *, k_chunk):
    e = pl.program_id(0)
    w = pl.program_id(1)
    n = ht_scr.shape[1]

    @pl.when((e == 0) & (w == 0))
    def _():
        load = pltpu.make_async_copy(ht_hbm, ht_scr, sem)
        load.start()
        load.wait()

    t0 = first_ref[e, w]
    t1 = first_ref[e, w + 1]
    start = (t0 // ROUTE_TILE) * ROUTE_TILE
    n_chunks = jnp.where(t1 > t0, (t1 - start + k_chunk - 1) // k_chunk, 0)
    slot0 = w * ROUTE_WIN

    def chunk(k):
        lo_tok = start + k * k_chunk
        base = pl.multiple_of(jnp.minimum(lo_tok, n - k_chunk), ROUTE_TILE)
        toks = pl.ds(base, k_chunk)
        tok = base + lax.broadcasted_iota(I32, (1, k_chunk), 1)
        pos_row = pos_ref[0, :, toks]
        keep = (sel_ref[0, :, toks] > 0.0) & (tok >= lo_tok)
        rel = jnp.where(keep, pos_row - slot0, -1)
        onehot = _one_hot_rows(rel, ROUTE_WIN)
        g_row = g_ref[0, :, toks]
        g_hi = g_row.astype(BF16)
        r1 = g_row - g_hi.astype(F32)
        g_mid = r1.astype(BF16)
        g_lo = (r1 - g_mid.astype(F32)).astype(BF16)
        g8 = jnp.concatenate([g_hi, g_mid, g_lo, jnp.zeros((5, k_chunk), BF16)], axis=0)
        return _nt_dot(ht_scr[:, toks], onehot), _nt_dot(g8, onehot)

    acc_scr[...], gs_scr[...] = chunk(0)

    def extra(k, carry):
        x, gs = chunk(k)
        acc_scr[...] += x
        gs_scr[...] += gs
        return carry

    lax.fori_loop(1, n_chunks, extra, 0)
    xs_ref[0] = acc_scr[...].T.astype(BF16)
    gs = gs_scr[...]
    gslot_ref[0] = gs[0:1] + gs[1:2] + gs[2:3]


def _dispatch(first_tok, h_t, sel3, pos3, g3, capp):
    d, n = h_t.shape
    k_chunk = min(10 * ROUTE_TILE, n)
    row3 = pl.BlockSpec((1, 1, n), lambda e, w, ft: (e, 0, 0))
    return pl.pallas_call(
        functools.partial(_dispatch_kernel, k_chunk=k_chunk),
        grid_spec=pltpu.PrefetchScalarGridSpec(
            num_scalar_prefetch=1,
            grid=(N_EXPERTS, capp // ROUTE_WIN),
            in_specs=[pl.BlockSpec(memory_space=pl.ANY), row3, row3, row3],
            out_specs=[
                pl.BlockSpec((1, ROUTE_WIN, d), lambda e, w, ft: (e, w, 0)),
                pl.BlockSpec((1, 1, ROUTE_WIN), lambda e, w, ft: (e, 0, w)),
            ],
            scratch_shapes=[pltpu.VMEM((d, n), BF16), pltpu.VMEM((d, ROUTE_WIN), F32),
                            pltpu.VMEM((8, ROUTE_WIN), F32), pltpu.SemaphoreType.DMA],
        ),
        out_shape=[jax.ShapeDtypeStruct((N_EXPERTS, capp, d), BF16),
                   jax.ShapeDtypeStruct((N_EXPERTS, 1, capp), F32)],
        compiler_params=_params(("arbitrary", "arbitrary")),
        name="moe_dispatch",
    )(first_tok, h_t, sel3, pos3, g3)


def _ffn_up_kernel(xs_ref, wg_ref, wu_ref, h_ref):
    xs = xs_ref[0].astype(BF16)
    a = jnp.dot(xs, wg_ref[0].astype(BF16), preferred_element_type=F32)
    u = jnp.dot(xs, wu_ref[0].astype(BF16), preferred_element_type=F32)
    h_ref[0] = (a * jax.nn.sigmoid(a) * u).astype(BF16)


def _ffn_up(xs, w_gate, w_up, layer):
    e, capp, d = xs.shape
    f = w_gate.shape[3]
    tf = 512
    return pl.pallas_call(
        _ffn_up_kernel,
        grid=(e, f // tf),
        in_specs=[
            pl.BlockSpec((1, capp, d), lambda i, j: (i, 0, 0)),
            pl.BlockSpec((None, 1, d, tf), lambda i, j: (layer, i, 0, j)),
            pl.BlockSpec((None, 1, d, tf), lambda i, j: (layer, i, 0, j)),
        ],
        out_specs=pl.BlockSpec((1, capp, tf), lambda i, j: (i, 0, j)),
        out_shape=jax.ShapeDtypeStruct((e, capp, f), BF16),
        compiler_params=_params(("arbitrary", "arbitrary")),
        name="moe_ffn_up",
    )(xs, w_gate, w_up)


def _ffn_down_kernel(h_ref, wd_ref, gslot_ref, y_ref):
    y = jnp.dot(h_ref[0], wd_ref[0].astype(BF16), preferred_element_type=F32)
    y_ref[0] = (y.T * gslot_ref[0]).astype(BF16)


def _ffn_down(h, w_down, gslot, layer):
    e, capp, f = h.shape
    d = w_down.shape[3]
    td = 1024
    return pl.pallas_call(
        _ffn_down_kernel,
        grid=(e, d // td),
        in_specs=[
            pl.BlockSpec((1, capp, f), lambda i, j: (i, 0, 0)),
            pl.BlockSpec((None, 1, f, td), lambda i, j: (layer, i, 0, j)),
            pl.BlockSpec((1, 1, capp), lambda i, j: (i, 0, 0)),
        ],
        out_specs=pl.BlockSpec((1, td, capp), lambda i, j: (i, j, 0)),
        out_shape=jax.ShapeDtypeStruct((e, d, capp), BF16),
        compiler_params=_params(("arbitrary", "arbitrary")),
        name="moe_ffn_down",
    )(h, w_down, gslot)


COMBINE_TD = 512
COMBINE_WIN = ROUTE_WIN


def _combine_kernel(ps_ref, pe_ref, y_ref, sel_ref, pos_ref, x_ref, gate_ref, o_ref, acc_scr):
    t = pl.program_id(1)
    n_exp, _, capp = y_ref.shape

    def window(e, k):
        start, _ = _window_plan(ps_ref[e, t], pe_ref[e, t], COMBINE_WIN)
        lo = start + k * COMBINE_WIN
        base = pl.multiple_of(jnp.minimum(lo, capp - COMBINE_WIN), LANES)
        onehot = _one_hot_window(pos_ref[e, pl.ds(t, 1), :], sel_ref[e, pl.ds(t, 1), :], lo, base, COMBINE_WIN)
        return [y_ref[e, :, pl.ds(base, COMBINE_WIN)]], [onehot]

    lhs, rhs = [], []
    for e in range(n_exp):
        rows, hots = window(e, 0)
        lhs += rows
        rhs += hots
    acc_scr[...] = jnp.dot(jnp.concatenate(lhs, axis=1), jnp.concatenate(rhs, axis=0), preferred_element_type=F32)

    for e in range(n_exp):
        n_chunks = _window_plan(ps_ref[e, t], pe_ref[e, t], COMBINE_WIN)[1]

        def extra(k, carry, e=e):
            rows, hots = window(e, k)
            acc_scr[...] += jnp.dot(jnp.concatenate(rows, axis=1), jnp.concatenate(hots, axis=0),
                                    preferred_element_type=F32)
            return carry

        lax.fori_loop(1, n_chunks, extra, 0)

    o_ref[...] = x_ref[...] + gate_ref[...] * acc_scr[...].T


def _combine(tile_start, tile_end, y_t, sel_t, pos_t, x, gate):
    e, d, capp = y_t.shape
    n = x.shape[0]
    n_tiles = n // ROUTE_TILE
    td = COMBINE_TD
    y_blk = pl.BlockSpec((e, td, capp), lambda j, t, ps, pe: (0, j, 0))
    full = pl.BlockSpec((e, n_tiles, ROUTE_TILE), lambda j, t, ps, pe: (0, 0, 0))
    tile = pl.BlockSpec((ROUTE_TILE, td), lambda j, t, ps, pe: (t, j))
    return pl.pallas_call(
        _combine_kernel,
        grid_spec=pltpu.PrefetchScalarGridSpec(
            num_scalar_prefetch=2,
            grid=(d // td, n_tiles),
            in_specs=[y_blk, full, full, tile, pl.BlockSpec((1, td), lambda j, t, ps, pe: (0, j))],
            out_specs=tile,
            scratch_shapes=[pltpu.VMEM((td, ROUTE_TILE), F32)],
        ),
        out_shape=jax.ShapeDtypeStruct((n, d), F32),
        compiler_params=_params(("arbitrary", "arbitrary")),
        name="moe_combine",
    )(tile_start, tile_end, y_t, sel_t, pos_t, x, gate)


def _tile_bounds(pos, tile, cap):
    start = pos[:, ::tile]
    end = jnp.concatenate([start[:, 1:], jnp.full((pos.shape[0], 1), cap, I32)], axis=1)
    return start, end


def _expert_choice_ffn(x, g, shift, scale, gate, w_router, w_gate, w_up, w_down, layer):
    n, d = x.shape
    cap = CAPACITY_FACTOR * n // N_EXPERTS
    capp = max(cap, ROUTE_WIN)
    h_t, aff_t = _router(x, g, shift, scale, w_router.T.astype(BF16))
    sel, pos, gsel, first_tok = _select(aff_t, cap, capp)
    as3 = lambda a: a.reshape(N_EXPERTS, 1, n)
    xs, gslot = _dispatch(first_tok, h_t, as3(sel), as3(pos), as3(gsel), capp)
    hidden = _ffn_up(xs, w_gate, w_up, layer)
    y_t = _ffn_down(hidden, w_down, gslot, layer)
    as_tiles = lambda a: a.reshape(N_EXPERTS, n // ROUTE_TILE, ROUTE_TILE)
    return _combine(*_tile_bounds(pos, ROUTE_TILE, cap), y_t, as_tiles(sel), as_tiles(pos), x, gate)


def _final_kernel(x_ref, g_ref, o_ref):
    x = x_ref[...]
    o_ref[...] = x * lax.rsqrt(jnp.mean(x * x, axis=-1, keepdims=True) + EPS) * g_ref[...]


def _final_norm(x, g_final):
    n, d = x.shape
    tm = min(512, n)
    return pl.pallas_call(
        _final_kernel,
        grid=(n // tm,),
        in_specs=[pl.BlockSpec((tm, d), lambda i: (i, 0)), pl.BlockSpec((1, d), lambda i: (0, 0))],
        out_specs=pl.BlockSpec((tm, d), lambda i: (i, 0)),
        out_shape=jax.ShapeDtypeStruct((n, d), F32),
        compiler_params=_params(("arbitrary",)),
        name="final_norm",
    )(x, g_final)


FFT_K1_BLK = 16


def _fourier_in_kernel(x_ref, g_ref, sh_ref, sc_ref, cs_ref, perm_ref, a_ref, b_ref):
    h = _norm_mod(x_ref[...], g_ref[...], sh_ref[...], sc_ref[...]).astype(BF16)
    hp = jnp.dot(perm_ref[...], h, preferred_element_type=F32).astype(BF16)
    n2, t1_blk = a_ref.shape[0], a_ref.shape[1]
    for g in range(C_GROUPS):
        cols = slice(g * C_GROUP_W, (g + 1) * C_GROUP_W)
        ab = jnp.dot(hp[:, cols], cs_ref[...], preferred_element_type=F32)
        a_ref[:, :, cols] = ab[:, :C_GROUP_W].astype(BF16).reshape(n2, t1_blk, C_GROUP_W)
        b_ref[:, :, cols] = ab[:, C_GROUP_W:].astype(BF16).reshape(n2, t1_blk, C_GROUP_W)


def _fourier_in(x, g, shift, scale, cs):
    n, d = x.shape
    tm = 512
    n1 = n // FFT_N2
    t1_blk = tm // FFT_N2
    row = lambda i: (0, 0)
    tile = pl.BlockSpec((tm, d), lambda i: (i, 0))
    ab_blk = pl.BlockSpec((FFT_N2, t1_blk, d), lambda i: (0, i, 0))
    ab_shape = jax.ShapeDtypeStruct((FFT_N2, n1, d), BF16)
    perm = np.zeros((tm, tm), np.float32)
    src = np.arange(tm)
    perm[(src % FFT_N2) * t1_blk + src // FFT_N2, src] = 1.0
    return pl.pallas_call(
        _fourier_in_kernel,
        grid=(n // tm,),
        in_specs=[tile, pl.BlockSpec((1, d), row), pl.BlockSpec((1, d), row), pl.BlockSpec((1, d), row),
                  pl.BlockSpec((C_GROUP_W, 2 * C_GROUP_W), row), pl.BlockSpec((tm, tm), row)],
        out_specs=[ab_blk, ab_blk],
        out_shape=[ab_shape, ab_shape],
        compiler_params=_params(("arbitrary",)),
        name="fourier_channel_dft",
    )(x, g, shift, scale, cs, jnp.asarray(perm, BF16))


def _fourier_stage1_kernel(a_ref, b_ref, ma_ref, mb_ref, ct_ref, st_ref, zr_ref, zi_ref):
    n1 = a_ref.shape[1]
    z = (jnp.dot(ma_ref[...], a_ref[0], preferred_element_type=F32)
         + jnp.dot(mb_ref[...], b_ref[0], preferred_element_type=F32))
    ct = ct_ref[0]
    st = st_ref[0]
    for c in range(a_ref.shape[2] // LANES):
        cols = slice(c * LANES, (c + 1) * LANES)
        zr = z[:n1, cols]
        zi = z[n1:, cols]
        zr_ref[0, :, cols] = (zr * ct + zi * st).astype(BF16)
        zi_ref[0, :, cols] = (zi * ct - zr * st).astype(BF16)


def _fourier_stage1(a3, b3, ma, mb, ct, st):
    n2, n1, d = a3.shape
    blk = pl.BlockSpec((1, n1, d), lambda j: (j, 0, 0))
    mat = pl.BlockSpec((2 * n1, n1), lambda j: (0, 0))
    tw = pl.BlockSpec((1, n1, LANES), lambda j: (j, 0, 0))
    shp = jax.ShapeDtypeStruct((n2, n1, d), BF16)
    return pl.pallas_call(
        _fourier_stage1_kernel,
        grid=(n2,),
        in_specs=[blk, blk, mat, mat, tw, tw],
        out_specs=[blk, blk],
        out_shape=[shp, shp],
        compiler_params=_params(("arbitrary",)),
        name="fourier_stage1",
    )(a3, b3, ma, mb, ct, st)


def _fourier_out_kernel(zr_ref, zi_ref, bc_ref, bs_ref, wo_ref, x_ref, gm_ref, o_ref):
    rows = zr_ref.shape[0] * zr_ref.shape[1]
    d = zr_ref.shape[2]
    zr = zr_ref[...].reshape(rows, d)
    zi = zi_ref[...].reshape(rows, d)
    f = (jnp.dot(bc_ref[...], zr, preferred_element_type=F32)
         + jnp.dot(bs_ref[...], zi, preferred_element_type=F32))
    y = jnp.dot(f.astype(BF16), wo_ref[...], preferred_element_type=F32)
    o_ref[...] = x_ref[...] + gm_ref[...] * y.reshape(o_ref.shape)


def _fourier_out(zr3, zi3, bd_c, bd_s, w_o_bf16, x, gm):
    n2, n1, d = zr3.shape
    n = n1 * n2
    rows = n2 * FFT_K1_BLK
    x3 = x.reshape(n2, n1, d)
    blk = pl.BlockSpec((n2, FFT_K1_BLK, d), lambda i: (0, i, 0))
    const = lambda i: (0, 0)
    out = pl.pallas_call(
        _fourier_out_kernel,
        grid=(n1 // FFT_K1_BLK,),
        in_specs=[blk, blk, pl.BlockSpec((rows, rows), const), pl.BlockSpec((rows, rows), const),
                  pl.BlockSpec((d, d), const), blk, pl.BlockSpec((1, d), const)],
        out_specs=blk,
        out_shape=jax.ShapeDtypeStruct((n2, n1, d), F32),
        compiler_params=_params(("arbitrary",)),
        name="fourier_stage2_out",
    )(zr3, zi3, bd_c, bd_s, w_o_bf16, x3, gm)
    return out.reshape(n, d)


def _fourier_constants(n):
    n1, n2 = n // FFT_N2, FFT_N2
    two_pi = 2.0 * np.pi

    def angles(a, b, period):
        return two_pi * ((np.outer(a, b) % period).astype(np.float64) / period)

    kc = np.arange(C_GROUP_W)
    ang = angles(kc, kc, C_GROUP_W)
    cs = np.concatenate([np.cos(ang), np.sin(ang)], axis=1) / np.sqrt(C_GROUP_W)
    k1 = np.arange(n1)
    ang1 = angles(k1, k1, n1)
    c1, s1 = np.cos(ang1) / np.sqrt(n), np.sin(ang1) / np.sqrt(n)
    ma = np.concatenate([c1, -s1], axis=0)
    mb = np.concatenate([-s1, -c1], axis=0)
    t2 = np.arange(n2)
    angt = angles(t2, k1, n)
    ct = np.repeat(np.cos(angt)[:, :, None], LANES, axis=2)
    st = np.repeat(np.sin(angt)[:, :, None], LANES, axis=2)
    ang2 = angles(t2, t2, n2)
    k1_blk = FFT_K1_BLK
    bd_c = np.zeros((n2 * k1_blk, n2 * k1_blk))
    bd_s = np.zeros((n2 * k1_blk, n2 * k1_blk))
    for kl in range(k1_blk):
        bd_c[kl::k1_blk, kl::k1_blk] = np.cos(ang2)
        bd_s[kl::k1_blk, kl::k1_blk] = np.sin(ang2)
    bf = lambda a: jnp.asarray(a, F32).astype(BF16)
    return bf(cs), bf(ma), bf(mb), jnp.asarray(ct, F32), jnp.asarray(st, F32), bf(bd_c), bf(bd_s)


def _fourier_mix_layer(x, g, shift, scale, gm, w_o):
    cs, ma, mb, ct, st, bd_c, bd_s = _fourier_constants(x.shape[0])
    a3, b3 = _fourier_in(x, g, shift, scale, cs)
    zr3, zi3 = _fourier_stage1(a3, b3, ma, mb, ct, st)
    return _fourier_out(zr3, zi3, bd_c, bd_s, w_o.astype(BF16), x, gm)


def _even_layer_mix(x, ctx, mods, g_mix, w_in, w_out, lamv, g_subln, ln_g, ln_b, w_s, b_s, lam_init, need_ctx_out):
    sm, cm, gm = mods["lat"][0:3]
    smc, cmc, gmc = mods["ctx"][0:3]
    n = x.shape[0]
    w_in_b = w_in.astype(BF16)
    w_out_b = w_out.astype(BF16)
    ws_b = w_s.astype(BF16)
    bs_b = jnp.broadcast_to(b_s[:, :, None], (B_GROUPS, CHUNK, B_GROUP_W))
    cos_l, sin_l = _rope_tables(n, True)
    cos_c, sin_c = _rope_tables(ctx.shape[0], False)
    q_l, k_l, vt_l, uv_l = _in_projection(x, g_mix, sm, cm, w_in_b, cos_l, sin_l)
    q_c, k_c, vt_c, uv_c = _in_projection(ctx, g_mix, smc, cmc, w_in_b, cos_c, sin_c)
    a_l = _diff_attention(q_l, [k_c, k_l], [vt_c, vt_l], lamv, g_subln, lam_init)
    x = _finish_even(a_l, uv_l, x, w_out_b, ws_b, bs_b, ln_g, ln_b, gm)
    if need_ctx_out:
        a_c = _diff_attention(q_c, [k_c], [vt_c], lamv, g_subln, lam_init)
        ctx = _finish_even(a_c, uv_c, ctx, w_out_b, ws_b, bs_b, ln_g, ln_b, gmc)
    return x, ctx


def kernel(x, c, ctx, c_ctx, w_mod, b_mod, g_norm_mix, g_norm_ffn, w_in, w_out, lam_q1, lam_k1, lam_q2, lam_k2,
           g_subln, sgu_ln_g, sgu_ln_b, w_spatial, b_spatial, w_fourier_out, w_router, w_gate, w_up, w_down, g_final):
    assert x.shape[0] == 1 and DEPTH == 2
    d = D_MODEL
    x2 = x[0]
    ctx2 = ctx[0]
    c8 = jnp.zeros((8, d), F32).at[0].set(c[0]).at[1].set(c_ctx)
    mod_all = _modulation(c8, w_mod, b_mod)
    row = lambda v: v.reshape(1, -1)

    def mods_of(i):
        lat = [mod_all[i, 0:1, k * d:(k + 1) * d] for k in range(6)]
        cx = [mod_all[i, 1:2, k * d:(k + 1) * d] for k in range(6)]
        return {"lat": lat, "ctx": cx}

    m0 = mods_of(0)
    lam_init0 = 0.8 - 0.6 * math.exp(-0.3 * 0)
    lamv = jnp.zeros((8, LANES), F32)
    for r, v in enumerate((lam_q1[0], lam_k1[0], lam_q2[0], lam_k2[0])):
        lamv = lamv.at[r, :A_HEAD_DIM].set(v)
    x2, ctx2 = _even_layer_mix(x2, ctx2, m0, row(g_norm_mix[0]), w_in[0], w_out[0], lamv, row(g_subln[0]),
                               row(sgu_ln_g[0]), row(sgu_ln_b[0]), w_spatial[0], b_spatial[0], lam_init0, True)
    x2 = _expert_choice_ffn(x2, row(g_norm_ffn[0]), m0["lat"][3], m0["lat"][4], m0["lat"][5],
                            w_router[0], w_gate, w_up, w_down, 0)
    ctx2 = _expert_choice_ffn(ctx2, row(g_norm_ffn[0]), m0["ctx"][3], m0["ctx"][4], m0["ctx"][5],
                              w_router[0], w_gate, w_up, w_down, 0)

    m1 = mods_of(1)
    x2 = _fourier_mix_layer(x2, row(g_norm_mix[1]), m1["lat"][0], m1["lat"][1], m1["lat"][2], w_fourier_out[0])
    x2 = _expert_choice_ffn(x2, row(g_norm_ffn[1]), m1["lat"][3], m1["lat"][4], m1["lat"][5],
                            w_router[1], w_gate, w_up, w_down, 1)
    out = _final_norm(x2, row(g_final))
    del ctx2
    return out[None]
```

```python
import functools
import math

import numpy as np
import jax
import jax.numpy as jnp
from jax import lax
from jax.experimental import pallas as pl
from jax.experimental.pallas import tpu as pltpu

F32 = jnp.float32
BF16 = jnp.bfloat16
I32 = jnp.int32

D_MODEL = 2048
DEPTH = 2
GRID_W = 64
EPS = 1e-6

A_HEADS = 8
A_HEAD_DIM = 64
A_V_DIM = 2 * A_HEAD_DIM
A_WIDTH = A_HEADS * A_V_DIM
QK_COLS = A_HEADS * 2 * A_HEAD_DIM
ROPE_THETA = 10000.0

B_GROUPS = 8
B_GROUP_W = 128
B_WIDTH = B_GROUPS * B_GROUP_W
CHUNK = 128
IN_COLS = 2 * QK_COLS + A_WIDTH + 2 * B_WIDTH

C_GROUPS = 4
C_GROUP_W = D_MODEL // C_GROUPS

N_EXPERTS = 16
CAPACITY_FACTOR = 2
F_EXPERT = D_MODEL // 2

LANES = 128
MXU_DIM = 256
VMEM_LIMIT_BYTES = 56 * 1024 * 1024

FFT_N2 = 32
ROUTE_TILE = 256
ROUTE_WIN = 256


def _params(sem):
    return pltpu.CompilerParams(dimension_semantics=sem, vmem_limit_bytes=VMEM_LIMIT_BYTES)


def _nt_dot(a, b):
    return lax.dot_general(a, b, (((1,), (1,)), ((), ())), preferred_element_type=F32)


def _norm_mod(x, g, shift, scale):
    ms = jnp.mean(x * x, axis=-1, keepdims=True)
    y = x * lax.rsqrt(ms + EPS) * g
    return y * (1.0 + scale) + shift


def _mod_kernel(c_ref, w_ref, b_ref, o_ref):
    c = c_ref[...]
    s = c * jax.nn.sigmoid(c)
    o_ref[...] = jnp.dot(s.astype(BF16), w_ref[...].astype(BF16), preferred_element_type=F32) + b_ref[...]


def _modulation(c8, w_mod, b_mod):
    depth, d, n6 = w_mod.shape
    tn = 1024
    return pl.pallas_call(
        _mod_kernel,
        grid=(depth, n6 // tn),
        in_specs=[
            pl.BlockSpec((8, d), lambda l, j: (0, 0)),
            pl.BlockSpec((None, d, tn), lambda l, j: (l, 0, j)),
            pl.BlockSpec((None, 1, tn), lambda l, j: (l, 0, j)),
        ],
        out_specs=pl.BlockSpec((None, 8, tn), lambda l, j: (l, 0, j)),
        out_shape=jax.ShapeDtypeStruct((depth, 8, n6), F32),
        compiler_params=_params(("arbitrary", "arbitrary")),
        name="modulation",
    )(c8, w_mod, b_mod.reshape(depth, 1, n6))


PROJ_TN = 512
PROJ_ROW_PARTS = 4
_Q_TILES = QK_COLS // PROJ_TN
_V_TILES = A_WIDTH // PROJ_TN
_UV_TILES = 2 * B_WIDTH // PROJ_TN


def _rope_chunk(z, cos, sin_signed, first_half):
    partner = jnp.where(first_half, pltpu.roll(z, LANES - 16, 1), pltpu.roll(z, 16, 1))
    return z * cos + partner * sin_signed


def _proj_kernel(x_ref, g_ref, sh_ref, sc_ref, w_ref, cos_ref, sin_ref,
                 q_ref, k_ref, vt_ref, uv_ref, h_scr):
    j = pl.program_id(1)

    tm = h_scr.shape[0]

    @pl.when(j == 0)
    def _():
        for r in range(tm // CHUNK):
            rows = slice(r * CHUNK, (r + 1) * CHUNK)
            h_scr[rows, :] = _norm_mod(x_ref[rows, :], g_ref[...], sh_ref[...], sc_ref[...]).astype(BF16)

    n_parts = min(PROJ_ROW_PARTS, tm // LANES)
    parts = [slice(r * (tm // n_parts), (r + 1) * (tm // n_parts)) for r in range(n_parts)]

    def project(rows):
        return jnp.dot(h_scr[rows, :], w_ref[...], preferred_element_type=F32)

    def roped(rows, scale):
        z = project(rows)
        cos = cos_ref[rows, :]
        sin = sin_ref[rows, :]
        lane = lax.broadcasted_iota(I32, cos.shape, 1)
        first_half = (lane % 32) < 16
        chunks = []
        for c in range(PROJ_TN // LANES):
            zc = z[:, c * LANES:(c + 1) * LANES]
            chunks.append(_rope_chunk(zc, cos, sin, first_half) * scale)
        return jnp.concatenate(chunks, axis=1)

    @pl.when(j < _Q_TILES)
    def _():
        for rows in parts:
            q_ref[rows, :] = roped(rows, A_HEAD_DIM ** -0.5 * math.log2(math.e)).astype(BF16)

    @pl.when((j >= _Q_TILES) & (j < 2 * _Q_TILES))
    def _():
        for rows in parts:
            k_ref[rows, :] = roped(rows, 1.0).astype(BF16)

    @pl.when((j >= 2 * _Q_TILES) & (j < 2 * _Q_TILES + _V_TILES))
    def _():
        for rows in parts:
            vt_ref[:, rows] = project(rows).T.astype(BF16)

    @pl.when(j >= 2 * _Q_TILES + _V_TILES)
    def _():
        for rows in parts:
            uv_ref[rows, :] = jax.nn.gelu(project(rows))


def _in_projection(x, g, shift, scale, w_in_bf16, cos_t, sin_t):
    n, d = x.shape
    tm = min(1024, n)
    nq = _Q_TILES
    nj = IN_COLS // PROJ_TN
    row = lambda i, j: (0, 0)
    v_tile = lambda j: jnp.clip(j - 2 * nq, 0, _V_TILES - 1)
    return pl.pallas_call(
        _proj_kernel,
        grid=(n // tm, nj),
        in_specs=[
            pl.BlockSpec((tm, d), lambda i, j: (i, 0)),
            pl.BlockSpec((1, d), row), pl.BlockSpec((1, d), row), pl.BlockSpec((1, d), row),
            pl.BlockSpec((d, PROJ_TN), lambda i, j: (0, j)),
            pl.BlockSpec((tm, LANES), lambda i, j: (i, 0)),
            pl.BlockSpec((tm, LANES), lambda i, j: (i, 0)),
        ],
        out_specs=[
            pl.BlockSpec((tm, PROJ_TN), lambda i, j: (i, jnp.clip(j, 0, nq - 1))),
            pl.BlockSpec((tm, PROJ_TN), lambda i, j: (i, jnp.clip(j - nq, 0, nq - 1))),
            pl.BlockSpec((PROJ_TN, tm), lambda i, j: (v_tile(j), i)),
            pl.BlockSpec((tm, PROJ_TN), lambda i, j: (i, jnp.clip(j - 2 * nq - _V_TILES, 0, _UV_TILES - 1))),
        ],
        out_shape=[
            jax.ShapeDtypeStruct((n, QK_COLS), BF16),
            jax.ShapeDtypeStruct((n, QK_COLS), BF16),
            jax.ShapeDtypeStruct((A_WIDTH, n), BF16),
            jax.ShapeDtypeStruct((n, 2 * B_WIDTH), F32),
        ],
        scratch_shapes=[pltpu.VMEM((tm, d), BF16)],
        compiler_params=_params(("arbitrary", "arbitrary")),
        name="in_projection",
    )(x, g, shift, scale, w_in_bf16, cos_t, sin_t)


def _rope_tables(n, rotate):
    if not rotate:
        return jnp.ones((n, LANES), F32), jnp.zeros((n, LANES), F32)
    t = np.arange(n)
    row = (t // GRID_W).astype(np.float32)
    col = (t % GRID_W).astype(np.float32)
    dim = A_HEAD_DIM // 2
    inv = (np.float32(ROPE_THETA) ** (-np.arange(0, dim, 2, dtype=np.float32) / np.float32(dim))).astype(np.float32)
    ang_r = row[:, None] * inv[None, :]
    ang_c = col[:, None] * inv[None, :]
    ang64 = np.concatenate([ang_r, ang_r, ang_c, ang_c], axis=1)
    sign64 = np.concatenate([-np.ones(16), np.ones(16), -np.ones(16), np.ones(16)]).astype(np.float32)
    ang = np.tile(ang64, (1, LANES // 64))
    sign = np.tile(sign64, LANES // 64)
    return jnp.asarray(np.cos(ang), F32), jnp.asarray(np.sin(ang) * sign[None, :], F32)


def _attn_kernel(*refs, n_parts, tq, tk, lam_init):
    q_ref = refs[0]
    k_parts = refs[1:1 + n_parts]
    vt_parts = refs[1 + n_parts:1 + 2 * n_parts]
    lamv_ref, gs_ref, o_ref, m_scr, acc_scr, qst_scr, sa_scr, sb_scr, k_ref, vt_ref, done_scr = refs[1 + 2 * n_parts:]
    n_blocks = q_ref.shape[0] // tq
    n_kv = k_ref.shape[0] // tk

    off = 0
    for k_part, vt_part in zip(k_parts, vt_parts):
        rows = k_part.shape[0]
        k_ref[off:off + rows, :] = k_part[...]
        vt_ref[:, off:off + rows] = vt_part[...]
        off += rows

    def query_operand(i):
        qt = q_ref[pl.ds(pl.multiple_of(i * tq, tq), tq), :].astype(F32).T
        row = lax.broadcasted_iota(I32, qt.shape, 0)
        zero = jnp.zeros_like(qt)
        return jnp.concatenate([jnp.where(row < A_HEAD_DIM, qt, zero),
                                jnp.where(row >= A_HEAD_DIM, qt, zero)], axis=1).astype(BF16)

    def scores(qst, j):
        off = pl.multiple_of(j * tk, tk)
        return jnp.dot(k_ref[pl.ds(off, tk), :], qst, preferred_element_type=F32)

    ones_rows = jnp.ones((16, tk), BF16)

    def consume(s, j):
        off = pl.multiple_of(j * tk, tk)
        vtb = jnp.concatenate([vt_ref[:, pl.ds(off, tk)], ones_rows], axis=0)
        m_old = m_scr[...]
        m_new = jnp.maximum(m_old, jnp.max(s, axis=0, keepdims=True))
        alpha = jnp.exp2(m_old - m_new)
        p = jnp.exp2(s - m_new).astype(BF16)
        acc_scr[...] = alpha * acc_scr[...] + jnp.dot(vtb, p, preferred_element_type=F32)
        m_scr[...] = m_new

    lv = lamv_ref[...]
    lam = (jnp.exp(jnp.sum(lv[0:1] * lv[1:2], axis=-1, keepdims=True))
           - jnp.exp(jnp.sum(lv[2:3] * lv[3:4], axis=-1, keepdims=True)) + lam_init)

    cross_block = n_kv % 2 == 0
    qst_scr[...] = query_operand(0)
    if cross_block:
        sa_scr[...] = scores(qst_scr[...], 0)
        done_scr[...] = jnp.ones(done_scr.shape, F32)

    def finish(acc, i):
        ot = acc[:A_V_DIM, :] / acc[A_V_DIM:A_V_DIM + 1, :]
        o = (ot[:, :tq] - lam * ot[:, tq:]).T
        a = o * lax.rsqrt(jnp.mean(o * o, axis=-1, keepdims=True) + EPS) * gs_ref[...]
        o_ref[pl.ds(pl.multiple_of(i * tq, tq), tq), :] = (a * (1.0 - lam_init)).astype(BF16)

    def block(i, carry):
        qst = qst_scr[...]
        m_scr[...] = jnp.full(m_scr.shape, -jnp.inf, F32)
        acc_scr[...] = jnp.zeros(acc_scr.shape, F32)
        if not cross_block:
            sa_scr[...] = scores(qst, 0)

        def pair(p, c):
            j = 2 * p
            sb_scr[...] = scores(qst, j + 1)
            consume(sa_scr[...], j)
            sa_scr[...] = scores(qst, j + 2)
            consume(sb_scr[...], j + 1)
            return c

        qst_next = query_operand(jnp.minimum(i + 1, n_blocks - 1))
        if cross_block:
            lax.fori_loop(0, n_kv // 2 - 1, pair, 0)
            sb_scr[...] = scores(qst, n_kv - 1)
            consume(sa_scr[...], n_kv - 2)
            sa_scr[...] = scores(qst_next, 0)
            finish(done_scr[...], jnp.maximum(i - 1, 0))
            consume(sb_scr[...], n_kv - 1)
            done_scr[...] = acc_scr[...]
        else:
            lax.fori_loop(0, (n_kv - 1) // 2, pair, 0)
            consume(sa_scr[...], n_kv - 1)
            finish(acc_scr[...], i)
        qst_scr[...] = qst_next
        return carry

    lax.fori_loop(0, n_blocks, block, 0)
    if cross_block:
        finish(done_scr[...], n_blocks - 1)


def _pick_tile(n, candidates):
    for c in candidates:
        if n % c == 0:
            return c
    raise ValueError(f"no tile for {n}")


def _diff_attention(q, k_parts, vt_parts, lamv, g_subln, lam_init):
    n = q.shape[0]
    nk = sum(k.shape[0] for k in k_parts)
    tq = min(512, n)
    tk = _pick_tile(nk, (1408, 768, 512, 256))
    score_buf = pltpu.VMEM((tk, 2 * tq), F32)
    return pl.pallas_call(
        functools.partial(_attn_kernel, n_parts=len(k_parts), tq=tq, tk=tk, lam_init=lam_init),
        grid=(A_HEADS,),
        in_specs=[pl.BlockSpec((n, A_V_DIM), lambda h: (0, h))]
        + [pl.BlockSpec((k.shape[0], A_V_DIM), lambda h: (0, h)) for k in k_parts]
        + [pl.BlockSpec((A_V_DIM, vt.shape[1]), lambda h: (h, 0)) for vt in vt_parts]
        + [pl.BlockSpec((8, LANES), lambda h: (0, 0)), pl.BlockSpec((1, A_V_DIM), lambda h: (0, 0))],
        out_specs=pl.BlockSpec((n, A_V_DIM), lambda h: (0, h)),
        out_shape=jax.ShapeDtypeStruct((n, A_WIDTH), BF16),
        scratch_shapes=[pltpu.VMEM((1, 2 * tq), F32),
                        pltpu.VMEM((A_V_DIM + 16, 2 * tq), F32),
                        pltpu.VMEM((A_V_DIM, 2 * tq), BF16),
                        score_buf, score_buf,
                        pltpu.VMEM((nk, A_V_DIM), BF16), pltpu.VMEM((A_V_DIM, nk), BF16),
                        pltpu.VMEM((A_V_DIM + 16, 2 * tq), F32)],
        compiler_params=_params(("arbitrary",)),
        name="diff_attention",
    )(q, *k_parts, *vt_parts, lamv, g_subln)


def _finish_kernel(a_ref, uv_ref, x_ref, wout_ref, ws_ref, bs_ref, lng_ref, lnb_ref, gm_ref, o_ref, cat_scr):
    tm = a_ref.shape[0]
    cat_scr[:, :A_WIDTH] = a_ref[...]
    for c in range(tm // CHUNK):
        rows = slice(c * CHUNK, (c + 1) * CHUNK)
        for g in range(B_GROUPS):
            cols = slice(g * B_GROUP_W, (g + 1) * B_GROUP_W)
            u = uv_ref[rows, g * B_GROUP_W:(g + 1) * B_GROUP_W]
            v = uv_ref[rows, B_WIDTH + g * B_GROUP_W:B_WIDTH + (g + 1) * B_GROUP_W]
            mu = jnp.mean(v, axis=-1, keepdims=True)
            var = jnp.mean(jnp.square(v - mu), axis=-1, keepdims=True)
            vn = (v - mu) * lax.rsqrt(var + EPS) * lng_ref[:, cols] + lnb_ref[:, cols]
            mixed = jnp.dot(ws_ref[g], vn.astype(BF16), preferred_element_type=F32) + bs_ref[g]
            cat_scr[rows, A_WIDTH + g * B_GROUP_W:A_WIDTH + (g + 1) * B_GROUP_W] = (u * mixed).astype(BF16)
    y = jnp.dot(cat_scr[...], wout_ref[...], preferred_element_type=F32)
    o_ref[...] = x_ref[...] + gm_ref[...] * y


def _finish_even(a, uv, x, w_out_bf16, ws_bf16, bs_b, ln_g, ln_b, gm):
    n, d = x.shape
    tm = min(512, n)
    row = lambda i: (0, 0)
    return pl.pallas_call(
        _finish_kernel,
        grid=(n // tm,),
        in_specs=[
            pl.BlockSpec((tm, A_WIDTH), lambda i: (i, 0)),
            pl.BlockSpec((tm, 2 * B_WIDTH), lambda i: (i, 0)),
            pl.BlockSpec((tm, d), lambda i: (i, 0)),
            pl.BlockSpec((A_WIDTH + B_WIDTH, d), row),
            pl.BlockSpec((B_GROUPS, CHUNK, CHUNK), lambda i: (0, 0, 0)),
            pl.BlockSpec((B_GROUPS, CHUNK, B_GROUP_W), lambda i: (0, 0, 0)),
            pl.BlockSpec((1, B_WIDTH), row), pl.BlockSpec((1, B_WIDTH), row),
            pl.BlockSpec((1, d), row),
        ],
        out_specs=pl.BlockSpec((tm, d), lambda i: (i, 0)),
        out_shape=jax.ShapeDtypeStruct((n, d), F32),
        scratch_shapes=[pltpu.VMEM((tm, A_WIDTH + B_WIDTH), BF16)],
        compiler_params=_params(("arbitrary",)),
        name="finish_even",
    )(a, uv, x, w_out_bf16, ws_bf16, bs_b, ln_g, ln_b, gm)


def _router_kernel(x_ref, g_ref, sh_ref, sc_ref, wrt_ref, ht_ref, aff_ref):
    h = _norm_mod(x_ref[...], g_ref[...], sh_ref[...], sc_ref[...])
    logits = _nt_dot(wrt_ref[...], h.astype(BF16))
    m = jnp.max(logits, axis=0, keepdims=True)
    e = jnp.exp(logits - m)
    aff_ref[...] = e / jnp.sum(e, axis=0, keepdims=True)
    ht_ref[...] = h.T.astype(BF16)


def _router(x, g, shift, scale, w_router_t_bf16):
    n, d = x.shape
    tm = min(512, n)
    row = lambda i: (0, 0)
    return pl.pallas_call(
        _router_kernel,
        grid=(n // tm,),
        in_specs=[
            pl.BlockSpec((tm, d), lambda i: (i, 0)),
            pl.BlockSpec((1, d), row), pl.BlockSpec((1, d), row), pl.BlockSpec((1, d), row),
            pl.BlockSpec((N_EXPERTS, d), row),
        ],
        out_specs=[pl.BlockSpec((d, tm), lambda i: (0, i)), pl.BlockSpec((N_EXPERTS, tm), lambda i: (0, i))],
        out_shape=[jax.ShapeDtypeStruct((d, n), BF16), jax.ShapeDtypeStruct((N_EXPERTS, n), F32)],
        compiler_params=_params(("arbitrary",)),
        name="router",
    )(x, g, shift, scale, w_router_t_bf16)


def _select_kernel(aff_ref, sel_ref, pos_ref, g_ref, first_ref, *, cap, capp):
    aff = aff_ref[...]
    n = aff.shape[1]
    idx = lax.broadcasted_iota(I32, aff.shape, 1)
    capf = float(cap)

    def count(mask):
        return jnp.sum(jnp.where(mask, 1.0, 0.0), axis=1, keepdims=True)

    def as_float(bits):
        return pltpu.bitcast(bits, F32)

    def thr_body(i, thr):
        cand = thr | jnp.left_shift(jnp.int32(1), 30 - i)
        return jnp.where(count(aff >= as_float(cand)) >= capf, cand, thr)

    thr = lax.fori_loop(0, 31, thr_body, jnp.zeros((aff.shape[0], 1), I32))
    gt = aff >= as_float(thr + 1)
    eq = (aff >= as_float(thr)) & jnp.logical_not(gt)
    need = capf - count(gt)
    nbits = int(n).bit_length()

    def tie_body(i, lim):
        cand = lim | jnp.left_shift(jnp.int32(1), nbits - 1 - i)
        return jnp.where(count(eq & (idx < cand)) <= need, cand, lim)

    lim = lax.fori_loop(0, nbits, tie_body, jnp.zeros((aff.shape[0], 1), I32))
    sel = jnp.where(gt | (eq & (idx < lim)), 1.0, 0.0)
    sel_ref[...] = sel
    g_ref[...] = aff * sel

    ri = lax.broadcasted_iota(I32, (LANES, LANES), 0)
    ci = lax.broadcasted_iota(I32, (LANES, LANES), 1)
    upper = jnp.where(ri < ci, 1.0, 0.0).astype(BF16)
    carry = jnp.zeros((aff.shape[0], 1), F32)
    for c in range(n // LANES):
        m = sel[:, c * LANES:(c + 1) * LANES]
        within = jnp.dot(m.astype(BF16), upper, preferred_element_type=F32)
        pos_ref[:, c * LANES:(c + 1) * LANES] = (within + carry).astype(I32)
        carry = carry + jnp.sum(m, axis=1, keepdims=True)

    incl = pos_ref[...].astype(F32) + sel
    lane = lax.broadcasted_iota(I32, first_ref.shape, 1)
    first = jnp.zeros(first_ref.shape, F32)
    for w in range(capp // ROUTE_WIN + 1):
        first = jnp.where(lane == w, count(incl <= float(w * ROUTE_WIN)), first)
    first_ref[...] = first.astype(I32)


def _select(aff_t, cap, capp):
    e, n = aff_t.shape
    full = lambda: (0, 0)
    return pl.pallas_call(
        functools.partial(_select_kernel, cap=cap, capp=capp),
        grid=(),
        in_specs=[pl.BlockSpec((e, n), full)],
        out_specs=[pl.BlockSpec((e, n), full)] * 3 + [pl.BlockSpec((e, LANES), full)],
        out_shape=[jax.ShapeDtypeStruct((e, n), F32), jax.ShapeDtypeStruct((e, n), I32),
                   jax.ShapeDtypeStruct((e, n), F32), jax.ShapeDtypeStruct((e, LANES), I32)],
        compiler_params=pltpu.CompilerParams(vmem_limit_bytes=VMEM_LIMIT_BYTES),
        name="expert_select",
    )(aff_t)


def _one_hot_rows(rel, win):
    r = lax.broadcasted_iota(I32, (win, rel.shape[1]), 0)
    return jnp.where(r == rel, 1.0, 0.0).astype(BF16)


def _one_hot_window(pos_row, sel_row, lo, base, win=ROUTE_WIN):
    rel = jnp.where((sel_row > 0.0) & (pos_row >= lo), pos_row - base, -1)
    return _one_hot_rows(rel, win)


def _window_plan(p0, p1, win=ROUTE_WIN):
    start = (p0 // LANES) * LANES
    n_chunks = jnp.where(p1 > p0, (p1 - start + win - 1) // win, 0)
    return start, n_chunks


def _dispatch_kernel(first_ref, ht_hbm, sel_ref, pos_ref, g_ref, xs_ref, gslot_ref, ht_scr, acc_scr, gs_scr, sem,
                     *, k_chunk):
    e = pl.program_id(0)
    w = pl.program_id(1)
    n = ht_scr.shape[1]

    @pl.when((e == 0) & (w == 0))
    def _():
        load = pltpu.make_async_copy(ht_hbm, ht_scr, sem)
        load.start()
        load.wait()

    t0 = first_ref[e, w]
    t1 = first_ref[e, w + 1]
    start = (t0 // ROUTE_TILE) * ROUTE_TILE
    n_chunks = jnp.where(t1 > t0, (t1 - start + k_chunk - 1) // k_chunk, 0)
    slot0 = w * ROUTE_WIN

    def chunk(k):
        lo_tok = start + k * k_chunk
        base = pl.multiple_of(jnp.minimum(lo_tok, n - k_chunk), ROUTE_TILE)
        toks = pl.ds(base, k_chunk)
        tok = base + lax.broadcasted_iota(I32, (1, k_chunk), 1)
        pos_row = pos_ref[0, :, toks]
        keep = (sel_ref[0, :, toks] > 0.0) & (tok >= lo_tok)
        rel = jnp.where(keep, pos_row - slot0, -1)
        onehot = _one_hot_rows(rel, ROUTE_WIN)
        g_row = g_ref[0, :, toks]
        g_hi = g_row.astype(BF16)
        r1 = g_row - g_hi.astype(F32)
        g_mid = r1.astype(BF16)
        g_lo = (r1 - g_mid.astype(F32)).astype(BF16)
        g8 = jnp.concatenate([g_hi, g_mid, g_lo, jnp.zeros((5, k_chunk), BF16)], axis=0)
        return _nt_dot(ht_scr[:, toks], onehot), _nt_dot(g8, onehot)

    acc_scr[...], gs_scr[...] = chunk(0)

    def extra(k, carry):
        x, gs = chunk(k)
        acc_scr[...] += x
        gs_scr[...] += gs
        return carry

    lax.fori_loop(1, n_chunks, extra, 0)
    xs_ref[0] = acc_scr[...].T.astype(BF16)
    gs = gs_scr[...]
    gslot_ref[0] = gs[0:1] + gs[1:2] + gs[2:3]


def _dispatch(first_tok, h_t, sel3, pos3, g3, capp):
    d, n = h_t.shape
    k_chunk = min(10 * ROUTE_TILE, n)
    row3 = pl.BlockSpec((1, 1, n), lambda e, w, ft: (e, 0, 0))
    return pl.pallas_call(
        functools.partial(_dispatch_kernel, k_chunk=k_chunk),
        grid_spec=pltpu.PrefetchScalarGridSpec(
            num_scalar_prefetch=1,
            grid=(N_EXPERTS, capp // ROUTE_WIN),
            in_specs=[pl.BlockSpec(memory_space=pl.ANY), row3, row3, row3],
            out_specs=[
                pl.BlockSpec((1, ROUTE_WIN, d), lambda e, w, ft: (e, w, 0)),
                pl.BlockSpec((1, 1, ROUTE_WIN), lambda e, w, ft: (e, 0, w)),
            ],
            scratch_shapes=[pltpu.VMEM((d, n), BF16), pltpu.VMEM((d, ROUTE_WIN), F32),
                            pltpu.VMEM((8, ROUTE_WIN), F32), pltpu.SemaphoreType.DMA],
        ),
        out_shape=[jax.ShapeDtypeStruct((N_EXPERTS, capp, d), BF16),
                   jax.ShapeDtypeStruct((N_EXPERTS, 1, capp), F32)],
        compiler_params=_params(("arbitrary", "arbitrary")),
        name="moe_dispatch",
    )(first_tok, h_t, sel3, pos3, g3)


def _ffn_up_kernel(xs_ref, wg_ref, wu_ref, h_ref):
    xs = xs_ref[0].astype(BF16)
    a = jnp.dot(xs, wg_ref[0].astype(BF16), preferred_element_type=F32)
    u = jnp.dot(xs, wu_ref[0].astype(BF16), preferred_element_type=F32)
    h_ref[0] = (a * jax.nn.sigmoid(a) * u).astype(BF16)


def _ffn_up(xs, w_gate, w_up, layer):
    e, capp, d = xs.shape
    f = w_gate.shape[3]
    tf = 512
    return pl.pallas_call(
        _ffn_up_kernel,
        grid=(e, f // tf),
        in_specs=[
            pl.BlockSpec((1, capp, d), lambda i, j: (i, 0, 0)),
            pl.BlockSpec((None, 1, d, tf), lambda i, j: (layer, i, 0, j)),
            pl.BlockSpec((None, 1, d, tf), lambda i, j: (layer, i, 0, j)),
        ],
        out_specs=pl.BlockSpec((1, capp, tf), lambda i, j: (i, 0, j)),
        out_shape=jax.ShapeDtypeStruct((e, capp, f), BF16),
        compiler_params=_params(("arbitrary", "arbitrary")),
        name="moe_ffn_up",
    )(xs, w_gate, w_up)


def _ffn_down_kernel(h_ref, wd_ref, gslot_ref, y_ref):
    y = jnp.dot(h_ref[0], wd_ref[0].astype(BF16), preferred_element_type=F32)
    y_ref[0] = (y.T * gslot_ref[0]).astype(BF16)


def _ffn_down(h, w_down, gslot, layer):
    e, capp, f = h.shape
    d = w_down.shape[3]
    td = 2048
    return pl.pallas_call(
        _ffn_down_kernel,
        grid=(e, d // td),
        in_specs=[
            pl.BlockSpec((1, capp, f), lambda i, j: (i, 0, 0)),
            pl.BlockSpec((None, 1, f, td), lambda i, j: (layer, i, 0, j)),
            pl.BlockSpec((1, 1, capp), lambda i, j: (i, 0, 0)),
        ],
        out_specs=pl.BlockSpec((1, td, capp), lambda i, j: (i, j, 0)),
        out_shape=jax.ShapeDtypeStruct((e, d, capp), BF16),
        compiler_params=_params(("arbitrary", "arbitrary")),
        name="moe_ffn_down",
    )(h, w_down, gslot)


COMBINE_TD = 512
COMBINE_WIN = ROUTE_WIN


def _combine_kernel(ps_ref, pe_ref, y_ref, sel_ref, pos_ref, x_ref, gate_ref, o_ref, acc_scr):
    t = pl.program_id(1)
    n_exp, _, capp = y_ref.shape

    def window(e, k):
        start, _ = _window_plan(ps_ref[e, t], pe_ref[e, t], COMBINE_WIN)
        lo = start + k * COMBINE_WIN
        base = pl.multiple_of(jnp.minimum(lo, capp - COMBINE_WIN), LANES)
        onehot = _one_hot_window(pos_ref[e, pl.ds(t, 1), :], sel_ref[e, pl.ds(t, 1), :], lo, base, COMBINE_WIN)
        return [y_ref[e, :, pl.ds(base, COMBINE_WIN)]], [onehot]

    lhs, rhs = [], []
    for e in range(n_exp):
        rows, hots = window(e, 0)
        lhs += rows
        rhs += hots
    acc_scr[...] = jnp.dot(jnp.concatenate(lhs, axis=1), jnp.concatenate(rhs, axis=0), preferred_element_type=F32)

    for e in range(n_exp):
        n_chunks = _window_plan(ps_ref[e, t], pe_ref[e, t], COMBINE_WIN)[1]

        def extra(k, carry, e=e):
            rows, hots = window(e, k)
            acc_scr[...] += jnp.dot(jnp.concatenate(rows, axis=1), jnp.concatenate(hots, axis=0),
                                    preferred_element_type=F32)
            return carry

        lax.fori_loop(1, n_chunks, extra, 0)

    o_ref[...] = x_ref[...] + gate_ref[...] * acc_scr[...].T


def _combine(tile_start, tile_end, y_t, sel_t, pos_t, x, gate):
    e, d, capp = y_t.shape
    n = x.shape[0]
    n_tiles = n // ROUTE_TILE
    td = COMBINE_TD
    y_blk = pl.BlockSpec((e, td, capp), lambda j, t, ps, pe: (0, j, 0))
    full = pl.BlockSpec((e, n_tiles, ROUTE_TILE), lambda j, t, ps, pe: (0, 0, 0))
    tile = pl.BlockSpec((ROUTE_TILE, td), lambda j, t, ps, pe: (t, j))
    return pl.pallas_call(
        _combine_kernel,
        grid_spec=pltpu.PrefetchScalarGridSpec(
            num_scalar_prefetch=2,
            grid=(d // td, n_tiles),
            in_specs=[y_blk, full, full, tile, pl.BlockSpec((1, td), lambda j, t, ps, pe: (0, j))],
            out_specs=tile,
            scratch_shapes=[pltpu.VMEM((td, ROUTE_TILE), F32)],
        ),
        out_shape=jax.ShapeDtypeStruct((n, d), F32),
        compiler_params=_params(("arbitrary", "arbitrary")),
        name="moe_combine",
    )(tile_start, tile_end, y_t, sel_t, pos_t, x, gate)


def _tile_bounds(pos, tile, cap):
    start = pos[:, ::tile]
    end = jnp.concatenate([start[:, 1:], jnp.full((pos.shape[0], 1), cap, I32)], axis=1)
    return start, end


def _expert_choice_ffn(x, g, shift, scale, gate, w_router, w_gate, w_up, w_down, layer):
    n, d = x.shape
    cap = CAPACITY_FACTOR * n // N_EXPERTS
    capp = max(cap, ROUTE_WIN)
    h_t, aff_t = _router(x, g, shift, scale, w_router.T.astype(BF16))
    sel, pos, gsel, first_tok = _select(aff_t, cap, capp)
    as3 = lambda a: a.reshape(N_EXPERTS, 1, n)
    xs, gslot = _dispatch(first_tok, h_t, as3(sel), as3(pos), as3(gsel), capp)
    hidden = _ffn_up(xs, w_gate, w_up, layer)
    y_t = _ffn_down(hidden, w_down, gslot, layer)
    as_tiles = lambda a: a.reshape(N_EXPERTS, n // ROUTE_TILE, ROUTE_TILE)
    return _combine(*_tile_bounds(pos, ROUTE_TILE, cap), y_t, as_tiles(sel), as_tiles(pos), x, gate)


def _final_kernel(x_ref, g_ref, o_ref):
    x = x_ref[...]
    o_ref[...] = x * lax.rsqrt(jnp.mean(x * x, axis=-1, keepdims=True) + EPS) * g_ref[...]


def _final_norm(x, g_final):
    n, d = x.shape
    tm = min(512, n)
    return pl.pallas_call(
        _final_kernel,
        grid=(n // tm,),
        in_specs=[pl.BlockSpec((tm, d), lambda i: (i, 0)), pl.BlockSpec((1, d), lambda i: (0, 0))],
        out_specs=pl.BlockSpec((tm, d), lambda i: (i, 0)),
        out_shape=jax.ShapeDtypeStruct((n, d), F32),
        compiler_params=_params(("arbitrary",)),
        name="final_norm",
    )(x, g_final)


FFT_K1_BLK = 16


def _fourier_in_kernel(x_ref, g_ref, sh_ref, sc_ref, cs_ref, perm_ref, a_ref, b_ref):
    h = _norm_mod(x_ref[...], g_ref[...], sh_ref[...], sc_ref[...]).astype(BF16)
    hp = jnp.dot(perm_ref[...], h, preferred_element_type=F32).astype(BF16)
    n2, t1_blk = a_ref.shape[0], a_ref.shape[1]
    for g in range(C_GROUPS):
        cols = slice(g * C_GROUP_W, (g + 1) * C_GROUP_W)
        ab = jnp.dot(hp[:, cols], cs_ref[...], preferred_element_type=F32)
        a_ref[:, :, cols] = ab[:, :C_GROUP_W].astype(BF16).reshape(n2, t1_blk, C_GROUP_W)
        b_ref[:, :, cols] = ab[:, C_GROUP_W:].astype(BF16).reshape(n2, t1_blk, C_GROUP_W)


def _fourier_in(x, g, shift, scale, cs):
    n, d = x.shape
    tm = 512
    n1 = n // FFT_N2
    t1_blk = tm // FFT_N2
    row = lambda i: (0, 0)
    tile = pl.BlockSpec((tm, d), lambda i: (i, 0))
    ab_blk = pl.BlockSpec((FFT_N2, t1_blk, d), lambda i: (0, i, 0))
    ab_shape = jax.ShapeDtypeStruct((FFT_N2, n1, d), BF16)
    perm = np.zeros((tm, tm), np.float32)
    src = np.arange(tm)
    perm[(src % FFT_N2) * t1_blk + src // FFT_N2, src] = 1.0
    return pl.pallas_call(
        _fourier_in_kernel,
        grid=(n // tm,),
        in_specs=[tile, pl.BlockSpec((1, d), row), pl.BlockSpec((1, d), row), pl.BlockSpec((1, d), row),
                  pl.BlockSpec((C_GROUP_W, 2 * C_GROUP_W), row), pl.BlockSpec((tm, tm), row)],
        out_specs=[ab_blk, ab_blk],
        out_shape=[ab_shape, ab_shape],
        compiler_params=_params(("arbitrary",)),
        name="fourier_channel_dft",
    )(x, g, shift, scale, cs, jnp.asarray(perm, BF16))


def _fourier_stage1_kernel(a_ref, b_ref, ma_ref, mb_ref, ct_ref, st_ref, zr_ref, zi_ref):
    n1 = a_ref.shape[1]
    z = (jnp.dot(ma_ref[...], a_ref[0], preferred_element_type=F32)
         + jnp.dot(mb_ref[...], b_ref[0], preferred_element_type=F32))
    ct = ct_ref[0]
    st = st_ref[0]
    for c in range(a_ref.shape[2] // LANES):
        cols = slice(c * LANES, (c + 1) * LANES)
        zr = z[:n1, cols]
        zi = z[n1:, cols]
        zr_ref[0, :, cols] = (zr * ct + zi * st).astype(BF16)
        zi_ref[0, :, cols] = (zi * ct - zr * st).astype(BF16)


def _fourier_stage1(a3, b3, ma, mb, ct, st):
    n2, n1, d = a3.shape
    blk = pl.BlockSpec((1, n1, d), lambda j: (j, 0, 0))
    mat = pl.BlockSpec((2 * n1, n1), lambda j: (0, 0))
    tw = pl.BlockSpec((1, n1, LANES), lambda j: (j, 0, 0))
    shp = jax.ShapeDtypeStruct((n2, n1, d), BF16)
    return pl.pallas_call(
        _fourier_stage1_kernel,
        grid=(n2,),
        in_specs=[blk, blk, mat, mat, tw, tw],
        out_specs=[blk, blk],
        out_shape=[shp, shp],
        compiler_params=_params(("arbitrary",)),
        name="fourier_stage1",
    )(a3, b3, ma, mb, ct, st)


def _fourier_out_kernel(zr_ref, zi_ref, bc_ref, bs_ref, wo_ref, x_ref, gm_ref, o_ref):
    rows = zr_ref.shape[0] * zr_ref.shape[1]
    d = zr_ref.shape[2]
    zr = zr_ref[...].reshape(rows, d)
    zi = zi_ref[...].reshape(rows, d)
    f = (jnp.dot(bc_ref[...], zr, preferred_element_type=F32)
         + jnp.dot(bs_ref[...], zi, preferred_element_type=F32))
    y = jnp.dot(f.astype(BF16), wo_ref[...], preferred_element_type=F32)
    o_ref[...] = x_ref[...] + gm_ref[...] * y.reshape(o_ref.shape)


def _fourier_out(zr3, zi3, bd_c, bd_s, w_o_bf16, x, gm):
    n2, n1, d = zr3.shape
    n = n1 * n2
    rows = n2 * FFT_K1_BLK
    x3 = x.reshape(n2, n1, d)
    blk = pl.BlockSpec((n2, FFT_K1_BLK, d), lambda i: (0, i, 0))
    const = lambda i: (0, 0)
    out = pl.pallas_call(
        _fourier_out_kernel,
        grid=(n1 // FFT_K1_BLK,),
        in_specs=[blk, blk, pl.BlockSpec((rows, rows), const), pl.BlockSpec((rows, rows), const),
                  pl.BlockSpec((d, d), const), blk, pl.BlockSpec((1, d), const)],
        out_specs=blk,
        out_shape=jax.ShapeDtypeStruct((n2, n1, d), F32),
        compiler_params=_params(("arbitrary",)),
        name="fourier_stage2_out",
    )(zr3, zi3, bd_c, bd_s, w_o_bf16, x3, gm)
    return out.reshape(n, d)


def _fourier_constants(n):
    n1, n2 = n // FFT_N2, FFT_N2
    two_pi = 2.0 * np.pi

    def angles(a, b, period):
        return two_pi * ((np.outer(a, b) % period).astype(np.float64) / period)

    kc = np.arange(C_GROUP_W)
    ang = angles(kc, kc, C_GROUP_W)
    cs = np.concatenate([np.cos(ang), np.sin(ang)], axis=1) / np.sqrt(C_GROUP_W)
    k1 = np.arange(n1)
    ang1 = angles(k1, k1, n1)
    c1, s1 = np.cos(ang1) / np.sqrt(n), np.sin(ang1) / np.sqrt(n)
    ma = np.concatenate([c1, -s1], axis=0)
    mb = np.concatenate([-s1, -c1], axis=0)
    t2 = np.arange(n2)
    angt = angles(t2, k1, n)
    ct = np.repeat(np.cos(angt)[:, :, None], LANES, axis=2)
    st = np.repeat(np.sin(angt)[:, :, None], LANES, axis=2)
    ang2 = angles(t2, t2, n2)
    k1_blk = FFT_K1_BLK
    bd_c = np.zeros((n2 * k1_blk, n2 * k1_blk))
    bd_s = np.zeros((n2 * k1_blk, n2 * k1_blk))
    for kl in range(k1_blk):
        bd_c[kl::k1_blk, kl::k1_blk] = np.cos(ang2)
        bd_s[kl::k1_blk, kl::k1_blk] = np.sin(ang2)
    bf = lambda a: jnp.asarray(a, F32).astype(BF16)
    return bf(cs), bf(ma), bf(mb), jnp.asarray(ct, F32), jnp.asarray(st, F32), bf(bd_c), bf(bd_s)


def _fourier_mix_layer(x, g, shift, scale, gm, w_o):
    cs, ma, mb, ct, st, bd_c, bd_s = _fourier_constants(x.shape[0])
    a3, b3 = _fourier_in(x, g, shift, scale, cs)
    zr3, zi3 = _fourier_stage1(a3, b3, ma, mb, ct, st)
    return _fourier_out(zr3, zi3, bd_c, bd_s, w_o.astype(BF16), x, gm)


def _even_layer_mix(x, ctx, mods, g_mix, w_in, w_out, lamv, g_subln, ln_g, ln_b, w_s, b_s, lam_init, need_ctx_out):
    sm, cm, gm = mods["lat"][0:3]
    smc, cmc, gmc = mods["ctx"][0:3]
    n = x.shape[0]
    w_in_b = w_in.astype(BF16)
    w_out_b = w_out.astype(BF16)
    ws_b = w_s.astype(BF16)
    bs_b = jnp.broadcast_to(b_s[:, :, None], (B_GROUPS, CHUNK, B_GROUP_W))
    cos_l, sin_l = _rope_tables(n, True)
    cos_c, sin_c = _rope_tables(ctx.shape[0], False)
    q_l, k_l, vt_l, uv_l = _in_projection(x, g_mix, sm, cm, w_in_b, cos_l, sin_l)
    q_c, k_c, vt_c, uv_c = _in_projection(ctx, g_mix, smc, cmc, w_in_b, cos_c, sin_c)
    a_l = _diff_attention(q_l, [k_c, k_l], [vt_c, vt_l], lamv, g_subln, lam_init)
    x = _finish_even(a_l, uv_l, x, w_out_b, ws_b, bs_b, ln_g, ln_b, gm)
    if need_ctx_out:
        a_c = _diff_attention(q_c, [k_c], [vt_c], lamv, g_subln, lam_init)
        ctx = _finish_even(a_c, uv_c, ctx, w_out_b, ws_b, bs_b, ln_g, ln_b, gmc)
    return x, ctx


def kernel(x, c, ctx, c_ctx, w_mod, b_mod, g_norm_mix, g_norm_ffn, w_in, w_out, lam_q1, lam_k1, lam_q2, lam_k2,
           g_subln, sgu_ln_g, sgu_ln_b, w_spatial, b_spatial, w_fourier_out, w_router, w_gate, w_up, w_down, g_final):
    assert x.shape[0] == 1 and DEPTH == 2
    d = D_MODEL
    x2 = x[0]
    ctx2 = ctx[0]
    c8 = jnp.zeros((8, d), F32).at[0].set(c[0]).at[1].set(c_ctx)
    mod_all = _modulation(c8, w_mod, b_mod)
    row = lambda v: v.reshape(1, -1)

    def mods_of(i):
        lat = [mod_all[i, 0:1, k * d:(k + 1) * d] for k in range(6)]
        cx = [mod_all[i, 1:2, k * d:(k + 1) * d] for k in range(6)]
        return {"lat": lat, "ctx": cx}

    m0 = mods_of(0)
    lam_init0 = 0.8 - 0.6 * math.exp(-0.3 * 0)
    lamv = jnp.zeros((8, LANES), F32)
    for r, v in enumerate((lam_q1[0], lam_k1[0], lam_q2[0], lam_k2[0])):
        lamv = lamv.at[r, :A_HEAD_DIM].set(v)
    x2, ctx2 = _even_layer_mix(x2, ctx2, m0, row(g_norm_mix[0]), w_in[0], w_out[0], lamv, row(g_subln[0]),
                               row(sgu_ln_g[0]), row(sgu_ln_b[0]), w_spatial[0], b_spatial[0], lam_init0, True)
    x2 = _expert_choice_ffn(x2, row(g_norm_ffn[0]), m0["lat"][3], m0["lat"][4], m0["lat"][5],
                            w_router[0], w_gate, w_up, w_down, 0)
    ctx2 = _expert_choice_ffn(ctx2, row(g_norm_ffn[0]), m0["ctx"][3], m0["ctx"][4], m0["ctx"][5],
                              w_router[0], w_gate, w_up, w_down, 0)

    m1 = mods_of(1)
    x2 = _fourier_mix_layer(x2, row(g_norm_mix[1]), m1["lat"][0], m1["lat"][1], m1["lat"][2], w_fourier_out[0])
    x2 = _expert_choice_ffn(x2, row(g_norm_ffn[1]), m1["lat"][3], m1["lat"][4], m1["lat"][5],
                            w_router[1], w_gate, w_up, w_down, 1)
    out = _final_norm(x2, row(g_final))
    del ctx2
    return out[None]
```

```python
import functools
import math

import numpy as np
import jax
import jax.numpy as jnp
from jax import lax
from jax.experimental import pallas as pl
from jax.experimental.pallas import tpu as pltpu

F32 = jnp.float32
BF16 = jnp.bfloat16
I32 = jnp.int32

D_MODEL = 2048
DEPTH = 2
GRID_W = 64
EPS = 1e-6

A_HEADS = 8
A_HEAD_DIM = 64
A_V_DIM = 2 * A_HEAD_DIM
A_WIDTH = A_HEADS * A_V_DIM
QK_COLS = A_HEADS * 2 * A_HEAD_DIM
ROPE_THETA = 10000.0

B_GROUPS = 8
B_GROUP_W = 128
B_WIDTH = B_GROUPS * B_GROUP_W
CHUNK = 128
IN_COLS = 2 * QK_COLS + A_WIDTH + 2 * B_WIDTH

C_GROUPS = 4
C_GROUP_W = D_MODEL // C_GROUPS

N_EXPERTS = 16
CAPACITY_FACTOR = 2
F_EXPERT = D_MODEL // 2

LANES = 128
MXU_DIM = 256
VMEM_LIMIT_BYTES = 56 * 1024 * 1024

FFT_N2 = 32
ROUTE_TILE = 256
ROUTE_WIN = 256


def _params(sem):
    return pltpu.CompilerParams(dimension_semantics=sem, vmem_limit_bytes=VMEM_LIMIT_BYTES)


def _nt_dot(a, b):
    return lax.dot_general(a, b, (((1,), (1,)), ((), ())), preferred_element_type=F32)


def _norm_mod(x, g, shift, scale):
    ms = jnp.mean(x * x, axis=-1, keepdims=True)
    y = x * lax.rsqrt(ms + EPS) * g
    return y * (1.0 + scale) + shift


def _mod_kernel(c_ref, w_ref, b_ref, o_ref):
    c = c_ref[...]
    s = c * jax.nn.sigmoid(c)
    o_ref[...] = jnp.dot(s.astype(BF16), w_ref[...].astype(BF16), preferred_element_type=F32) + b_ref[...]


def _modulation(c8, w_mod, b_mod):
    depth, d, n6 = w_mod.shape
    tn = 1024
    return pl.pallas_call(
        _mod_kernel,
        grid=(depth, n6 // tn),
        in_specs=[
            pl.BlockSpec((8, d), lambda l, j: (0, 0)),
            pl.BlockSpec((None, d, tn), lambda l, j: (l, 0, j)),
            pl.BlockSpec((None, 1, tn), lambda l, j: (l, 0, j)),
        ],
        out_specs=pl.BlockSpec((None, 8, tn), lambda l, j: (l, 0, j)),
        out_shape=jax.ShapeDtypeStruct((depth, 8, n6), F32),
        compiler_params=_params(("arbitrary", "arbitrary")),
        name="modulation",
    )(c8, w_mod, b_mod.reshape(depth, 1, n6))


PROJ_TN = 1024
PROJ_ROW_PARTS = 4
_Q_TILES = QK_COLS // PROJ_TN
_V_TILES = A_WIDTH // PROJ_TN
_UV_TILES = 2 * B_WIDTH // PROJ_TN


def _rope_chunk(z, cos, sin_signed, first_half):
    partner = jnp.where(first_half, pltpu.roll(z, LANES - 16, 1), pltpu.roll(z, 16, 1))
    return z * cos + partner * sin_signed


def _proj_kernel(x_ref, g_ref, sh_ref, sc_ref, w_ref, cos_ref, sin_ref,
                 q_ref, k_ref, vt_ref, uv_ref, h_scr):
    j = pl.program_id(1)

    tm = h_scr.shape[0]

    @pl.when(j == 0)
    def _():
        for r in range(tm // CHUNK):
            rows = slice(r * CHUNK, (r + 1) * CHUNK)
            h_scr[rows, :] = _norm_mod(x_ref[rows, :], g_ref[...], sh_ref[...], sc_ref[...]).astype(BF16)

    n_parts = min(PROJ_ROW_PARTS, tm // LANES)
    parts = [slice(r * (tm // n_parts), (r + 1) * (tm // n_parts)) for r in range(n_parts)]

    def project(rows):
        return jnp.dot(h_scr[rows, :], w_ref[...], preferred_element_type=F32)

    def roped(rows, scale):
        z = project(rows)
        cos = cos_ref[rows, :]
        sin = sin_ref[rows, :]
        lane = lax.broadcasted_iota(I32, cos.shape, 1)
        first_half = (lane % 32) < 16
        chunks = []
        for c in range(PROJ_TN // LANES):
            zc = z[:, c * LANES:(c + 1) * LANES]
            chunks.append(_rope_chunk(zc, cos, sin, first_half) * scale)
        return jnp.concatenate(chunks, axis=1)

    @pl.when(j < _Q_TILES)
    def _():
        for rows in parts:
            q_ref[rows, :] = roped(rows, A_HEAD_DIM ** -0.5 * math.log2(math.e)).astype(BF16)

    @pl.when((j >= _Q_TILES) & (j < 2 * _Q_TILES))
    def _():
        for rows in parts:
            k_ref[rows, :] = roped(rows, 1.0).astype(BF16)

    @pl.when((j >= 2 * _Q_TILES) & (j < 2 * _Q_TILES + _V_TILES))
    def _():
        for rows in parts:
            vt_ref[:, rows] = project(rows).T.astype(BF16)

    @pl.when(j >= 2 * _Q_TILES + _V_TILES)
    def _():
        for rows in parts:
            uv_ref[rows, :] = jax.nn.gelu(project(rows))


def _in_projection(x, g, shift, scale, w_in_bf16, cos_t, sin_t):
    n, d = x.shape
    tm = min(1024, n)
    nq = _Q_TILES
    nj = IN_COLS // PROJ_TN
    row = lambda i, j: (0, 0)
    v_tile = lambda j: jnp.clip(j - 2 * nq, 0, _V_TILES - 1)
    return pl.pallas_call(
        _proj_kernel,
        grid=(n // tm, nj),
        in_specs=[
            pl.BlockSpec((tm, d), lambda i, j: (i, 0)),
            pl.BlockSpec((1, d), row), pl.BlockSpec((1, d), row), pl.BlockSpec((1, d), row),
            pl.BlockSpec((d, PROJ_TN), lambda i, j: (0, j)),
            pl.BlockSpec((tm, LANES), lambda i, j: (i, 0)),
            pl.BlockSpec((tm, LANES), lambda i, j: (i, 0)),
        ],
        out_specs=[
            pl.BlockSpec((tm, PROJ_TN), lambda i, j: (i, jnp.clip(j, 0, nq - 1))),
            pl.BlockSpec((tm, PROJ_TN), lambda i, j: (i, jnp.clip(j - nq, 0, nq - 1))),
            pl.BlockSpec((PROJ_TN, tm), lambda i, j: (v_tile(j), i)),
            pl.BlockSpec((tm, PROJ_TN), lambda i, j: (i, jnp.clip(j - 2 * nq - _V_TILES, 0, _UV_TILES - 1))),
        ],
        out_shape=[
            jax.ShapeDtypeStruct((n, QK_COLS), BF16),
            jax.ShapeDtypeStruct((n, QK_COLS), BF16),
            jax.ShapeDtypeStruct((A_WIDTH, n), BF16),
            jax.ShapeDtypeStruct((n, 2 * B_WIDTH), F32),
        ],
        scratch_shapes=[pltpu.VMEM((tm, d), BF16)],
        compiler_params=_params(("arbitrary", "arbitrary")),
        name="in_projection",
    )(x, g, shift, scale, w_in_bf16, cos_t, sin_t)


def _rope_tables(n, rotate):
    if not rotate:
        return jnp.ones((n, LANES), F32), jnp.zeros((n, LANES), F32)
    t = np.arange(n)
    row = (t // GRID_W).astype(np.float32)
    col = (t % GRID_W).astype(np.float32)
    dim = A_HEAD_DIM // 2
    inv = (np.float32(ROPE_THETA) ** (-np.arange(0, dim, 2, dtype=np.float32) / np.float32(dim))).astype(np.float32)
    ang_r = row[:, None] * inv[None, :]
    ang_c = col[:, None] * inv[None, :]
    ang64 = np.concatenate([ang_r, ang_r, ang_c, ang_c], axis=1)
    sign64 = np.concatenate([-np.ones(16), np.ones(16), -np.ones(16), np.ones(16)]).astype(np.float32)
    ang = np.tile(ang64, (1, LANES // 64))
    sign = np.tile(sign64, LANES // 64)
    return jnp.asarray(np.cos(ang), F32), jnp.asarray(np.sin(ang) * sign[None, :], F32)


def _attn_kernel(*refs, n_parts, tq, tk, lam_init):
    q_ref = refs[0]
    k_parts = refs[1:1 + n_parts]
    vt_parts = refs[1 + n_parts:1 + 2 * n_parts]
    lamv_ref, gs_ref, o_ref, m_scr, acc_scr, qst_scr, sa_scr, sb_scr, k_ref, vt_ref, done_scr = refs[1 + 2 * n_parts:]
    n_blocks = q_ref.shape[0] // tq
    n_kv = k_ref.shape[0] // tk

    off = 0
    for k_part, vt_part in zip(k_parts, vt_parts):
        rows = k_part.shape[0]
        k_ref[off:off + rows, :] = k_part[...]
        vt_ref[:, off:off + rows] = vt_part[...]
        off += rows

    def query_operand(i):
        qt = q_ref[pl.ds(pl.multiple_of(i * tq, tq), tq), :].astype(F32).T
        row = lax.broadcasted_iota(I32, qt.shape, 0)
        zero = jnp.zeros_like(qt)
        return jnp.concatenate([jnp.where(row < A_HEAD_DIM, qt, zero),
                                jnp.where(row >= A_HEAD_DIM, qt, zero)], axis=1).astype(BF16)

    def scores(qst, j):
        off = pl.multiple_of(j * tk, tk)
        return jnp.dot(k_ref[pl.ds(off, tk), :], qst, preferred_element_type=F32)

    ones_rows = jnp.ones((16, tk), BF16)

    def consume(s, j):
        off = pl.multiple_of(j * tk, tk)
        vtb = jnp.concatenate([vt_ref[:, pl.ds(off, tk)], ones_rows], axis=0)
        m_old = m_scr[...]
        m_new = jnp.maximum(m_old, jnp.max(s, axis=0, keepdims=True))
        alpha = jnp.exp2(m_old - m_new)
        p = jnp.exp2(s - m_new).astype(BF16)
        acc_scr[...] = alpha * acc_scr[...] + jnp.dot(vtb, p, preferred_element_type=F32)
        m_scr[...] = m_new

    lv = lamv_ref[...]
    lam = (jnp.exp(jnp.sum(lv[0:1] * lv[1:2], axis=-1, keepdims=True))
           - jnp.exp(jnp.sum(lv[2:3] * lv[3:4], axis=-1, keepdims=True)) + lam_init)

    cross_block = n_kv % 2 == 0
    qst_scr[...] = query_operand(0)
    if cross_block:
        sa_scr[...] = scores(qst_scr[...], 0)
        done_scr[...] = jnp.ones(done_scr.shape, F32)

    def finish(acc, i):
        ot = acc[:A_V_DIM, :] / acc[A_V_DIM:A_V_DIM + 1, :]
        o = (ot[:, :tq] - lam * ot[:, tq:]).T
        a = o * lax.rsqrt(jnp.mean(o * o, axis=-1, keepdims=True) + EPS) * gs_ref[...]
        o_ref[pl.ds(pl.multiple_of(i * tq, tq), tq), :] = (a * (1.0 - lam_init)).astype(BF16)

    def block(i, carry):
        qst = qst_scr[...]
        m_scr[...] = jnp.full(m_scr.shape, -jnp.inf, F32)
        acc_scr[...] = jnp.zeros(acc_scr.shape, F32)
        if not cross_block:
            sa_scr[...] = scores(qst, 0)

        def pair(p, c):
            j = 2 * p
            sb_scr[...] = scores(qst, j + 1)
            consume(sa_scr[...], j)
            sa_scr[...] = scores(qst, j + 2)
            consume(sb_scr[...], j + 1)
            return c

        qst_next = query_operand(jnp.minimum(i + 1, n_blocks - 1))
        if cross_block:
            lax.fori_loop(0, n_kv // 2 - 1, pair, 0)
            sb_scr[...] = scores(qst, n_kv - 1)
            consume(sa_scr[...], n_kv - 2)
            sa_scr[...] = scores(qst_next, 0)
            finish(done_scr[...], jnp.maximum(i - 1, 0))
            consume(sb_scr[...], n_kv - 1)
            done_scr[...] = acc_scr[...]
        else:
            lax.fori_loop(0, (n_kv - 1) // 2, pair, 0)
            consume(sa_scr[...], n_kv - 1)
            finish(acc_scr[...], i)
        qst_scr[...] = qst_next
        return carry

    lax.fori_loop(0, n_blocks, block, 0)
    if cross_block:
        finish(done_scr[...], n_blocks - 1)


def _pick_tile(n, candidates):
    for c in candidates:
        if n % c == 0:
            return c
    raise ValueError(f"no tile for {n}")


def _diff_attention(q, k_parts, vt_parts, lamv, g_subln, lam_init):
    n = q.shape[0]
    nk = sum(k.shape[0] for k in k_parts)
    tq = min(512, n)
    tk = _pick_tile(nk, (1408, 768, 512, 256))
    score_buf = pltpu.VMEM((tk, 2 * tq), F32)
    return pl.pallas_call(
        functools.partial(_attn_kernel, n_parts=len(k_parts), tq=tq, tk=tk, lam_init=lam_init),
        grid=(A_HEADS,),
        in_specs=[pl.BlockSpec((n, A_V_DIM), lambda h: (0, h))]
        + [pl.BlockSpec((k.shape[0], A_V_DIM), lambda h: (0, h)) for k in k_parts]
        + [pl.BlockSpec((A_V_DIM, vt.shape[1]), lambda h: (h, 0)) for vt in vt_parts]
        + [pl.BlockSpec((8, LANES), lambda h: (0, 0)), pl.BlockSpec((1, A_V_DIM), lambda h: (0, 0))],
        out_specs=pl.BlockSpec((n, A_V_DIM), lambda h: (0, h)),
        out_shape=jax.ShapeDtypeStruct((n, A_WIDTH), BF16),
        scratch_shapes=[pltpu.VMEM((1, 2 * tq), F32),
                        pltpu.VMEM((A_V_DIM + 16, 2 * tq), F32),
                        pltpu.VMEM((A_V_DIM, 2 * tq), BF16),
                        score_buf, score_buf,
                        pltpu.VMEM((nk, A_V_DIM), BF16), pltpu.VMEM((A_V_DIM, nk), BF16),
                        pltpu.VMEM((A_V_DIM + 16, 2 * tq), F32)],
        compiler_params=_params(("arbitrary",)),
        name="diff_attention",
    )(q, *k_parts, *vt_parts, lamv, g_subln)


def _finish_kernel(a_ref, uv_ref, x_ref, wout_ref, ws_ref, bs_ref, lng_ref, lnb_ref, gm_ref, o_ref, cat_scr):
    tm = a_ref.shape[0]
    cat_scr[:, :A_WIDTH] = a_ref[...]
    for c in range(tm // CHUNK):
        rows = slice(c * CHUNK, (c + 1) * CHUNK)
        for g in range(B_GROUPS):
            cols = slice(g * B_GROUP_W, (g + 1) * B_GROUP_W)
            u = uv_ref[rows, g * B_GROUP_W:(g + 1) * B_GROUP_W]
            v = uv_ref[rows, B_WIDTH + g * B_GROUP_W:B_WIDTH + (g + 1) * B_GROUP_W]
            mu = jnp.mean(v, axis=-1, keepdims=True)
            var = jnp.mean(jnp.square(v - mu), axis=-1, keepdims=True)
            vn = (v - mu) * lax.rsqrt(var + EPS) * lng_ref[:, cols] + lnb_ref[:, cols]
            mixed = jnp.dot(ws_ref[g], vn.astype(BF16), preferred_element_type=F32) + bs_ref[g]
            cat_scr[rows, A_WIDTH + g * B_GROUP_W:A_WIDTH + (g + 1) * B_GROUP_W] = (u * mixed).astype(BF16)
    y = jnp.dot(cat_scr[...], wout_ref[...], preferred_element_type=F32)
    o_ref[...] = x_ref[...] + gm_ref[...] * y


def _finish_even(a, uv, x, w_out_bf16, ws_bf16, bs_b, ln_g, ln_b, gm):
    n, d = x.shape
    tm = min(512, n)
    row = lambda i: (0, 0)
    return pl.pallas_call(
        _finish_kernel,
        grid=(n // tm,),
        in_specs=[
            pl.BlockSpec((tm, A_WIDTH), lambda i: (i, 0)),
            pl.BlockSpec((tm, 2 * B_WIDTH), lambda i: (i, 0)),
            pl.BlockSpec((tm, d), lambda i: (i, 0)),
            pl.BlockSpec((A_WIDTH + B_WIDTH, d), row),
            pl.BlockSpec((B_GROUPS, CHUNK, CHUNK), lambda i: (0, 0, 0)),
            pl.BlockSpec((B_GROUPS, CHUNK, B_GROUP_W), lambda i: (0, 0, 0)),
            pl.BlockSpec((1, B_WIDTH), row), pl.BlockSpec((1, B_WIDTH), row),
            pl.BlockSpec((1, d), row),
        ],
        out_specs=pl.BlockSpec((tm, d), lambda i: (i, 0)),
        out_shape=jax.ShapeDtypeStruct((n, d), F32),
        scratch_shapes=[pltpu.VMEM((tm, A_WIDTH + B_WIDTH), BF16)],
        compiler_params=_params(("arbitrary",)),
        name="finish_even",
    )(a, uv, x, w_out_bf16, ws_bf16, bs_b, ln_g, ln_b, gm)


def _router_kernel(x_ref, g_ref, sh_ref, sc_ref, wrt_ref, ht_ref, aff_ref):
    h = _norm_mod(x_ref[...], g_ref[...], sh_ref[...], sc_ref[...])
    logits = _nt_dot(wrt_ref[...], h.astype(BF16))
    m = jnp.max(logits, axis=0, keepdims=True)
    e = jnp.exp(logits - m)
    aff_ref[...] = e / jnp.sum(e, axis=0, keepdims=True)
    ht_ref[...] = h.T.astype(BF16)


def _router(x, g, shift, scale, w_router_t_bf16):
    n, d = x.shape
    tm = min(512, n)
    row = lambda i: (0, 0)
    return pl.pallas_call(
        _router_kernel,
        grid=(n // tm,),
        in_specs=[
            pl.BlockSpec((tm, d), lambda i: (i, 0)),
            pl.BlockSpec((1, d), row), pl.BlockSpec((1, d), row), pl.BlockSpec((1, d), row),
            pl.BlockSpec((N_EXPERTS, d), row),
        ],
        out_specs=[pl.BlockSpec((d, tm), lambda i: (0, i)), pl.BlockSpec((N_EXPERTS, tm), lambda i: (0, i))],
        out_shape=[jax.ShapeDtypeStruct((d, n), BF16), jax.ShapeDtypeStruct((N_EXPERTS, n), F32)],
        compiler_params=_params(("arbitrary",)),
        name="router",
    )(x, g, shift, scale, w_router_t_bf16)


def _select_kernel(aff_ref, sel_ref, pos_ref, g_ref, first_ref, *, cap, capp):
    aff = aff_ref[...]
    n = aff.shape[1]
    idx = lax.broadcasted_iota(I32, aff.shape, 1)
    capf = float(cap)

    def count(mask):
        return jnp.sum(jnp.where(mask, 1.0, 0.0), axis=1, keepdims=True)

    def as_float(bits):
        return pltpu.bitcast(bits, F32)

    def thr_body(i, thr):
        cand = thr | jnp.left_shift(jnp.int32(1), 30 - i)
        return jnp.where(count(aff >= as_float(cand)) >= capf, cand, thr)

    thr = lax.fori_loop(0, 31, thr_body, jnp.zeros((aff.shape[0], 1), I32))
    gt = aff >= as_float(thr + 1)
    eq = (aff >= as_float(thr)) & jnp.logical_not(gt)
    need = capf - count(gt)
    nbits = int(n).bit_length()

    def tie_body(i, lim):
        cand = lim | jnp.left_shift(jnp.int32(1), nbits - 1 - i)
        return jnp.where(count(eq & (idx < cand)) <= need, cand, lim)

    lim = lax.fori_loop(0, nbits, tie_body, jnp.zeros((aff.shape[0], 1), I32))
    sel = jnp.where(gt | (eq & (idx < lim)), 1.0, 0.0)
    sel_ref[...] = sel
    g_ref[...] = aff * sel

    ri = lax.broadcasted_iota(I32, (LANES, LANES), 0)
    ci = lax.broadcasted_iota(I32, (LANES, LANES), 1)
    upper = jnp.where(ri < ci, 1.0, 0.0).astype(BF16)
    carry = jnp.zeros((aff.shape[0], 1), F32)
    for c in range(n // LANES):
        m = sel[:, c * LANES:(c + 1) * LANES]
        within = jnp.dot(m.astype(BF16), upper, preferred_element_type=F32)
        pos_ref[:, c * LANES:(c + 1) * LANES] = (within + carry).astype(I32)
        carry = carry + jnp.sum(m, axis=1, keepdims=True)

    incl = pos_ref[...].astype(F32) + sel
    lane = lax.broadcasted_iota(I32, first_ref.shape, 1)
    first = jnp.zeros(first_ref.shape, F32)
    for w in range(capp // ROUTE_WIN + 1):
        first = jnp.where(lane == w, count(incl <= float(w * ROUTE_WIN)), first)
    first_ref[...] = first.astype(I32)


def _select(aff_t, cap, capp):
    e, n = aff_t.shape
    full = lambda: (0, 0)
    return pl.pallas_call(
        functools.partial(_select_kernel, cap=cap, capp=capp),
        grid=(),
        in_specs=[pl.BlockSpec((e, n), full)],
        out_specs=[pl.BlockSpec((e, n), full)] * 3 + [pl.BlockSpec((e, LANES), full)],
        out_shape=[jax.ShapeDtypeStruct((e, n), F32), jax.ShapeDtypeStruct((e, n), I32),
                   jax.ShapeDtypeStruct((e, n), F32), jax.ShapeDtypeStruct((e, LANES), I32)],
        compiler_params=pltpu.CompilerParams(vmem_limit_bytes=VMEM_LIMIT_BYTES),
        name="expert_select",
    )(aff_t)


def _one_hot_rows(rel, win):
    r = lax.broadcasted_iota(I32, (win, rel.shape[1]), 0)
    return jnp.where(r == rel, 1.0, 0.0).astype(BF16)


def _one_hot_window(pos_row, sel_row, lo, base, win=ROUTE_WIN):
    rel = jnp.where((sel_row > 0.0) & (pos_row >= lo), pos_row - base, -1)
    return _one_hot_rows(rel, win)


def _window_plan(p0, p1, win=ROUTE_WIN):
    start = (p0 // LANES) * LANES
    n_chunks = jnp.where(p1 > p0, (p1 - start + win - 1) // win, 0)
    return start, n_chunks


def _dispatch_kernel(first_ref, ht_hbm, sel_ref, pos_ref, g_ref, xs_ref, gslot_ref, ht_scr, acc_scr, gs_scr, sem,
                     *, k_chunk):
    e = pl.program_id(0)
    w = pl.program_id(1)
    n = ht_scr.shape[1]

    @pl.when((e == 0) & (w == 0))
    def _():
        load = pltpu.make_async_copy(ht_hbm, ht_scr, sem)
        load.start()
        load.wait()

    t0 = first_ref[e, w]
    t1 = first_ref[e, w + 1]
    start = (t0 // ROUTE_TILE) * ROUTE_TILE
    slot0 = w * ROUTE_WIN

    def chunk(lo_tok, size):
        base = pl.multiple_of(jnp.minimum(lo_tok, n - size), ROUTE_TILE)
        toks = pl.ds(base, size)
        tok = base + lax.broadcasted_iota(I32, (1, size), 1)
        pos_row = pos_ref[0, :, toks]
        keep = (sel_ref[0, :, toks] > 0.0) & (tok >= lo_tok)
        rel = jnp.where(keep, pos_row - slot0, -1)
        onehot = _one_hot_rows(rel, ROUTE_WIN)
        g_row = g_ref[0, :, toks]
        g_hi = g_row.astype(BF16)
        r1 = g_row - g_hi.astype(F32)
        g_mid = r1.astype(BF16)
        g_lo = (r1 - g_mid.astype(F32)).astype(BF16)
        g8 = jnp.concatenate([g_hi, g_mid, g_lo, jnp.zeros((5, size), BF16)], axis=0)
        return _nt_dot(ht_scr[:, toks], onehot), _nt_dot(g8, onehot)

    acc_scr[...], gs_scr[...] = chunk(start, k_chunk)
    end_main = start + k_chunk
    n_extra = jnp.where(t1 > end_main, (t1 - end_main + ROUTE_TILE - 1) // ROUTE_TILE, 0)

    def extra(k, carry):
        x, gs = chunk(end_main + k * ROUTE_TILE, ROUTE_TILE)
        acc_scr[...] += x
        gs_scr[...] += gs
        return carry

    lax.fori_loop(0, n_extra, extra, 0)
    xs_ref[0] = acc_scr[...].T.astype(BF16)
    gs = gs_scr[...]
    gslot_ref[0] = gs[0:1] + gs[1:2] + gs[2:3]


def _dispatch(first_tok, h_t, sel3, pos3, g3, capp):
    d, n = h_t.shape
    k_chunk = min(9 * ROUTE_TILE, n)
    row3 = pl.BlockSpec((1, 1, n), lambda e, w, ft: (e, 0, 0))
    return pl.pallas_call(
        functools.partial(_dispatch_kernel, k_chunk=k_chunk),
        grid_spec=pltpu.PrefetchScalarGridSpec(
            num_scalar_prefetch=1,
            grid=(N_EXPERTS, capp // ROUTE_WIN),
            in_specs=[pl.BlockSpec(memory_space=pl.ANY), row3, row3, row3],
            out_specs=[
                pl.BlockSpec((1, ROUTE_WIN, d), lambda e, w, ft: (e, w, 0)),
                pl.BlockSpec((1, 1, ROUTE_WIN), lambda e, w, ft: (e, 0, w)),
            ],
            scratch_shapes=[pltpu.VMEM((d, n), BF16), pltpu.VMEM((d, ROUTE_WIN), F32),
                            pltpu.VMEM((8, ROUTE_WIN), F32), pltpu.SemaphoreType.DMA],
        ),
        out_shape=[jax.ShapeDtypeStruct((N_EXPERTS, capp, d), BF16),
                   jax.ShapeDtypeStruct((N_EXPERTS, 1, capp), F32)],
        compiler_params=_params(("arbitrary", "arbitrary")),
        name="moe_dispatch",
    )(first_tok, h_t, sel3, pos3, g3)


def _ffn_up_kernel(xs_ref, wg_ref, wu_ref, h_ref):
    xs = xs_ref[0].astype(BF16)
    a = jnp.dot(xs, wg_ref[0].astype(BF16), preferred_element_type=F32)
    u = jnp.dot(xs, wu_ref[0].astype(BF16), preferred_element_type=F32)
    h_ref[0] = (a * jax.nn.sigmoid(a) * u).astype(BF16)


def _ffn_up(xs, w_gate, w_up, layer):
    e, capp, d = xs.shape
    f = w_gate.shape[3]
    tf = 512
    return pl.pallas_call(
        _ffn_up_kernel,
        grid=(e, f // tf),
        in_specs=[
            pl.BlockSpec((1, capp, d), lambda i, j: (i, 0, 0)),
            pl.BlockSpec((None, 1, d, tf), lambda i, j: (layer, i, 0, j)),
            pl.BlockSpec((None, 1, d, tf), lambda i, j: (layer, i, 0, j)),
        ],
        out_specs=pl.BlockSpec((1, capp, tf), lambda i, j: (i, 0, j)),
        out_shape=jax.ShapeDtypeStruct((e, capp, f), BF16),
        compiler_params=_params(("arbitrary", "arbitrary")),
        name="moe_ffn_up",
    )(xs, w_gate, w_up)


def _ffn_down_kernel(h_ref, wd_ref, gslot_ref, y_ref):
    y = jnp.dot(h_ref[0], wd_ref[0].astype(BF16), preferred_element_type=F32)
    y_ref[0] = (y.T * gslot_ref[0]).astype(BF16)


def _ffn_down(h, w_down, gslot, layer):
    e, capp, f = h.shape
    d = w_down.shape[3]
    td = 2048
    return pl.pallas_call(
        _ffn_down_kernel,
        grid=(e, d // td),
        in_specs=[
            pl.BlockSpec((1, capp, f), lambda i, j: (i, 0, 0)),
            pl.BlockSpec((None, 1, f, td), lambda i, j: (layer, i, 0, j)),
            pl.BlockSpec((1, 1, capp), lambda i, j: (i, 0, 0)),
        ],
        out_specs=pl.BlockSpec((1, td, capp), lambda i, j: (i, j, 0)),
        out_shape=jax.ShapeDtypeStruct((e, d, capp), BF16),
        compiler_params=_params(("arbitrary", "arbitrary")),
        name="moe_ffn_down",
    )(h, w_down, gslot)


COMBINE_TD = 512
COMBINE_WIN = ROUTE_WIN


def _combine_kernel(ps_ref, pe_ref, y_ref, sel_ref, pos_ref, x_ref, gate_ref, o_ref, acc_scr):
    t = pl.program_id(1)
    n_exp, _, capp = y_ref.shape

    def window(e, k):
        start, _ = _window_plan(ps_ref[e, t], pe_ref[e, t], COMBINE_WIN)
        lo = start + k * COMBINE_WIN
        base = pl.multiple_of(jnp.minimum(lo, capp - COMBINE_WIN), LANES)
        onehot = _one_hot_window(pos_ref[e, pl.ds(t, 1), :], sel_ref[e, pl.ds(t, 1), :], lo, base, COMBINE_WIN)
        return [y_ref[e, :, pl.ds(base, COMBINE_WIN)]], [onehot]

    lhs, rhs = [], []
    for e in range(n_exp):
        rows, hots = window(e, 0)
        lhs += rows
        rhs += hots
    acc_scr[...] = jnp.dot(jnp.concatenate(lhs, axis=1), jnp.concatenate(rhs, axis=0), preferred_element_type=F32)

    for e in range(n_exp):
        n_chunks = _window_plan(ps_ref[e, t], pe_ref[e, t], COMBINE_WIN)[1]

        def extra(k, carry, e=e):
            rows, hots = window(e, k)
            acc_scr[...] += jnp.dot(jnp.concatenate(rows, axis=1), jnp.concatenate(hots, axis=0),
                                    preferred_element_type=F32)
            return carry

        lax.fori_loop(1, n_chunks, extra, 0)

    o_ref[...] = x_ref[...] + gate_ref[...] * acc_scr[...].T


def _combine(tile_start, tile_end, y_t, sel_t, pos_t, x, gate):
    e, d, capp = y_t.shape
    n = x.shape[0]
    n_tiles = n // ROUTE_TILE
    td = COMBINE_TD
    y_blk = pl.BlockSpec((e, td, capp), lambda j, t, ps, pe: (0, j, 0))
    full = pl.BlockSpec((e, n_tiles, ROUTE_TILE), lambda j, t, ps, pe: (0, 0, 0))
    tile = pl.BlockSpec((ROUTE_TILE, td), lambda j, t, ps, pe: (t, j))
    return pl.pallas_call(
        _combine_kernel,
        grid_spec=pltpu.PrefetchScalarGridSpec(
            num_scalar_prefetch=2,
            grid=(d // td, n_tiles),
            in_specs=[y_blk, full, full, tile, pl.BlockSpec((1, td), lambda j, t, ps, pe: (0, j))],
            out_specs=tile,
            scratch_shapes=[pltpu.VMEM((td, ROUTE_TILE), F32)],
        ),
        out_shape=jax.ShapeDtypeStruct((n, d), F32),
        compiler_params=_params(("arbitrary", "arbitrary")),
        name="moe_combine",
    )(tile_start, tile_end, y_t, sel_t, pos_t, x, gate)


def _tile_bounds(pos, tile, cap):
    start = pos[:, ::tile]
    end = jnp.concatenate([start[:, 1:], jnp.full((pos.shape[0], 1), cap, I32)], axis=1)
    return start, end


def _expert_choice_ffn(x, g, shift, scale, gate, w_router, w_gate, w_up, w_down, layer):
    n, d = x.shape
    cap = CAPACITY_FACTOR * n // N_EXPERTS
    capp = max(cap, ROUTE_WIN)
    h_t, aff_t = _router(x, g, shift, scale, w_router.T.astype(BF16))
    sel, pos, gsel, first_tok = _select(aff_t, cap, capp)
    as3 = lambda a: a.reshape(N_EXPERTS, 1, n)
    xs, gslot = _dispatch(first_tok, h_t, as3(sel), as3(pos), as3(gsel), capp)
    hidden = _ffn_up(xs, w_gate, w_up, layer)
    y_t = _ffn_down(hidden, w_down, gslot, layer)
    as_tiles = lambda a: a.reshape(N_EXPERTS, n // ROUTE_TILE, ROUTE_TILE)
    return _combine(*_tile_bounds(pos, ROUTE_TILE, cap), y_t, as_tiles(sel), as_tiles(pos), x, gate)


def _final_kernel(x_ref, g_ref, o_ref):
    x = x_ref[...]
    o_ref[...] = x * lax.rsqrt(jnp.mean(x * x, axis=-1, keepdims=True) + EPS) * g_ref[...]


def _final_norm(x, g_final):
    n, d = x.shape
    tm = min(512, n)
    return pl.pallas_call(
        _final_kernel,
        grid=(n // tm,),
        in_specs=[pl.BlockSpec((tm, d), lambda i: (i, 0)), pl.BlockSpec((1, d), lambda i: (0, 0))],
        out_specs=pl.BlockSpec((tm, d), lambda i: (i, 0)),
        out_shape=jax.ShapeDtypeStruct((n, d), F32),
        compiler_params=_params(("arbitrary",)),
        name="final_norm",
    )(x, g_final)


FFT_K1_BLK = 16


def _fourier_in_kernel(x_ref, g_ref, sh_ref, sc_ref, cs_ref, perm_ref, a_ref, b_ref):
    h = _norm_mod(x_ref[...], g_ref[...], sh_ref[...], sc_ref[...]).astype(BF16)
    hp = jnp.dot(perm_ref[...], h, preferred_element_type=F32).astype(BF16)
    n2, t1_blk = a_ref.shape[0], a_ref.shape[1]
    for g in range(C_GROUPS):
        cols = slice(g * C_GROUP_W, (g + 1) * C_GROUP_W)
        ab = jnp.dot(hp[:, cols], cs_ref[...], preferred_element_type=F32)
        a_ref[:, :, cols] = ab[:, :C_GROUP_W].astype(BF16).reshape(n2, t1_blk, C_GROUP_W)
        b_ref[:, :, cols] = ab[:, C_GROUP_W:].astype(BF16).reshape(n2, t1_blk, C_GROUP_W)


def _fourier_in(x, g, shift, scale, cs):
    n, d = x.shape
    tm = 512
    n1 = n // FFT_N2
    t1_blk = tm // FFT_N2
    row = lambda i: (0, 0)
    tile = pl.BlockSpec((tm, d), lambda i: (i, 0))
    ab_blk = pl.BlockSpec((FFT_N2, t1_blk, d), lambda i: (0, i, 0))
    ab_shape = jax.ShapeDtypeStruct((FFT_N2, n1, d), BF16)
    perm = np.zeros((tm, tm), np.float32)
    src = np.arange(tm)
    perm[(src % FFT_N2) * t1_blk + src // FFT_N2, src] = 1.0
    return pl.pallas_call(
        _fourier_in_kernel,
        grid=(n // tm,),
        in_specs=[tile, pl.BlockSpec((1, d), row), pl.BlockSpec((1, d), row), pl.BlockSpec((1, d), row),
                  pl.BlockSpec((C_GROUP_W, 2 * C_GROUP_W), row), pl.BlockSpec((tm, tm), row)],
        out_specs=[ab_blk, ab_blk],
        out_shape=[ab_shape, ab_shape],
        compiler_params=_params(("arbitrary",)),
        name="fourier_channel_dft",
    )(x, g, shift, scale, cs, jnp.asarray(perm, BF16))


def _fourier_stage1_kernel(a_ref, b_ref, ma_ref, mb_ref, ct_ref, st_ref, zr_ref, zi_ref):
    n1 = a_ref.shape[1]
    z = (jnp.dot(ma_ref[...], a_ref[0], preferred_element_type=F32)
         + jnp.dot(mb_ref[...], b_ref[0], preferred_element_type=F32))
    ct = ct_ref[0]
    st = st_ref[0]
    for c in range(a_ref.shape[2] // LANES):
        cols = slice(c * LANES, (c + 1) * LANES)
        zr = z[:n1, cols]
        zi = z[n1:, cols]
        zr_ref[0, :, cols] = (zr * ct + zi * st).astype(BF16)
        zi_ref[0, :, cols] = (zi * ct - zr * st).astype(BF16)


def _fourier_stage1(a3, b3, ma, mb, ct, st):
    n2, n1, d = a3.shape
    blk = pl.BlockSpec((1, n1, d), lambda j: (j, 0, 0))
    mat = pl.BlockSpec((2 * n1, n1), lambda j: (0, 0))
    tw = pl.BlockSpec((1, n1, LANES), lambda j: (j, 0, 0))
    shp = jax.ShapeDtypeStruct((n2, n1, d), BF16)
    return pl.pallas_call(
        _fourier_stage1_kernel,
        grid=(n2,),
        in_specs=[blk, blk, mat, mat, tw, tw],
        out_specs=[blk, blk],
        out_shape=[shp, shp],
        compiler_params=_params(("arbitrary",)),
        name="fourier_stage1",
    )(a3, b3, ma, mb, ct, st)


def _fourier_out_kernel(zr_ref, zi_ref, bc_ref, bs_ref, wo_ref, x_ref, gm_ref, o_ref):
    rows = zr_ref.shape[0] * zr_ref.shape[1]
    d = zr_ref.shape[2]
    zr = zr_ref[...].reshape(rows, d)
    zi = zi_ref[...].reshape(rows, d)
    f = (jnp.dot(bc_ref[...], zr, preferred_element_type=F32)
         + jnp.dot(bs_ref[...], zi, preferred_element_type=F32))
    y = jnp.dot(f.astype(BF16), wo_ref[...], preferred_element_type=F32)
    o_ref[...] = x_ref[...] + gm_ref[...] * y.reshape(o_ref.shape)


def _fourier_out(zr3, zi3, bd_c, bd_s, w_o_bf16, x, gm):
    n2, n1, d = zr3.shape
    n = n1 * n2
    rows = n2 * FFT_K1_BLK
    x3 = x.reshape(n2, n1, d)
    blk = pl.BlockSpec((n2, FFT_K1_BLK, d), lambda i: (0, i, 0))
    const = lambda i: (0, 0)
    out = pl.pallas_call(
        _fourier_out_kernel,
        grid=(n1 // FFT_K1_BLK,),
        in_specs=[blk, blk, pl.BlockSpec((rows, rows), const), pl.BlockSpec((rows, rows), const),
                  pl.BlockSpec((d, d), const), blk, pl.BlockSpec((1, d), const)],
        out_specs=blk,
        out_shape=jax.ShapeDtypeStruct((n2, n1, d), F32),
        compiler_params=_params(("arbitrary",)),
        name="fourier_stage2_out",
    )(zr3, zi3, bd_c, bd_s, w_o_bf16, x3, gm)
    return out.reshape(n, d)


def _fourier_constants(n):
    n1, n2 = n // FFT_N2, FFT_N2
    two_pi = 2.0 * np.pi

    def angles(a, b, period):
        return two_pi * ((np.outer(a, b) % period).astype(np.float64) / period)

    kc = np.arange(C_GROUP_W)
    ang = angles(kc, kc, C_GROUP_W)
    cs = np.concatenate([np.cos(ang), np.sin(ang)], axis=1) / np.sqrt(C_GROUP_W)
    k1 = np.arange(n1)
    ang1 = angles(k1, k1, n1)
    c1, s1 = np.cos(ang1) / np.sqrt(n), np.sin(ang1) / np.sqrt(n)
    ma = np.concatenate([c1, -s1], axis=0)
    mb = np.concatenate([-s1, -c1], axis=0)
    t2 = np.arange(n2)
    angt = angles(t2, k1, n)
    ct = np.repeat(np.cos(angt)[:, :, None], LANES, axis=2)
    st = np.repeat(np.sin(angt)[:, :, None], LANES, axis=2)
    ang2 = angles(t2, t2, n2)
    k1_blk = FFT_K1_BLK
    bd_c = np.zeros((n2 * k1_blk, n2 * k1_blk))
    bd_s = np.zeros((n2 * k1_blk, n2 * k1_blk))
    for kl in range(k1_blk):
        bd_c[kl::k1_blk, kl::k1_blk] = np.cos(ang2)
        bd_s[kl::k1_blk, kl::k1_blk] = np.sin(ang2)
    bf = lambda a: jnp.asarray(a, F32).astype(BF16)
    return bf(cs), bf(ma), bf(mb), jnp.asarray(ct, F32), jnp.asarray(st, F32), bf(bd_c), bf(bd_s)


def _fourier_mix_layer(x, g, shift, scale, gm, w_o):
    cs, ma, mb, ct, st, bd_c, bd_s = _fourier_constants(x.shape[0])
    a3, b3 = _fourier_in(x, g, shift, scale, cs)
    zr3, zi3 = _fourier_stage1(a3, b3, ma, mb, ct, st)
    return _fourier_out(zr3, zi3, bd_c, bd_s, w_o.astype(BF16), x, gm)


def _even_layer_mix(x, ctx, mods, g_mix, w_in, w_out, lamv, g_subln, ln_g, ln_b, w_s, b_s, lam_init, need_ctx_out):
    sm, cm, gm = mods["lat"][0:3]
    smc, cmc, gmc = mods["ctx"][0:3]
    n = x.shape[0]
    w_in_b = w_in.astype(BF16)
    w_out_b = w_out.astype(BF16)
    ws_b = w_s.astype(BF16)
    bs_b = jnp.broadcast_to(b_s[:, :, None], (B_GROUPS, CHUNK, B_GROUP_W))
    cos_l, sin_l = _rope_tables(n, True)
    cos_c, sin_c = _rope_tables(ctx.shape[0], False)
    q_l, k_l, vt_l, uv_l = _in_projection(x, g_mix, sm, cm, w_in_b, cos_l, sin_l)
    q_c, k_c, vt_c, uv_c = _in_projection(ctx, g_mix, smc, cmc, w_in_b, cos_c, sin_c)
    a_l = _diff_attention(q_l, [k_c, k_l], [vt_c, vt_l], lamv, g_subln, lam_init)
    x = _finish_even(a_l, uv_l, x, w_out_b, ws_b, bs_b, ln_g, ln_b, gm)
    if need_ctx_out:
        a_c = _diff_attention(q_c, [k_c], [vt_c], lamv, g_subln, lam_init)
        ctx = _finish_even(a_c, uv_c, ctx, w_out_b, ws_b, bs_b, ln_g, ln_b, gmc)
    return x, ctx


def kernel(x, c, ctx, c_ctx, w_mod, b_mod, g_norm_mix, g_norm_ffn, w_in, w_out, lam_q1, lam_k1, lam_q2, lam_k2,
           g_subln, sgu_ln_g, sgu_ln_b, w_spatial, b_spatial, w_fourier_out, w_router, w_gate, w_up, w_down, g_final):
    assert x.shape[0] == 1 and DEPTH == 2
    d = D_MODEL
    x2 = x[0]
    ctx2 = ctx[0]
    c8 = jnp.zeros((8, d), F32).at[0].set(c[0]).at[1].set(c_ctx)
    mod_all = _modulation(c8, w_mod, b_mod)
    row = lambda v: v.reshape(1, -1)

    def mods_of(i):
        lat = [mod_all[i, 0:1, k * d:(k + 1) * d] for k in range(6)]
        cx = [mod_all[i, 1:2, k * d:(k + 1) * d] for k in range(6)]
        return {"lat": lat, "ctx": cx}

    m0 = mods_of(0)
    lam_init0 = 0.8 - 0.6 * math.exp(-0.3 * 0)
    lamv = jnp.zeros((8, LANES), F32)
    for r, v in enumerate((lam_q1[0], lam_k1[0], lam_q2[0], lam_k2[0])):
        lamv = lamv.at[r, :A_HEAD_DIM].set(v)
    x2, ctx2 = _even_layer_mix(x2, ctx2, m0, row(g_norm_mix[0]), w_in[0], w_out[0], lamv, row(g_subln[0]),
                               row(sgu_ln_g[0]), row(sgu_ln_b[0]), w_spatial[0], b_spatial[0], lam_init0, True)
    x2 = _expert_choice_ffn(x2, row(g_norm_ffn[0]), m0["lat"][3], m0["lat"][4], m0["lat"][5],
                            w_router[0], w_gate, w_up, w_down, 0)
    ctx2 = _expert_choice_ffn(ctx2, row(g_norm_ffn[0]), m0["ctx"][3], m0["ctx"][4], m0["ctx"][5],
                              w_router[0], w_gate, w_up, w_down, 0)

    m1 = mods_of(1)
    x2 = _fourier_mix_layer(x2, row(g_norm_mix[1]), m1["lat"][0], m1["lat"][1], m1["lat"][2], w_fourier_out[0])
    x2 = _expert_choice_ffn(x2, row(g_norm_ffn[1]), m1["lat"][3], m1["lat"][4], m1["lat"][5],
                            w_router[1], w_gate, w_up, w_down, 1)
    out = _final_norm(x2, row(g_final))
    del ctx2
    return out[None]
```

```python
import functools
import math

import numpy as np
import jax
import jax.numpy as jnp
from jax import lax
from jax.experimental import pallas as pl
from jax.experimental.pallas import tpu as pltpu

F32 = jnp.float32
BF16 = jnp.bfloat16
I32 = jnp.int32

D_MODEL = 2048
DEPTH = 2
GRID_W = 64
EPS = 1e-6

A_HEADS = 8
A_HEAD_DIM = 64
A_V_DIM = 2 * A_HEAD_DIM
A_WIDTH = A_HEADS * A_V_DIM
QK_COLS = A_HEADS * 2 * A_HEAD_DIM
ROPE_THETA = 10000.0

B_GROUPS = 8
B_GROUP_W = 128
B_WIDTH = B_GROUPS * B_GROUP_W
CHUNK = 128
IN_COLS = 2 * QK_COLS + A_WIDTH + 2 * B_WIDTH

C_GROUPS = 4
C_GROUP_W = D_MODEL // C_GROUPS

N_EXPERTS = 16
CAPACITY_FACTOR = 2
F_EXPERT = D_MODEL // 2

LANES = 128
MXU_DIM = 256
VMEM_LIMIT_BYTES = 56 * 1024 * 1024

FFT_N2 = 32
ROUTE_TILE = 256
ROUTE_WIN = 256


def _params(sem):
    return pltpu.CompilerParams(dimension_semantics=sem, vmem_limit_bytes=VMEM_LIMIT_BYTES)


def _nt_dot(a, b):
    return lax.dot_general(a, b, (((1,), (1,)), ((), ())), preferred_element_type=F32)


def _norm_mod(x, g, shift, scale):
    ms = jnp.mean(x * x, axis=-1, keepdims=True)
    y = x * lax.rsqrt(ms + EPS) * g
    return y * (1.0 + scale) + shift


def _mod_kernel(c_ref, w_ref, b_ref, o_ref):
    c = c_ref[...]
    s = c * jax.nn.sigmoid(c)
    o_ref[...] = jnp.dot(s.astype(BF16), w_ref[...].astype(BF16), preferred_element_type=F32) + b_ref[...]


def _modulation(c8, w_mod, b_mod):
    depth, d, n6 = w_mod.shape
    tn = 1024
    return pl.pallas_call(
        _mod_kernel,
        grid=(depth, n6 // tn),
        in_specs=[
            pl.BlockSpec((8, d), lambda l, j: (0, 0)),
            pl.BlockSpec((None, d, tn), lambda l, j: (l, 0, j)),
            pl.BlockSpec((None, 1, tn), lambda l, j: (l, 0, j)),
        ],
        out_specs=pl.BlockSpec((None, 8, tn), lambda l, j: (l, 0, j)),
        out_shape=jax.ShapeDtypeStruct((depth, 8, n6), F32),
        compiler_params=_params(("arbitrary", "arbitrary")),
        name="modulation",
    )(c8, w_mod, b_mod.reshape(depth, 1, n6))


PROJ_TN = 1024
PROJ_ROW_PARTS = 4
_Q_TILES = QK_COLS // PROJ_TN
_V_TILES = A_WIDTH // PROJ_TN
_UV_TILES = 2 * B_WIDTH // PROJ_TN


def _rope_chunk(z, cos, sin_signed, first_half):
    partner = jnp.where(first_half, pltpu.roll(z, LANES - 16, 1), pltpu.roll(z, 16, 1))
    return z * cos + partner * sin_signed


def _proj_kernel(x_ref, g_ref, sh_ref, sc_ref, w_ref, cos_ref, sin_ref,
                 q_ref, k_ref, vt_ref, uv_ref, h_scr):
    j = pl.program_id(1)

    tm = h_scr.shape[0]

    @pl.when(j == 0)
    def _():
        for r in range(tm // CHUNK):
            rows = slice(r * CHUNK, (r + 1) * CHUNK)
            h_scr[rows, :] = _norm_mod(x_ref[rows, :], g_ref[...], sh_ref[...], sc_ref[...]).astype(BF16)

    n_parts = min(PROJ_ROW_PARTS, tm // LANES)
    parts = [slice(r * (tm // n_parts), (r + 1) * (tm // n_parts)) for r in range(n_parts)]

    def project(rows):
        return jnp.dot(h_scr[rows, :], w_ref[...], preferred_element_type=F32)

    def roped(rows, scale):
        z = project(rows)
        cos = cos_ref[rows, :]
        sin = sin_ref[rows, :]
        lane = lax.broadcasted_iota(I32, cos.shape, 1)
        first_half = (lane % 32) < 16
        chunks = []
        for c in range(PROJ_TN // LANES):
            zc = z[:, c * LANES:(c + 1) * LANES]
            chunks.append(_rope_chunk(zc, cos, sin, first_half) * scale)
        return jnp.concatenate(chunks, axis=1)

    @pl.when(j < _Q_TILES)
    def _():
        for rows in parts:
            q_ref[rows, :] = roped(rows, A_HEAD_DIM ** -0.5 * math.log2(math.e)).astype(BF16)

    @pl.when((j >= _Q_TILES) & (j < 2 * _Q_TILES))
    def _():
        for rows in parts:
            k_ref[rows, :] = roped(rows, 1.0).astype(BF16)

    @pl.when((j >= 2 * _Q_TILES) & (j < 2 * _Q_TILES + _V_TILES))
    def _():
        for rows in parts:
            vt_ref[:, rows] = project(rows).T.astype(BF16)

    @pl.when(j >= 2 * _Q_TILES + _V_TILES)
    def _():
        for rows in parts:
            uv_ref[rows, :] = jax.nn.gelu(project(rows))


def _in_projection(x, g, shift, scale, w_in_bf16, cos_t, sin_t):
    n, d = x.shape
    tm = min(1024, n)
    nq = _Q_TILES
    nj = IN_COLS // PROJ_TN
    row = lambda i, j: (0, 0)
    v_tile = lambda j: jnp.clip(j - 2 * nq, 0, _V_TILES - 1)
    return pl.pallas_call(
        _proj_kernel,
        grid=(n // tm, nj),
        in_specs=[
            pl.BlockSpec((tm, d), lambda i, j: (i, 0)),
            pl.BlockSpec((1, d), row), pl.BlockSpec((1, d), row), pl.BlockSpec((1, d), row),
            pl.BlockSpec((d, PROJ_TN), lambda i, j: (0, j)),
            pl.BlockSpec((tm, LANES), lambda i, j: (i, 0)),
            pl.BlockSpec((tm, LANES), lambda i, j: (i, 0)),
        ],
        out_specs=[
            pl.BlockSpec((tm, PROJ_TN), lambda i, j: (i, jnp.clip(j, 0, nq - 1))),
            pl.BlockSpec((tm, PROJ_TN), lambda i, j: (i, jnp.clip(j - nq, 0, nq - 1))),
            pl.BlockSpec((PROJ_TN, tm), lambda i, j: (v_tile(j), i)),
            pl.BlockSpec((tm, PROJ_TN), lambda i, j: (i, jnp.clip(j - 2 * nq - _V_TILES, 0, _UV_TILES - 1))),
        ],
        out_shape=[
            jax.ShapeDtypeStruct((n, QK_COLS), BF16),
            jax.ShapeDtypeStruct((n, QK_COLS), BF16),
            jax.ShapeDtypeStruct((A_WIDTH, n), BF16),
            jax.ShapeDtypeStruct((n, 2 * B_WIDTH), F32),
        ],
        scratch_shapes=[pltpu.VMEM((tm, d), BF16)],
        compiler_params=_params(("arbitrary", "arbitrary")),
        name="in_projection",
    )(x, g, shift, scale, w_in_bf16, cos_t, sin_t)


def _rope_tables(n, rotate):
    if not rotate:
        return jnp.ones((n, LANES), F32), jnp.zeros((n, LANES), F32)
    t = np.arange(n)
    row = (t // GRID_W).astype(np.float32)
    col = (t % GRID_W).astype(np.float32)
    dim = A_HEAD_DIM // 2
    inv = (np.float32(ROPE_THETA) ** (-np.arange(0, dim, 2, dtype=np.float32) / np.float32(dim))).astype(np.float32)
    ang_r = row[:, None] * inv[None, :]
    ang_c = col[:, None] * inv[None, :]
    ang64 = np.concatenate([ang_r, ang_r, ang_c, ang_c], axis=1)
    sign64 = np.concatenate([-np.ones(16), np.ones(16), -np.ones(16), np.ones(16)]).astype(np.float32)
    ang = np.tile(ang64, (1, LANES // 64))
    sign = np.tile(sign64, LANES // 64)
    return jnp.asarray(np.cos(ang), F32), jnp.asarray(np.sin(ang) * sign[None, :], F32)


def _attn_kernel(*refs, n_parts, tq, tk, lam_init):
    q_ref = refs[0]
    k_parts = refs[1:1 + n_parts]
    vt_parts = refs[1 + n_parts:1 + 2 * n_parts]
    lamv_ref, gs_ref, o_ref, m_scr, acc_scr, qst_scr, sa_scr, sb_scr, k_ref, vt_ref, done_scr = refs[1 + 2 * n_parts:]
    n_blocks = q_ref.shape[0] // tq
    n_kv = k_ref.shape[0] // tk

    off = 0
    for k_part, vt_part in zip(k_parts, vt_parts):
        rows = k_part.shape[0]
        k_ref[off:off + rows, :] = k_part[...]
        vt_ref[:, off:off + rows] = vt_part[...]
        off += rows

    def query_operand(i):
        qt = q_ref[pl.ds(pl.multiple_of(i * tq, tq), tq), :].astype(F32).T
        row = lax.broadcasted_iota(I32, qt.shape, 0)
        zero = jnp.zeros_like(qt)
        return jnp.concatenate([jnp.where(row < A_HEAD_DIM, qt, zero),
                                jnp.where(row >= A_HEAD_DIM, qt, zero)], axis=1).astype(BF16)

    def scores(qst, j):
        off = pl.multiple_of(j * tk, tk)
        return jnp.dot(k_ref[pl.ds(off, tk), :], qst, preferred_element_type=F32)

    ones_rows = jnp.ones((16, tk), BF16)

    def consume(s, j):
        off = pl.multiple_of(j * tk, tk)
        vtb = jnp.concatenate([vt_ref[:, pl.ds(off, tk)], ones_rows], axis=0)
        m_old = m_scr[...]
        m_new = jnp.maximum(m_old, jnp.max(s, axis=0, keepdims=True))
        alpha = jnp.exp2(m_old - m_new)
        p = jnp.exp2(s - m_new).astype(BF16)
        acc_scr[...] = alpha * acc_scr[...] + jnp.dot(vtb, p, preferred_element_type=F32)
        m_scr[...] = m_new

    lv = lamv_ref[...]
    lam = (jnp.exp(jnp.sum(lv[0:1] * lv[1:2], axis=-1, keepdims=True))
           - jnp.exp(jnp.sum(lv[2:3] * lv[3:4], axis=-1, keepdims=True)) + lam_init)

    cross_block = n_kv % 2 == 0
    qst_scr[...] = query_operand(0)
    if cross_block:
        sa_scr[...] = scores(qst_scr[...], 0)
        done_scr[...] = jnp.ones(done_scr.shape, F32)

    def finish(acc, i):
        ot = acc[:A_V_DIM, :] / acc[A_V_DIM:A_V_DIM + 1, :]
        o = (ot[:, :tq] - lam * ot[:, tq:]).T
        a = o * lax.rsqrt(jnp.mean(o * o, axis=-1, keepdims=True) + EPS) * gs_ref[...]
        o_ref[pl.ds(pl.multiple_of(i * tq, tq), tq), :] = (a * (1.0 - lam_init)).astype(BF16)

    def block(i, carry):
        qst = qst_scr[...]
        m_scr[...] = jnp.full(m_scr.shape, -jnp.inf, F32)
        acc_scr[...] = jnp.zeros(acc_scr.shape, F32)
        if not cross_block:
            sa_scr[...] = scores(qst, 0)

        def pair(p, c):
            j = 2 * p
            sb_scr[...] = scores(qst, j + 1)
            consume(sa_scr[...], j)
            sa_scr[...] = scores(qst, j + 2)
            consume(sb_scr[...], j + 1)
            return c

        qst_next = query_operand(jnp.minimum(i + 1, n_blocks - 1))
        if cross_block:
            lax.fori_loop(0, n_kv // 2 - 1, pair, 0)
            sb_scr[...] = scores(qst, n_kv - 1)
            consume(sa_scr[...], n_kv - 2)
            sa_scr[...] = scores(qst_next, 0)
            finish(done_scr[...], jnp.maximum(i - 1, 0))
            consume(sb_scr[...], n_kv - 1)
            done_scr[...] = acc_scr[...]
        else:
            lax.fori_loop(0, (n_kv - 1) // 2, pair, 0)
            consume(sa_scr[...], n_kv - 1)
            finish(acc_scr[...], i)
        qst_scr[...] = qst_next
        return carry

    lax.fori_loop(0, n_blocks, block, 0)
    if cross_block:
        finish(done_scr[...], n_blocks - 1)


def _pick_tile(n, candidates):
    for c in candidates:
        if n % c == 0:
            return c
    raise ValueError(f"no tile for {n}")


def _diff_attention(q, k_parts, vt_parts, lamv, g_subln, lam_init):
    n = q.shape[0]
    nk = sum(k.shape[0] for k in k_parts)
    tq = min(512, n)
    tk = _pick_tile(nk, (1408, 768, 512, 256))
    score_buf = pltpu.VMEM((tk, 2 * tq), F32)
    return pl.pallas_call(
        functools.partial(_attn_kernel, n_parts=len(k_parts), tq=tq, tk=tk, lam_init=lam_init),
        grid=(A_HEADS,),
        in_specs=[pl.BlockSpec((n, A_V_DIM), lambda h: (0, h))]
        + [pl.BlockSpec((k.shape[0], A_V_DIM), lambda h: (0, h)) for k in k_parts]
        + [pl.BlockSpec((A_V_DIM, vt.shape[1]), lambda h: (h, 0)) for vt in vt_parts]
        + [pl.BlockSpec((8, LANES), lambda h: (0, 0)), pl.BlockSpec((1, A_V_DIM), lambda h: (0, 0))],
        out_specs=pl.BlockSpec((n, A_V_DIM), lambda h: (0, h)),
        out_shape=jax.ShapeDtypeStruct((n, A_WIDTH), BF16),
        scratch_shapes=[pltpu.VMEM((1, 2 * tq), F32),
                        pltpu.VMEM((A_V_DIM + 16, 2 * tq), F32),
                        pltpu.VMEM((A_V_DIM, 2 * tq), BF16),
                        score_buf, score_buf,
                        pltpu.VMEM((nk, A_V_DIM), BF16), pltpu.VMEM((A_V_DIM, nk), BF16),
                        pltpu.VMEM((A_V_DIM + 16, 2 * tq), F32)],
        compiler_params=_params(("arbitrary",)),
        name="diff_attention",
    )(q, *k_parts, *vt_parts, lamv, g_subln)


def _finish_kernel(a_ref, uv_ref, x_ref, wout_ref, ws_ref, bs_ref, lng_ref, lnb_ref, gm_ref, o_ref, cat_scr):
    tm = a_ref.shape[0]
    cat_scr[:, :A_WIDTH] = a_ref[...]
    for c in range(tm // CHUNK):
        rows = slice(c * CHUNK, (c + 1) * CHUNK)
        for g in range(B_GROUPS):
            cols = slice(g * B_GROUP_W, (g + 1) * B_GROUP_W)
            u = uv_ref[rows, g * B_GROUP_W:(g + 1) * B_GROUP_W]
            v = uv_ref[rows, B_WIDTH + g * B_GROUP_W:B_WIDTH + (g + 1) * B_GROUP_W]
            mu = jnp.mean(v, axis=-1, keepdims=True)
            var = jnp.mean(jnp.square(v - mu), axis=-1, keepdims=True)
            vn = (v - mu) * lax.rsqrt(var + EPS) * lng_ref[:, cols] + lnb_ref[:, cols]
            mixed = jnp.dot(ws_ref[g], vn.astype(BF16), preferred_element_type=F32) + bs_ref[g]
            cat_scr[rows, A_WIDTH + g * B_GROUP_W:A_WIDTH + (g + 1) * B_GROUP_W] = (u * mixed).astype(BF16)
    y = jnp.dot(cat_scr[...], wout_ref[...], preferred_element_type=F32)
    o_ref[...] = x_ref[...] + gm_ref[...] * y


def _finish_even(a, uv, x, w_out_bf16, ws_bf16, bs_b, ln_g, ln_b, gm):
    n, d = x.shape
    tm = min(512, n)
    row = lambda i: (0, 0)
    return pl.pallas_call(
        _finish_kernel,
        grid=(n // tm,),
        in_specs=[
            pl.BlockSpec((tm, A_WIDTH), lambda i: (i, 0)),
            pl.BlockSpec((tm, 2 * B_WIDTH), lambda i: (i, 0)),
            pl.BlockSpec((tm, d), lambda i: (i, 0)),
            pl.BlockSpec((A_WIDTH + B_WIDTH, d), row),
            pl.BlockSpec((B_GROUPS, CHUNK, CHUNK), lambda i: (0, 0, 0)),
            pl.BlockSpec((B_GROUPS, CHUNK, B_GROUP_W), lambda i: (0, 0, 0)),
            pl.BlockSpec((1, B_WIDTH), row), pl.BlockSpec((1, B_WIDTH), row),
            pl.BlockSpec((1, d), row),
        ],
        out_specs=pl.BlockSpec((tm, d), lambda i: (i, 0)),
        out_shape=jax.ShapeDtypeStruct((n, d), F32),
        scratch_shapes=[pltpu.VMEM((tm, A_WIDTH + B_WIDTH), BF16)],
        compiler_params=_params(("arbitrary",)),
        name="finish_even",
    )(a, uv, x, w_out_bf16, ws_bf16, bs_b, ln_g, ln_b, gm)


def _router_kernel(x_ref, g_ref, sh_ref, sc_ref, wrt_ref, ht_ref, aff_ref):
    h = _norm_mod(x_ref[...], g_ref[...], sh_ref[...], sc_ref[...])
    logits = _nt_dot(wrt_ref[...], h.astype(BF16))
    m = jnp.max(logits, axis=0, keepdims=True)
    e = jnp.exp(logits - m)
    aff_ref[...] = e / jnp.sum(e, axis=0, keepdims=True)
    ht_ref[...] = h.T.astype(BF16)


def _router(x, g, shift, scale, w_router_t_bf16):
    n, d = x.shape
    tm = min(512, n)
    row = lambda i: (0, 0)
    return pl.pallas_call(
        _router_kernel,
        grid=(n // tm,),
        in_specs=[
            pl.BlockSpec((tm, d), lambda i: (i, 0)),
            pl.BlockSpec((1, d), row), pl.BlockSpec((1, d), row), pl.BlockSpec((1, d), row),
            pl.BlockSpec((N_EXPERTS, d), row),
        ],
        out_specs=[pl.BlockSpec((d, tm), lambda i: (0, i)), pl.BlockSpec((N_EXPERTS, tm), lambda i: (0, i))],
        out_shape=[jax.ShapeDtypeStruct((d, n), BF16), jax.ShapeDtypeStruct((N_EXPERTS, n), F32)],
        compiler_params=_params(("arbitrary",)),
        name="router",
    )(x, g, shift, scale, w_router_t_bf16)


def _select_kernel(aff_ref, sel_ref, pos_ref, g_ref, first_ref, *, cap, capp):
    aff = aff_ref[...]
    n = aff.shape[1]
    idx = lax.broadcasted_iota(I32, aff.shape, 1)
    capf = float(cap)

    def count(mask):
        return jnp.sum(jnp.where(mask, 1.0, 0.0), axis=1, keepdims=True)

    def as_float(bits):
        return pltpu.bitcast(bits, F32)

    def thr_body(i, thr):
        cand = thr | jnp.left_shift(jnp.int32(1), 30 - i)
        return jnp.where(count(aff >= as_float(cand)) >= capf, cand, thr)

    thr = lax.fori_loop(0, 31, thr_body, jnp.zeros((aff.shape[0], 1), I32))
    gt = aff >= as_float(thr + 1)
    eq = (aff >= as_float(thr)) & jnp.logical_not(gt)
    need = capf - count(gt)
    nbits = int(n).bit_length()

    def tie_body(i, lim):
        cand = lim | jnp.left_shift(jnp.int32(1), nbits - 1 - i)
        return jnp.where(count(eq & (idx < cand)) <= need, cand, lim)

    lim = lax.fori_loop(0, nbits, tie_body, jnp.zeros((aff.shape[0], 1), I32))
    sel = jnp.where(gt | (eq & (idx < lim)), 1.0, 0.0)
    sel_ref[...] = sel
    g_ref[...] = aff * sel

    ri = lax.broadcasted_iota(I32, (LANES, LANES), 0)
    ci = lax.broadcasted_iota(I32, (LANES, LANES), 1)
    upper = jnp.where(ri < ci, 1.0, 0.0).astype(BF16)
    carry = jnp.zeros((aff.shape[0], 1), F32)
    for c in range(n // LANES):
        m = sel[:, c * LANES:(c + 1) * LANES]
        within = jnp.dot(m.astype(BF16), upper, preferred_element_type=F32)
        pos_ref[:, c * LANES:(c + 1) * LANES] = (within + carry).astype(I32)
        carry = carry + jnp.sum(m, axis=1, keepdims=True)

    incl = pos_ref[...].astype(F32) + sel
    lane = lax.broadcasted_iota(I32, first_ref.shape, 1)
    first = jnp.zeros(first_ref.shape, F32)
    for w in range(capp // ROUTE_WIN + 1):
        first = jnp.where(lane == w, count(incl <= float(w * ROUTE_WIN)), first)
    first_ref[...] = first.astype(I32)


def _select(aff_t, cap, capp):
    e, n = aff_t.shape
    full = lambda: (0, 0)
    return pl.pallas_call(
        functools.partial(_select_kernel, cap=cap, capp=capp),
        grid=(),
        in_specs=[pl.BlockSpec((e, n), full)],
        out_specs=[pl.BlockSpec((e, n), full)] * 3 + [pl.BlockSpec((e, LANES), full)],
        out_shape=[jax.ShapeDtypeStruct((e, n), F32), jax.ShapeDtypeStruct((e, n), I32),
                   jax.ShapeDtypeStruct((e, n), F32), jax.ShapeDtypeStruct((e, LANES), I32)],
        compiler_params=pltpu.CompilerParams(vmem_limit_bytes=VMEM_LIMIT_BYTES),
        name="expert_select",
    )(aff_t)


def _one_hot_rows(rel, win):
    r = lax.broadcasted_iota(I32, (win, rel.shape[1]), 0)
    return jnp.where(r == rel, 1.0, 0.0).astype(BF16)


def _one_hot_window(pos_row, sel_row, lo, base, win=ROUTE_WIN):
    rel = jnp.where((sel_row > 0.0) & (pos_row >= lo), pos_row - base, -1)
    return _one_hot_rows(rel, win)


def _window_plan(p0, p1, win=ROUTE_WIN):
    start = (p0 // LANES) * LANES
    n_chunks = jnp.where(p1 > p0, (p1 - start + win - 1) // win, 0)
    return start, n_chunks


def _dispatch_kernel(first_ref, ht_hbm, sel_ref, pos_ref, g_ref, xs_ref, gslot_ref, ht_scr, acc_scr, gs_scr, sem,
                     *, k_chunk):
    e = pl.program_id(0)
    w = pl.program_id(1)
    n = ht_scr.shape[1]

    @pl.when((e == 0) & (w == 0))
    def _():
        load = pltpu.make_async_copy(ht_hbm, ht_scr, sem)
        load.start()
        load.wait()

    t0 = first_ref[e, w]
    t1 = first_ref[e, w + 1]
    start = (t0 // ROUTE_TILE) * ROUTE_TILE
    slot0 = w * ROUTE_WIN

    def chunk(lo_tok, size):
        base = pl.multiple_of(jnp.minimum(lo_tok, n - size), ROUTE_TILE)
        toks = pl.ds(base, size)
        tok = base + lax.broadcasted_iota(I32, (1, size), 1)
        pos_row = pos_ref[0, :, toks]
        keep = (sel_ref[0, :, toks] > 0.0) & (tok >= lo_tok)
        rel = jnp.where(keep, pos_row - slot0, -1)
        onehot = _one_hot_rows(rel, ROUTE_WIN)
        g_row = g_ref[0, :, toks]
        g_hi = g_row.astype(BF16)
        r1 = g_row - g_hi.astype(F32)
        g_mid = r1.astype(BF16)
        g_lo = (r1 - g_mid.astype(F32)).astype(BF16)
        g8 = jnp.concatenate([g_hi, g_mid, g_lo, jnp.zeros((5, size), BF16)], axis=0)
        return _nt_dot(ht_scr[:, toks], onehot), _nt_dot(g8, onehot)

    acc_scr[...], gs_scr[...] = chunk(start, k_chunk)
    end_main = start + k_chunk
    n_extra = jnp.where(t1 > end_main, (t1 - end_main + ROUTE_TILE - 1) // ROUTE_TILE, 0)

    def extra(k, carry):
        x, gs = chunk(end_main + k * ROUTE_TILE, ROUTE_TILE)
        acc_scr[...] += x
        gs_scr[...] += gs
        return carry

    lax.fori_loop(0, n_extra, extra, 0)
    xs_ref[0] = acc_scr[...].T.astype(BF16)
    gs = gs_scr[...]
    gslot_ref[0] = gs[0:1] + gs[1:2] + gs[2:3]


def _dispatch(first_tok, h_t, sel3, pos3, g3, capp):
    d, n = h_t.shape
    k_chunk = min(9 * ROUTE_TILE, n)
    row3 = pl.BlockSpec((1, 1, n), lambda e, w, ft: (e, 0, 0))
    return pl.pallas_call(
        functools.partial(_dispatch_kernel, k_chunk=k_chunk),
        grid_spec=pltpu.PrefetchScalarGridSpec(
            num_scalar_prefetch=1,
            grid=(N_EXPERTS, capp // ROUTE_WIN),
            in_specs=[pl.BlockSpec(memory_space=pl.ANY), row3, row3, row3],
            out_specs=[
                pl.BlockSpec((1, ROUTE_WIN, d), lambda e, w, ft: (e, w, 0)),
                pl.BlockSpec((1, 1, ROUTE_WIN), lambda e, w, ft: (e, 0, w)),
            ],
            scratch_shapes=[pltpu.VMEM((d, n), BF16), pltpu.VMEM((d, ROUTE_WIN), F32),
                            pltpu.VMEM((8, ROUTE_WIN), F32), pltpu.SemaphoreType.DMA],
        ),
        out_shape=[jax.ShapeDtypeStruct((N_EXPERTS, capp, d), BF16),
                   jax.ShapeDtypeStruct((N_EXPERTS, 1, capp), F32)],
        compiler_params=_params(("arbitrary", "arbitrary")),
        name="moe_dispatch",
    )(first_tok, h_t, sel3, pos3, g3)


def _ffn_up_kernel(xs_ref, wg_ref, wu_ref, h_ref):
    xs = xs_ref[0].astype(BF16)
    a = jnp.dot(xs, wg_ref[0].astype(BF16), preferred_element_type=F32)
    u = jnp.dot(xs, wu_ref[0].astype(BF16), preferred_element_type=F32)
    h_ref[0] = (a * jax.nn.sigmoid(a) * u).astype(BF16)


def _ffn_up(xs, w_gate, w_up, layer):
    e, capp, d = xs.shape
    f = w_gate.shape[3]
    tf = 512
    return pl.pallas_call(
        _ffn_up_kernel,
        grid=(e, f // tf),
        in_specs=[
            pl.BlockSpec((1, capp, d), lambda i, j: (i, 0, 0)),
            pl.BlockSpec((None, 1, d, tf), lambda i, j: (layer, i, 0, j)),
            pl.BlockSpec((None, 1, d, tf), lambda i, j: (layer, i, 0, j)),
        ],
        out_specs=pl.BlockSpec((1, capp, tf), lambda i, j: (i, 0, j)),
        out_shape=jax.ShapeDtypeStruct((e, capp, f), BF16),
        compiler_params=_params(("arbitrary", "arbitrary")),
        name="moe_ffn_up",
    )(xs, w_gate, w_up)


def _ffn_down_kernel(h_ref, wd_ref, gslot_ref, y_ref):
    y = jnp.dot(h_ref[0], wd_ref[0].astype(BF16), preferred_element_type=F32)
    y_ref[0] = (y.T * gslot_ref[0]).astype(BF16)


def _ffn_down(h, w_down, gslot, layer):
    e, capp, f = h.shape
    d = w_down.shape[3]
    td = 2048
    return pl.pallas_call(
        _ffn_down_kernel,
        grid=(e, d // td),
        in_specs=[
            pl.BlockSpec((1, capp, f), lambda i, j: (i, 0, 0)),
            pl.BlockSpec((None, 1, f, td), lambda i, j: (layer, i, 0, j)),
            pl.BlockSpec((1, 1, capp), lambda i, j: (i, 0, 0)),
        ],
        out_specs=pl.BlockSpec((1, td, capp), lambda i, j: (i, j, 0)),
        out_shape=jax.ShapeDtypeStruct((e, d, capp), BF16),
        compiler_params=_params(("arbitrary", "arbitrary")),
        name="moe_ffn_down",
    )(h, w_down, gslot)


COMBINE_TD = 1024
COMBINE_WIN = ROUTE_WIN


def _combine_kernel(ps_ref, pe_ref, y_ref, sel_ref, pos_ref, x_ref, gate_ref, o_ref, acc_scr):
    t = pl.program_id(1)
    n_exp, _, capp = y_ref.shape

    def window(e, k):
        start, _ = _window_plan(ps_ref[e, t], pe_ref[e, t], COMBINE_WIN)
        lo = start + k * COMBINE_WIN
        base = pl.multiple_of(jnp.minimum(lo, capp - COMBINE_WIN), LANES)
        onehot = _one_hot_window(pos_ref[e, pl.ds(t, 1), :], sel_ref[e, pl.ds(t, 1), :], lo, base, COMBINE_WIN)
        return [y_ref[e, :, pl.ds(base, COMBINE_WIN)]], [onehot]

    lhs, rhs = [], []
    for e in range(n_exp):
        rows, hots = window(e, 0)
        lhs += rows
        rhs += hots
    acc_scr[...] = jnp.dot(jnp.concatenate(lhs, axis=1), jnp.concatenate(rhs, axis=0), preferred_element_type=F32)

    for e in range(n_exp):
        n_chunks = _window_plan(ps_ref[e, t], pe_ref[e, t], COMBINE_WIN)[1]

        def extra(k, carry, e=e):
            rows, hots = window(e, k)
            acc_scr[...] += jnp.dot(jnp.concatenate(rows, axis=1), jnp.concatenate(hots, axis=0),
                                    preferred_element_type=F32)
            return carry

        lax.fori_loop(1, n_chunks, extra, 0)

    o_ref[...] = x_ref[...] + gate_ref[...] * acc_scr[...].T


def _combine(tile_start, tile_end, y_t, sel_t, pos_t, x, gate):
    e, d, capp = y_t.shape
    n = x.shape[0]
    n_tiles = n // ROUTE_TILE
    td = COMBINE_TD
    y_blk = pl.BlockSpec((e, td, capp), lambda j, t, ps, pe: (0, j, 0), pipeline_mode=pl.Buffered(1))
    full = pl.BlockSpec((e, n_tiles, ROUTE_TILE), lambda j, t, ps, pe: (0, 0, 0))
    tile = pl.BlockSpec((ROUTE_TILE, td), lambda j, t, ps, pe: (t, j))
    return pl.pallas_call(
        _combine_kernel,
        grid_spec=pltpu.PrefetchScalarGridSpec(
            num_scalar_prefetch=2,
            grid=(d // td, n_tiles),
            in_specs=[y_blk, full, full, tile, pl.BlockSpec((1, td), lambda j, t, ps, pe: (0, j))],
            out_specs=tile,
            scratch_shapes=[pltpu.VMEM((td, ROUTE_TILE), F32)],
        ),
        out_shape=jax.ShapeDtypeStruct((n, d), F32),
        compiler_params=_params(("arbitrary", "arbitrary")),
        name="moe_combine",
    )(tile_start, tile_end, y_t, sel_t, pos_t, x, gate)


def _tile_bounds(pos, tile, cap):
    start = pos[:, ::tile]
    end = jnp.concatenate([start[:, 1:], jnp.full((pos.shape[0], 1), cap, I32)], axis=1)
    return start, end


def _expert_choice_ffn(x, g, shift, scale, gate, w_router, w_gate, w_up, w_down, layer):
    n, d = x.shape
    cap = CAPACITY_FACTOR * n // N_EXPERTS
    capp = max(cap, ROUTE_WIN)
    h_t, aff_t = _router(x, g, shift, scale, w_router.T.astype(BF16))
    sel, pos, gsel, first_tok = _select(aff_t, cap, capp)
    as3 = lambda a: a.reshape(N_EXPERTS, 1, n)
    xs, gslot = _dispatch(first_tok, h_t, as3(sel), as3(pos), as3(gsel), capp)
    hidden = _ffn_up(xs, w_gate, w_up, layer)
    y_t = _ffn_down(hidden, w_down, gslot, layer)
    as_tiles = lambda a: a.reshape(N_EXPERTS, n // ROUTE_TILE, ROUTE_TILE)
    return _combine(*_tile_bounds(pos, ROUTE_TILE, cap), y_t, as_tiles(sel), as_tiles(pos), x, gate)


def _final_kernel(x_ref, g_ref, o_ref):
    x = x_ref[...]
    o_ref[...] = x * lax.rsqrt(jnp.mean(x * x, axis=-1, keepdims=True) + EPS) * g_ref[...]


def _final_norm(x, g_final):
    n, d = x.shape
    tm = min(512, n)
    return pl.pallas_call(
        _final_kernel,
        grid=(n // tm,),
        in_specs=[pl.BlockSpec((tm, d), lambda i: (i, 0)), pl.BlockSpec((1, d), lambda i: (0, 0))],
        out_specs=pl.BlockSpec((tm, d), lambda i: (i, 0)),
        out_shape=jax.ShapeDtypeStruct((n, d), F32),
        compiler_params=_params(("arbitrary",)),
        name="final_norm",
    )(x, g_final)


FFT_K1_BLK = 16


def _fourier_in_kernel(x_ref, g_ref, sh_ref, sc_ref, cs_ref, perm_ref, a_ref, b_ref):
    h = _norm_mod(x_ref[...], g_ref[...], sh_ref[...], sc_ref[...]).astype(BF16)
    hp = jnp.dot(perm_ref[...], h, preferred_element_type=F32).astype(BF16)
    n2, t1_blk = a_ref.shape[0], a_ref.shape[1]
    for g in range(C_GROUPS):
        cols = slice(g * C_GROUP_W, (g + 1) * C_GROUP_W)
        ab = jnp.dot(hp[:, cols], cs_ref[...], preferred_element_type=F32)
        a_ref[:, :, cols] = ab[:, :C_GROUP_W].astype(BF16).reshape(n2, t1_blk, C_GROUP_W)
        b_ref[:, :, cols] = ab[:, C_GROUP_W:].astype(BF16).reshape(n2, t1_blk, C_GROUP_W)


def _fourier_in(x, g, shift, scale, cs):
    n, d = x.shape
    tm = 512
    n1 = n // FFT_N2
    t1_blk = tm // FFT_N2
    row = lambda i: (0, 0)
    tile = pl.BlockSpec((tm, d), lambda i: (i, 0))
    ab_blk = pl.BlockSpec((FFT_N2, t1_blk, d), lambda i: (0, i, 0))
    ab_shape = jax.ShapeDtypeStruct((FFT_N2, n1, d), BF16)
    perm = np.zeros((tm, tm), np.float32)
    src = np.arange(tm)
    perm[(src % FFT_N2) * t1_blk + src // FFT_N2, src] = 1.0
    return pl.pallas_call(
        _fourier_in_kernel,
        grid=(n // tm,),
        in_specs=[tile, pl.BlockSpec((1, d), row), pl.BlockSpec((1, d), row), pl.BlockSpec((1, d), row),
                  pl.BlockSpec((C_GROUP_W, 2 * C_GROUP_W), row), pl.BlockSpec((tm, tm), row)],
        out_specs=[ab_blk, ab_blk],
        out_shape=[ab_shape, ab_shape],
        compiler_params=_params(("arbitrary",)),
        name="fourier_channel_dft",
    )(x, g, shift, scale, cs, jnp.asarray(perm, BF16))


def _fourier_stage1_kernel(a_ref, b_ref, ma_ref, mb_ref, ct_ref, st_ref, zr_ref, zi_ref):
    n1 = a_ref.shape[1]
    z = (jnp.dot(ma_ref[...], a_ref[0], preferred_element_type=F32)
         + jnp.dot(mb_ref[...], b_ref[0], preferred_element_type=F32))
    ct = ct_ref[0]
    st = st_ref[0]
    for c in range(a_ref.shape[2] // LANES):
        cols = slice(c * LANES, (c + 1) * LANES)
        zr = z[:n1, cols]
        zi = z[n1:, cols]
        zr_ref[0, :, cols] = (zr * ct + zi * st).astype(BF16)
        zi_ref[0, :, cols] = (zi * ct - zr * st).astype(BF16)


def _fourier_stage1(a3, b3, ma, mb, ct, st):
    n2, n1, d = a3.shape
    blk = pl.BlockSpec((1, n1, d), lambda j: (j, 0, 0))
    mat = pl.BlockSpec((2 * n1, n1), lambda j: (0, 0))
    tw = pl.BlockSpec((1, n1, LANES), lambda j: (j, 0, 0))
    shp = jax.ShapeDtypeStruct((n2, n1, d), BF16)
    return pl.pallas_call(
        _fourier_stage1_kernel,
        grid=(n2,),
        in_specs=[blk, blk, mat, mat, tw, tw],
        out_specs=[blk, blk],
        out_shape=[shp, shp],
        compiler_params=_params(("arbitrary",)),
        name="fourier_stage1",
    )(a3, b3, ma, mb, ct, st)


def _fourier_out_kernel(zr_ref, zi_ref, bc_ref, bs_ref, wo_ref, x_ref, gm_ref, o_ref):
    rows = zr_ref.shape[0] * zr_ref.shape[1]
    d = zr_ref.shape[2]
    zr = zr_ref[...].reshape(rows, d)
    zi = zi_ref[...].reshape(rows, d)
    f = (jnp.dot(bc_ref[...], zr, preferred_element_type=F32)
         + jnp.dot(bs_ref[...], zi, preferred_element_type=F32))
    y = jnp.dot(f.astype(BF16), wo_ref[...], preferred_element_type=F32)
    o_ref[...] = x_ref[...] + gm_ref[...] * y.reshape(o_ref.shape)


def _fourier_out(zr3, zi3, bd_c, bd_s, w_o_bf16, x, gm):
    n2, n1, d = zr3.shape
    n = n1 * n2
    rows = n2 * FFT_K1_BLK
    x3 = x.reshape(n2, n1, d)
    blk = pl.BlockSpec((n2, FFT_K1_BLK, d), lambda i: (0, i, 0))
    const = lambda i: (0, 0)
    out = pl.pallas_call(
        _fourier_out_kernel,
        grid=(n1 // FFT_K1_BLK,),
        in_specs=[blk, blk, pl.BlockSpec((rows, rows), const), pl.BlockSpec((rows, rows), const),
                  pl.BlockSpec((d, d), const), blk, pl.BlockSpec((1, d), const)],
        out_specs=blk,
        out_shape=jax.ShapeDtypeStruct((n2, n1, d), F32),
        compiler_params=_params(("arbitrary",)),
        name="fourier_stage2_out",
    )(zr3, zi3, bd_c, bd_s, w_o_bf16, x3, gm)
    return out.reshape(n, d)


def _fourier_constants(n):
    n1, n2 = n // FFT_N2, FFT_N2
    two_pi = 2.0 * np.pi

    def angles(a, b, period):
        return two_pi * ((np.outer(a, b) % period).astype(np.float64) / period)

    kc = np.arange(C_GROUP_W)
    ang = angles(kc, kc, C_GROUP_W)
    cs = np.concatenate([np.cos(ang), np.sin(ang)], axis=1) / np.sqrt(C_GROUP_W)
    k1 = np.arange(n1)
    ang1 = angles(k1, k1, n1)
    c1, s1 = np.cos(ang1) / np.sqrt(n), np.sin(ang1) / np.sqrt(n)
    ma = np.concatenate([c1, -s1], axis=0)
    mb = np.concatenate([-s1, -c1], axis=0)
    t2 = np.arange(n2)
    angt = angles(t2, k1, n)
    ct = np.repeat(np.cos(angt)[:, :, None], LANES, axis=2)
    st = np.repeat(np.sin(angt)[:, :, None], LANES, axis=2)
    ang2 = angles(t2, t2, n2)
    k1_blk = FFT_K1_BLK
    bd_c = np.zeros((n2 * k1_blk, n2 * k1_blk))
    bd_s = np.zeros((n2 * k1_blk, n2 * k1_blk))
    for kl in range(k1_blk):
        bd_c[kl::k1_blk, kl::k1_blk] = np.cos(ang2)
        bd_s[kl::k1_blk, kl::k1_blk] = np.sin(ang2)
    bf = lambda a: jnp.asarray(a, F32).astype(BF16)
    return bf(cs), bf(ma), bf(mb), jnp.asarray(ct, F32), jnp.asarray(st, F32), bf(bd_c), bf(bd_s)


def _fourier_mix_layer(x, g, shift, scale, gm, w_o):
    cs, ma, mb, ct, st, bd_c, bd_s = _fourier_constants(x.shape[0])
    a3, b3 = _fourier_in(x, g, shift, scale, cs)
    zr3, zi3 = _fourier_stage1(a3, b3, ma, mb, ct, st)
    return _fourier_out(zr3, zi3, bd_c, bd_s, w_o.astype(BF16), x, gm)


def _even_layer_mix(x, ctx, mods, g_mix, w_in, w_out, lamv, g_subln, ln_g, ln_b, w_s, b_s, lam_init, need_ctx_out):
    sm, cm, gm = mods["lat"][0:3]
    smc, cmc, gmc = mods["ctx"][0:3]
    n = x.shape[0]
    w_in_b = w_in.astype(BF16)
    w_out_b = w_out.astype(BF16)
    ws_b = w_s.astype(BF16)
    bs_b = jnp.broadcast_to(b_s[:, :, None], (B_GROUPS, CHUNK, B_GROUP_W))
    cos_l, sin_l = _rope_tables(n, True)
    cos_c, sin_c = _rope_tables(ctx.shape[0], False)
    q_l, k_l, vt_l, uv_l = _in_projection(x, g_mix, sm, cm, w_in_b, cos_l, sin_l)
    q_c, k_c, vt_c, uv_c = _in_projection(ctx, g_mix, smc, cmc, w_in_b, cos_c, sin_c)
    a_l = _diff_attention(q_l, [k_c, k_l], [vt_c, vt_l], lamv, g_subln, lam_init)
    x = _finish_even(a_l, uv_l, x, w_out_b, ws_b, bs_b, ln_g, ln_b, gm)
    if need_ctx_out:
        a_c = _diff_attention(q_c, [k_c], [vt_c], lamv, g_subln, lam_init)
        ctx = _finish_even(a_c, uv_c, ctx, w_out_b, ws_b, bs_b, ln_g, ln_b, gmc)
    return x, ctx


def kernel(x, c, ctx, c_ctx, w_mod, b_mod, g_norm_mix, g_norm_ffn, w_in, w_out, lam_q1, lam_k1, lam_q2, lam_k2,
           g_subln, sgu_ln_g, sgu_ln_b, w_spatial, b_spatial, w_fourier_out, w_router, w_gate, w_up, w_down, g_final):
    assert x.shape[0] == 1 and DEPTH == 2
    d = D_MODEL
    x2 = x[0]
    ctx2 = ctx[0]
    c8 = jnp.zeros((8, d), F32).at[0].set(c[0]).at[1].set(c_ctx)
    mod_all = _modulation(c8, w_mod, b_mod)
    row = lambda v: v.reshape(1, -1)

    def mods_of(i):
        lat = [mod_all[i, 0:1, k * d:(k + 1) * d] for k in range(6)]
        cx = [mod_all[i, 1:2, k * d:(k + 1) * d] for k in range(6)]
        return {"lat": lat, "ctx": cx}

    m0 = mods_of(0)
    lam_init0 = 0.8 - 0.6 * math.exp(-0.3 * 0)
    lamv = jnp.zeros((8, LANES), F32)
    for r, v in enumerate((lam_q1[0], lam_k1[0], lam_q2[0], lam_k2[0])):
        lamv = lamv.at[r, :A_HEAD_DIM].set(v)
    x2, ctx2 = _even_layer_mix(x2, ctx2, m0, row(g_norm_mix[0]), w_in[0], w_out[0], lamv, row(g_subln[0]),
                               row(sgu_ln_g[0]), row(sgu_ln_b[0]), w_spatial[0], b_spatial[0], lam_init0, True)
    x2 = _expert_choice_ffn(x2, row(g_norm_ffn[0]), m0["lat"][3], m0["lat"][4], m0["lat"][5],
                            w_router[0], w_gate, w_up, w_down, 0)
    ctx2 = _expert_choice_ffn(ctx2, row(g_norm_ffn[0]), m0["ctx"][3], m0["ctx"][4], m0["ctx"][5],
                              w_router[0], w_gate, w_up, w_down, 0)

    m1 = mods_of(1)
    x2 = _fourier_mix_layer(x2, row(g_norm_mix[1]), m1["lat"][0], m1["lat"][1], m1["lat"][2], w_fourier_out[0])
    x2 = _expert_choice_ffn(x2, row(g_norm_ffn[1]), m1["lat"][3], m1["lat"][4], m1["lat"][5],
                            w_router[1], w_gate, w_up, w_down, 1)
    out = _final_norm(x2, row(g_final))
    del ctx2
    return out[None]
```
